```python
import math
import jax, jax.numpy as jnp
from jax import lax
import numpy as np

D_MODEL = 2048
BATCH = 2
SEQ = 16384
DEPTH = 2

N_MIXERS = 2
N_ATTN_LAYERS = (DEPTH + N_MIXERS - 1) // N_MIXERS
N_DN_LAYERS = DEPTH // N_MIXERS

DEEPNORM_ALPHA = (2.0 * DEPTH) ** 0.25
DEEPNORM_BETA = (8.0 * DEPTH) ** -0.25
LN_EPS = 1e-5

ATTN_HEAD_DIM = 128
ATTN_HEADS_PER_GROUP = 8
DILATED_GROUPS = ((128, 1), (512, 4), (2048, 16))
N_GROUPS = len(DILATED_GROUPS)
ATTN_WIDTH = ATTN_HEADS_PER_GROUP * ATTN_HEAD_DIM
ATTN_IN_WIDTH = N_GROUPS * 3 * ATTN_WIDTH
ATTN_BLOCK = 128

NUM_BUCKETS = 32
MAX_DISTANCE = 2048

DN_QK_HEADS = 16
DN_V_HEADS = 32
DN_HEAD_K = 128
DN_HEAD_V = 128
DN_CONV = 4
DN_CHUNK = 64
DN_Q_W = DN_QK_HEADS * DN_HEAD_K
DN_V_W = DN_V_HEADS * DN_HEAD_V
DN_CONV_W = 2 * DN_Q_W + DN_V_W
DN_IN_WIDTH = DN_CONV_W + DN_V_W + 2 * DN_V_HEADS
DN_EPS = 1e-6

N_EXPERTS = 32
TOP_K = 4
D_FF = 2048
SWIGLU_LIMIT = 7.0
SWIGLU_ALPHA = 1.702
MOE_BLOCK = 256

kernel_name = 'hybrid_dilated_attn_gdn_moe_deepnorm'


def layer_norm(x, g, b):
    xf = x.astype(jnp.float32)
    mu = jnp.mean(xf, axis=-1, keepdims=True)
    var = jnp.mean(jnp.square(xf - mu), axis=-1, keepdims=True)
    return ((xf - mu) * lax.rsqrt(var + LN_EPS) * g + b).astype(x.dtype)


def t5_bucket(dist):
    max_exact = NUM_BUCKETS // 2
    d_f = jnp.maximum(dist, 1).astype(jnp.float32)
    large = max_exact + (jnp.log(d_f / max_exact) / math.log(MAX_DISTANCE / max_exact)
                         * (NUM_BUCKETS - max_exact)).astype(jnp.int32)
    large = jnp.minimum(large, NUM_BUCKETS - 1)
    return jnp.where(dist < max_exact, dist, large)


def dilated_group_attention(q, k, v, bias_table, window, dilation):
    B_, S_, H, HD = q.shape
    d = dilation
    J = window // d
    L = S_ // d
    bq = min(ATTN_BLOCK, L)
    nb = -(-L // bq)
    Lp = nb * bq
    n_prev = -(-J // bq)

    def streams(t):
        return jnp.transpose(t.reshape(B_, L, d, H, HD), (0, 2, 1, 3, 4))

    qs = jnp.pad(streams(q), ((0, 0), (0, 0), (0, Lp - L), (0, 0), (0, 0)))
    qs = qs.reshape(B_, d, nb, bq, H, HD)
    kv_pad = ((0, 0), (0, 0), (n_prev * bq, Lp - L), (0, 0), (0, 0))
    ks = jnp.pad(streams(k), kv_pad).reshape(B_, d, nb + n_prev, bq, H, HD)
    vs = jnp.pad(streams(v), kv_pad).reshape(B_, d, nb + n_prev, bq, H, HD)
    kw = jnp.concatenate([ks[:, :, i:i + nb] for i in range(n_prev + 1)], axis=3)
    vw = jnp.concatenate([vs[:, :, i:i + nb] for i in range(n_prev + 1)], axis=3)
    W = (n_prev + 1) * bq

    a_idx = jnp.arange(bq, dtype=jnp.int32)
    c_idx = jnp.arange(W, dtype=jnp.int32)
    steps = a_idx[:, None] + n_prev * bq - c_idx[None, :]
    k_pos = (jnp.arange(nb, dtype=jnp.int32)[:, None, None] * bq - n_prev * bq
             + c_idx[None, None, :])
    valid = (steps >= 0)[None] & (steps <= J)[None] & (k_pos >= 0)
    bucket = t5_bucket(jnp.maximum(steps, 0) * d)
    bias = jnp.transpose(bias_table[bucket], (2, 0, 1)).astype(jnp.float32)

    logits = jnp.einsum('brnqhd,brnkhd->brnhqk', qs, kw).astype(jnp.float32) * (HD ** -0.5) + bias
    logits = jnp.where(valid[:, None], logits, -jnp.inf)
    lse = jax.nn.logsumexp(logits, axis=-1)
    p = jnp.exp(logits - lse[..., None])
    o = jnp.einsum('brnhqk,brnkhd->brnqhd', p.astype(v.dtype), vw)

    o = o.reshape(B_, d, Lp, H, HD)[:, :, :L]
    o = jnp.transpose(o, (0, 2, 1, 3, 4)).reshape(B_, S_, H, HD)
    lse = jnp.transpose(lse, (0, 1, 2, 4, 3)).reshape(B_, d, Lp, H)[:, :, :L]
    lse = jnp.transpose(lse, (0, 2, 1, 3)).reshape(B_, S_, H)
    return o, lse


def dilated_attention_mixer(x, w_in, w_out, rel_bias):
    B_, S_, _ = x.shape
    H = ATTN_HEADS_PER_GROUP
    proj = x @ w_in
    outs, lses = [], []
    for g, (window, dilation) in enumerate(DILATED_GROUPS):
        qkv = proj[..., g * 3 * ATTN_WIDTH:(g + 1) * 3 * ATTN_WIDTH].reshape(B_, S_, 3, H, ATTN_HEAD_DIM)
        o, lse = dilated_group_attention(qkv[:, :, 0], qkv[:, :, 1], qkv[:, :, 2],
                                         rel_bias[:, g * H:(g + 1) * H], window, dilation)
        outs.append(o)
        lses.append(lse)
    wts = jax.nn.softmax(jnp.stack(lses), axis=0)
    o = jnp.einsum('gbsh,gbshd->bshd', wts, jnp.stack(outs).astype(jnp.float32)).astype(x.dtype)
    return o.reshape(B_, S_, ATTN_WIDTH) @ w_out


def causal_short_conv(x, w):
    K = w.shape[0]
    S_ = x.shape[1]
    xp = jnp.pad(x, ((0, 0), (K - 1, 0), (0, 0)))
    return sum(xp[:, j:j + S_] * w[j] for j in range(K))


def l2_normalize(t):
    tf = t.astype(jnp.float32)
    return tf * lax.rsqrt(jnp.sum(tf * tf, axis=-1, keepdims=True) + DN_EPS)


def chunk_gated_delta_rule(q, k, v, g, beta):
    B_, S_, H, DK = q.shape
    DV = v.shape[-1]
    C = DN_CHUNK
    N = S_ // C

    def chunks(t):
        t = t.astype(jnp.float32).reshape((B_, N, C, H) + t.shape[3:])
        return jnp.moveaxis(t, 3, 1)

    qc = chunks(q) * (DK ** -0.5)
    kc = chunks(k)
    vc = chunks(v)
    bc = chunks(beta)
    gc = jnp.cumsum(chunks(g), axis=-1)
    incl = jnp.tril(jnp.ones((C, C), dtype=bool))
    strict = jnp.tril(jnp.ones((C, C), dtype=bool), -1)
    decay = jnp.exp(jnp.where(incl, gc[..., :, None] - gc[..., None, :], -jnp.inf))
    kb = kc * bc[..., None]
    a_mat = jnp.where(strict, jnp.einsum('bhnid,bhnjd->bhnij', kb, kc) * decay, 0.0)
    eye = jnp.eye(C, dtype=jnp.float32)
    rhs = jnp.concatenate([vc * bc[..., None], kb * jnp.exp(gc)[..., None]], axis=-1)
    uw = lax.linalg.triangular_solve(eye + a_mat, rhs, left_side=True, lower=True, unit_diagonal=True)
    u, w = uw[..., :DV], uw[..., DV:]
    qk = jnp.einsum('bhnid,bhnjd->bhnij', qc, kc) * decay

    def step(state, inp):
        q_i, k_i, u_i, w_i, qk_i, g_i = inp
        v_new = u_i - jnp.einsum('bhck,bhkv->bhcv', w_i, state)
        o_i = (jnp.einsum('bhck,bhkv->bhcv', q_i * jnp.exp(g_i)[..., None], state)
               + jnp.einsum('bhij,bhjv->bhiv', qk_i, v_new))
        g_last = g_i[..., -1]
        k_dec = k_i * jnp.exp(g_last[..., None] - g_i)[..., None]
        state = state * jnp.exp(g_last)[..., None, None] + jnp.einsum('bhck,bhcv->bhkv', k_dec, v_new)
        return state, o_i

    xs = tuple(jnp.moveaxis(t, 2, 0) for t in (qc, kc, u, w, qk, gc))
    state0 = jnp.zeros((B_, H, DK, DV), jnp.float32)
    _, o = lax.scan(step, state0, xs)
    return jnp.transpose(o, (1, 0, 3, 2, 4)).reshape(B_, S_, H, DV)


def gated_deltanet_mixer(x, w_in, conv_w, a_log, dt_bias, norm_g, w_out):
    B_, S_, _ = x.shape
    proj = x @ w_in
    o1 = DN_CONV_W
    o2 = o1 + DN_V_W
    o3 = o2 + DN_V_HEADS
    qkv = jax.nn.silu(causal_short_conv(proj[..., :o1], conv_w))
    z = proj[..., o1:o2].reshape(B_, S_, DN_V_HEADS, DN_HEAD_V)
    b_raw = proj[..., o2:o3]
    a_raw = proj[..., o3:]
    rep = DN_V_HEADS // DN_QK_HEADS
    q = jnp.repeat(l2_normalize(qkv[..., :DN_Q_W].reshape(B_, S_, DN_QK_HEADS, DN_HEAD_K)), rep, axis=2)
    k = jnp.repeat(l2_normalize(qkv[..., DN_Q_W:2 * DN_Q_W].reshape(B_, S_, DN_QK_HEADS, DN_HEAD_K)), rep, axis=2)
    v = qkv[..., 2 * DN_Q_W:].reshape(B_, S_, DN_V_HEADS, DN_HEAD_V)
    beta = jax.nn.sigmoid(b_raw.astype(jnp.float32))
    g = -jnp.exp(a_log.astype(jnp.float32)) * jax.nn.softplus(a_raw.astype(jnp.float32)
                                                           + dt_bias.astype(jnp.float32))
    o = chunk_gated_delta_rule(q, k, v, g, beta)
    o = o * lax.rsqrt(jnp.mean(o * o, axis=-1, keepdims=True) + DN_EPS) * norm_g
    o = o * jax.nn.silu(z.astype(jnp.float32))
    return o.astype(x.dtype).reshape(B_, S_, DN_V_W) @ w_out


def clamped_swiglu(h):
    glu = jnp.minimum(h[..., :D_FF], SWIGLU_LIMIT)
    lin = jnp.clip(h[..., D_FF:], -SWIGLU_LIMIT, SWIGLU_LIMIT)
    return glu * jax.nn.sigmoid(SWIGLU_ALPHA * glu) * (lin + 1.0)


def moe_ffn(x, router_w, router_b, w_gate_up, b_gate_up, w_down, b_down):
    B_, S_, D_ = x.shape
    n_tok = B_ * S_
    x2d = x.reshape(n_tok, D_)
    logits = (x2d @ router_w).astype(jnp.float32) + router_b.astype(jnp.float32)
    top_logits, top_idx = lax.top_k(logits, TOP_K)
    gates = jax.nn.softmax(top_logits, axis=-1)
    n_pairs = n_tok * TOP_K
    flat_e = top_idx.reshape(n_pairs).astype(jnp.int32)
    flat_tok = jnp.repeat(jnp.arange(n_tok, dtype=jnp.int32), TOP_K)
    flat_gate = gates.reshape(n_pairs)
    order = jnp.argsort(flat_e)
    e_sorted = flat_e[order]
    counts = jnp.bincount(flat_e, length=N_EXPERTS).astype(jnp.int32)
    starts = jnp.cumsum(counts) - counts
    padded = (counts + MOE_BLOCK - 1) // MOE_BLOCK * MOE_BLOCK
    padded_end = jnp.cumsum(padded)
    padded_start = padded_end - padded
    dest = padded_start[e_sorted] + (jnp.arange(n_pairs, dtype=jnp.int32) - starts[e_sorted])
    n_blocks = -(-(n_pairs + N_EXPERTS * (MOE_BLOCK - 1)) // MOE_BLOCK)
    n_slots = n_blocks * MOE_BLOCK
    slot_tok = jnp.full((n_slots,), n_tok, jnp.int32).at[dest].set(flat_tok[order])
    slot_gate = jnp.zeros((n_slots,), jnp.float32).at[dest].set(flat_gate[order])
    block_expert = jnp.minimum(
        jnp.searchsorted(padded_end, jnp.arange(n_blocks, dtype=jnp.int32) * MOE_BLOCK, side='right'),
        N_EXPERTS - 1)

    def expert_block(args):
        tok_b, gate_b, e = args
        xb = x2d.at[tok_b].get(mode='fill', fill_value=0)
        h = xb @ w_gate_up[e] + b_gate_up[e]
        yb = clamped_swiglu(h) @ w_down[e] + b_down[e]
        return yb * gate_b[:, None].astype(yb.dtype)

    ys = lax.map(expert_block, (slot_tok.reshape(n_blocks, MOE_BLOCK),
                                slot_gate.reshape(n_blocks, MOE_BLOCK), block_expert))
    y = jax.ops.segment_sum(ys.reshape(n_slots, D_), slot_tok, num_segments=n_tok + 1)[:n_tok]
    return y.reshape(B_, S_, D_)


def setup_inputs(seed: int = 0) -> dict:
    key = jax.random.key(seed)
    ks = jax.random.split(key, 20)
    f32 = jnp.float32

    def nrm(k, shape, scale):
        return jax.random.normal(k, shape, f32) * scale

    x = nrm(ks[0], (BATCH, SEQ, D_MODEL), 1.0)
    rel_bias = nrm(ks[1], (NUM_BUCKETS, N_GROUPS * ATTN_HEADS_PER_GROUP), 0.2)
    attn_w_in = nrm(ks[2], (N_ATTN_LAYERS, D_MODEL, ATTN_IN_WIDTH), D_MODEL ** -0.5)
    attn_w_out = nrm(ks[3], (N_ATTN_LAYERS, ATTN_WIDTH, D_MODEL), DEEPNORM_BETA * ATTN_WIDTH ** -0.5)
    dn_w_in = nrm(ks[4], (N_DN_LAYERS, D_MODEL, DN_IN_WIDTH), D_MODEL ** -0.5)
    dn_conv_w = nrm(ks[5], (N_DN_LAYERS, DN_CONV, DN_CONV_W), DN_CONV ** -0.5)
    dn_a_log = jnp.log(jax.random.uniform(ks[6], (N_DN_LAYERS, DN_V_HEADS), f32, 1.0, 16.0))
    dt = jnp.exp(jax.random.uniform(ks[7], (N_DN_LAYERS, DN_V_HEADS), f32,
                                    math.log(1e-3), math.log(1e-1)))
    dn_dt_bias = dt + jnp.log(-jnp.expm1(-dt))
    dn_norm_g = 1.0 + nrm(ks[8], (N_DN_LAYERS, DN_HEAD_V), 0.02)
    dn_w_out = nrm(ks[9], (N_DN_LAYERS, DN_V_W, D_MODEL), DEEPNORM_BETA * DN_V_W ** -0.5)
    ln1_g = 1.0 + nrm(ks[10], (DEPTH, D_MODEL), 0.02)
    ln1_b = nrm(ks[11], (DEPTH, D_MODEL), 0.02)
    router_w = nrm(ks[12], (DEPTH, D_MODEL, N_EXPERTS), D_MODEL ** -0.5)
    router_b = nrm(ks[13], (DEPTH, N_EXPERTS), 0.01)
    w_gate_up = nrm(ks[14], (DEPTH, N_EXPERTS, D_MODEL, 2 * D_FF), D_MODEL ** -0.5)
    b_gate_up = nrm(ks[15], (DEPTH, N_EXPERTS, 2 * D_FF), 0.01)
    w_down = nrm(ks[16], (DEPTH, N_EXPERTS, D_FF, D_MODEL), DEEPNORM_BETA * D_FF ** -0.5)
    b_down = nrm(ks[17], (DEPTH, N_EXPERTS, D_MODEL), 0.01)
    ln2_g = 1.0 + nrm(ks[18], (DEPTH, D_MODEL), 0.02)
    ln2_b = nrm(ks[19], (DEPTH, D_MODEL), 0.02)
    return {'x': x, 'rel_bias': rel_bias, 'attn_w_in': attn_w_in, 'attn_w_out': attn_w_out,
            'dn_w_in': dn_w_in, 'dn_conv_w': dn_conv_w, 'dn_a_log': dn_a_log, 'dn_dt_bias': dn_dt_bias,
            'dn_norm_g': dn_norm_g, 'dn_w_out': dn_w_out, 'ln1_g': ln1_g, 'ln1_b': ln1_b,
            'router_w': router_w, 'router_b': router_b, 'w_gate_up': w_gate_up, 'b_gate_up': b_gate_up,
            'w_down': w_down, 'b_down': b_down, 'ln2_g': ln2_g, 'ln2_b': ln2_b}


def reference(x, rel_bias, attn_w_in, attn_w_out, dn_w_in, dn_conv_w, dn_a_log, dn_dt_bias,
              dn_norm_g, dn_w_out, ln1_g, ln1_b, router_w, router_b, w_gate_up, b_gate_up,
              w_down, b_down, ln2_g, ln2_b):
    for i in range(DEPTH):
        j = i // N_MIXERS
        if i % N_MIXERS == 0:
            h = dilated_attention_mixer(x, attn_w_in[j], attn_w_out[j], rel_bias)
        else:
            h = gated_deltanet_mixer(x, dn_w_in[j], dn_conv_w[j], dn_a_log[j], dn_dt_bias[j],
                                     dn_norm_g[j], dn_w_out[j])
        x = layer_norm(DEEPNORM_ALPHA * x + h, ln1_g[i], ln1_b[i])
        f = moe_ffn(x, router_w[i], router_b[i], w_gate_up[i], b_gate_up[i], w_down[i], b_down[i])
        x = layer_norm(DEEPNORM_ALPHA * x + f, ln2_g[i], ln2_b[i])
    return x
```

```python
import functools
import math

import jax
import jax.numpy as jnp
from jax import lax
from jax.experimental import pallas as pl
from jax.experimental.pallas import tpu as pltpu

F32 = jnp.float32
BF16 = jnp.bfloat16

DEPTH = 2
DEEPNORM_ALPHA = (2.0 * DEPTH) ** 0.25
LN_EPS = 1e-5
HEAD_DIM = 128
ATTN_HEADS = 8
ATTN_WIDTH = ATTN_HEADS * HEAD_DIM
DILATED_GROUPS = ((128, 1), (512, 4), (2048, 16))
ATTN_BLOCK = 128
MAX_DISTANCE = 2048
DN_QK_HEADS = 16
DN_V_HEADS = 32
DN_CONV = 4
DN_CHUNK = 64
DN_EPS = 1e-6
TOP_K = 4
SWIGLU_LIMIT = 7.0
SWIGLU_ALPHA = 1.702

MOE_ROWS = 512
MOE_FF_CHUNK = 512
VMEM_LIMIT = 56 * 1024 * 1024


def _params(*semantics):
    return pltpu.CompilerParams(dimension_semantics=semantics, vmem_limit_bytes=VMEM_LIMIT)


def _matmul_kernel(a_ref, w_ref, o_ref):
    o_ref[...] = jnp.dot(a_ref[...].astype(BF16), w_ref[...],
                         preferred_element_type=F32).astype(o_ref.dtype)


def _matmul(a, w, tm, tn, out_dtype):
    M, K = a.shape
    N = w.shape[1]
    return pl.pallas_call(
        _matmul_kernel,
        grid=(N // tn, M // tm),
        in_specs=[pl.BlockSpec((tm, K), lambda n, m: (m, 0)),
                  pl.BlockSpec((K, tn), lambda n, m: (0, n))],
        out_specs=pl.BlockSpec((tm, tn), lambda n, m: (m, n)),
        out_shape=jax.ShapeDtypeStruct((M, N), out_dtype),
        compiler_params=_params("arbitrary", "arbitrary"),
        name="dense_proj",
    )(a, w)


def _t5_bucket(dist, num_buckets):
    max_exact = num_buckets // 2
    d_f = jnp.maximum(dist, 1).astype(F32)
    large = max_exact + (jnp.log(d_f / max_exact) / math.log(MAX_DISTANCE / max_exact)
                         * (num_buckets - max_exact)).astype(jnp.int32)
    large = jnp.minimum(large, num_buckets - 1)
    return jnp.where(dist < max_exact, dist, large)


def _attn_bias(rel_bias_g, dilation):
    bq = ATTN_BLOCK
    a = jnp.arange(bq, dtype=jnp.int32)[:, None]
    c = jnp.arange(2 * bq, dtype=jnp.int32)[None, :]
    steps = a + bq - c
    valid = (steps >= 0) & (steps <= bq)
    bucket = _t5_bucket(jnp.maximum(steps, 0) * dilation, rel_bias_g.shape[0])
    bias = jnp.transpose(rel_bias_g[bucket], (2, 0, 1)).astype(F32)
    return jnp.where(valid[None], bias, -jnp.inf)


def _attn_kernel(q_ref, kp_ref, kc_ref, vp_ref, vc_ref, bias_ref, o_ref, lse_ref):
    bq = ATTN_BLOCK
    first = pl.program_id(2) == 0
    scale = HEAD_DIM ** -0.5
    nt = (((1,), (1,)), ((), ()))
    for h in range(ATTN_HEADS):
        sl = slice(h * HEAD_DIM, (h + 1) * HEAD_DIM)
        q = q_ref[0, :, sl]
        s_p = lax.dot_general(q, kp_ref[0, :, sl], nt, preferred_element_type=F32) * scale + bias_ref[h, :, :bq]
        s_c = lax.dot_general(q, kc_ref[0, :, sl], nt, preferred_element_type=F32) * scale + bias_ref[h, :, bq:]
        s_p = jnp.where(first, -jnp.inf, s_p)
        m = jnp.maximum(jnp.max(s_p, axis=-1, keepdims=True), jnp.max(s_c, axis=-1, keepdims=True))
        e_p = jnp.exp(s_p - m)
        e_c = jnp.exp(s_c - m)
        l = jnp.sum(e_p, axis=-1, keepdims=True) + jnp.sum(e_c, axis=-1, keepdims=True)
        inv = 1.0 / l
        o = (jnp.dot((e_p * inv).astype(BF16), vp_ref[0, :, sl], preferred_element_type=F32)
             + jnp.dot((e_c * inv).astype(BF16), vc_ref[0, :, sl], preferred_element_type=F32))
        o_ref[0, :, sl] = o
        lse_ref[0, :, sl] = jnp.broadcast_to(m + jnp.log(l), (bq, HEAD_DIM))


def _dilated_group(proj, bias, g, dilation):
    B, S, W = proj.shape
    d = dilation
    L = S // d
    nb = L // ATTN_BLOCK
    ncol = W // ATTN_WIDTH
    pv = proj.reshape(B, L, d * W)
    blk = (1, ATTN_BLOCK, ATTN_WIDTH)

    def spec(kind, prev):
        if prev:
            return pl.BlockSpec(blk, lambda b, r, i: (b, jnp.maximum(i - 1, 0), r * ncol + g * 3 + kind))
        return pl.BlockSpec(blk, lambda b, r, i: (b, i, r * ncol + g * 3 + kind))

    out_spec = pl.BlockSpec(blk, lambda b, r, i: (b, i, r))
    out_sds = jax.ShapeDtypeStruct((B, L, d * ATTN_WIDTH), F32)
    o, lse = pl.pallas_call(
        _attn_kernel,
        grid=(B, d, nb),
        in_specs=[spec(0, False), spec(1, True), spec(1, False), spec(2, True), spec(2, False),
                  pl.BlockSpec(bias.shape, lambda b, r, i: (0, 0, 0))],
        out_specs=[out_spec, out_spec],
        out_shape=[out_sds, out_sds],
        compiler_params=_params("arbitrary", "arbitrary", "arbitrary"),
        name=f"dilated_attn_g{g}",
    )(pv, pv, pv, pv, pv, bias)
    return o.reshape(B, S, ATTN_WIDTH), lse.reshape(B, S, ATTN_WIDTH)


def _layer_norm(y, g_ref, b_ref):
    mu = jnp.mean(y, axis=-1, keepdims=True)
    yc = y - mu
    var = jnp.mean(yc * yc, axis=-1, keepdims=True)
    return yc * lax.rsqrt(var + LN_EPS) * g_ref[...] + b_ref[...]


def _residual_ln_router(h, x_ref, g_ref, b_ref, rwt_ref, rb_ref, x1_ref, x1b_ref, topi_ref, gate_ref):
    xn = _layer_norm(DEEPNORM_ALPHA * x_ref[...] + h, g_ref, b_ref)
    x1_ref[...] = xn
    xb = xn.astype(BF16)
    x1b_ref[...] = xb
    logits = lax.dot_general(rwt_ref[...], xb, (((1,), (1,)), ((), ())),
                             preferred_element_type=F32) + rb_ref[...]
    n_exp = logits.shape[0]
    eidx = lax.broadcasted_iota(jnp.int32, logits.shape, 0)
    vals = []
    for k in range(TOP_K):
        mx = jnp.max(logits, axis=0, keepdims=True)
        idx = jnp.min(jnp.where(logits == mx, eidx, n_exp), axis=0, keepdims=True)
        topi_ref[k:k + 1, :] = idx
        vals.append(mx)
        logits = jnp.where(eidx == idx, -jnp.inf, logits)
    exps = [jnp.exp(v - vals[0]) for v in vals]
    denom = exps[0] + exps[1] + exps[2] + exps[3]
    for k in range(TOP_K):
        gate_ref[k:k + 1, :] = exps[k] / denom


def _attn_out_kernel(o0_ref, o1_ref, o2_ref, l0_ref, l1_ref, l2_ref, w_ref, x_ref, g_ref, b_ref,
                     rwt_ref, rb_ref, x1_ref, x1b_ref, topi_ref, gate_ref):
    l0, l1, l2 = l0_ref[...], l1_ref[...], l2_ref[...]
    m = jnp.maximum(jnp.maximum(l0, l1), l2)
    e0, e1, e2 = jnp.exp(l0 - m), jnp.exp(l1 - m), jnp.exp(l2 - m)
    den = e0 + e1 + e2
    o = (e0 / den) * o0_ref[...] + (e1 / den) * o1_ref[...] + (e2 / den) * o2_ref[...]
    h = jnp.dot(o.astype(BF16), w_ref[...], preferred_element_type=F32)
    _residual_ln_router(h, x_ref, g_ref, b_ref, rwt_ref, rb_ref, x1_ref, x1b_ref, topi_ref, gate_ref)


def _dn_out_kernel(o_ref, w_ref, x_ref, g_ref, b_ref, rwt_ref, rb_ref,
                   x1_ref, x1b_ref, topi_ref, gate_ref, acc_ref):
    kt = pl.program_id(1)
    part = jnp.dot(o_ref[...], w_ref[...], preferred_element_type=F32)

    @pl.when(kt == 0)
    def _():
        acc_ref[...] = part

    @pl.when(kt > 0)
    def _():
        acc_ref[...] += part

    @pl.when(kt == pl.num_programs(1) - 1)
    def _():
        _residual_ln_router(acc_ref[...], x_ref, g_ref, b_ref, rwt_ref, rb_ref,
                            x1_ref, x1b_ref, topi_ref, gate_ref)


def _post_specs(tm, D, E, idx):
    row = lambda *a: (idx(*a), 0)
    const = lambda *a: (0, 0)
    col = lambda *a: (0, idx(*a))
    ins = [pl.BlockSpec((tm, D), row),
           pl.BlockSpec((1, D), const), pl.BlockSpec((1, D), const),
           pl.BlockSpec((E, D), const), pl.BlockSpec((E, 1), const)]
    outs = [pl.BlockSpec((tm, D), row), pl.BlockSpec((tm, D), row),
            pl.BlockSpec((TOP_K, tm), col), pl.BlockSpec((TOP_K, tm), col)]
    return ins, outs


def _post_shapes(N, D):
    return [jax.ShapeDtypeStruct((N, D), F32), jax.ShapeDtypeStruct((N, D), BF16),
            jax.ShapeDtypeStruct((TOP_K, N), jnp.int32), jax.ShapeDtypeStruct((TOP_K, N), F32)]


def _attn_out(os_, lses, w_out, x2d, ln_g, ln_b, rwt, rb):
    N, D = x2d.shape
    E = rwt.shape[0]
    tm = 256
    ins, outs = _post_specs(tm, D, E, lambda m: m)
    wide = pl.BlockSpec((tm, ATTN_WIDTH), lambda m: (m, 0))
    return pl.pallas_call(
        _attn_out_kernel,
        grid=(N // tm,),
        in_specs=[wide] * 6 + [pl.BlockSpec(w_out.shape, lambda m: (0, 0))] + ins,
        out_specs=outs,
        out_shape=_post_shapes(N, D),
        compiler_params=_params("arbitrary"),
        name="attn_out_ln_router",
    )(*os_, *lses, w_out, x2d, ln_g, ln_b, rwt, rb)


def _dn_out(o, w_out, x2d, ln_g, ln_b, rwt, rb):
    N, D = x2d.shape
    E = rwt.shape[0]
    Kin = o.shape[1]
    tm, tk = 512, 1024
    ins, outs = _post_specs(tm, D, E, lambda m, k: m)
    return pl.pallas_call(
        _dn_out_kernel,
        grid=(N // tm, Kin // tk),
        in_specs=[pl.BlockSpec((tm, tk), lambda m, k: (m, k)),
                  pl.BlockSpec((tk, D), lambda m, k: (k, 0))] + ins,
        out_specs=outs,
        out_shape=_post_shapes(N, D),
        scratch_shapes=[pltpu.VMEM((tm, D), F32)],
        compiler_params=_params("arbitrary", "arbitrary"),
        name="dn_out_ln_router",
    )(o, w_out, x2d, ln_g, ln_b, rwt, rb)


def _route(topi, gates, n_exp, rows):
    K, N = topi.shape
    P = K * N
    flat_e = topi.reshape(P)
    flat_g = gates.reshape(P)
    flat_tok = jnp.tile(jnp.arange(N, dtype=jnp.int32), K)
    order = jnp.argsort(flat_e, stable=True)
    e_sorted = flat_e[order]
    counts = jnp.bincount(flat_e, length=n_exp).astype(jnp.int32)
    starts = jnp.cumsum(counts) - counts
    padded = (counts + rows - 1) // rows * rows
    padded_end = jnp.cumsum(padded)
    padded_start = padded_end - padded
    dest = padded_start[e_sorted] + (jnp.arange(P, dtype=jnp.int32) - starts[e_sorted])
    n_blocks = -(-(P + n_exp * (rows - 1)) // rows)
    n_slots = n_blocks * rows
    slot_tok = jnp.zeros((n_slots,), jnp.int32).at[dest].set(flat_tok[order])
    slot_gate = jnp.zeros((n_slots,), F32).at[dest].set(flat_g[order])
    pos = jnp.zeros((P,), jnp.int32).at[order].set(dest)
    n_used = (padded_end[-1] // rows).astype(jnp.int32)
    blk = jnp.arange(n_blocks, dtype=jnp.int32)
    be = jnp.minimum(jnp.searchsorted(padded_end, blk * rows, side='right'), n_exp - 1).astype(jnp.int32)
    be = jnp.where(blk < n_used, be, be[jnp.maximum(n_used - 1, 0)])
    return slot_tok, slot_gate, pos, be, n_used.reshape(1)


def _ffn_kernel(be_ref, nu_ref, xs_ref, wg_ref, wl_ref, bg_ref, bl_ref, wd_ref, bd_ref, gate_ref, o_ref):
    b = pl.program_id(0)
    c = pl.program_id(1)
    used = b < nu_ref[0]

    @pl.when(used)
    def _():
        x = xs_ref[...]
        hg = jnp.dot(x, wg_ref[0].astype(BF16), preferred_element_type=F32) + bg_ref[0]
        hl = jnp.dot(x, wl_ref[0].astype(BF16), preferred_element_type=F32) + bl_ref[0]
        glu = jnp.minimum(hg, SWIGLU_LIMIT)
        lin = jnp.clip(hl, -SWIGLU_LIMIT, SWIGLU_LIMIT)
        act = glu * jax.nn.sigmoid(SWIGLU_ALPHA * glu) * (lin + 1.0)
        y = jnp.dot(act.astype(BF16), wd_ref[0].astype(BF16), preferred_element_type=F32)

        @pl.when(c == 0)
        def _():
            o_ref[...] = y + bd_ref[0]

        @pl.when(c > 0)
        def _():
            o_ref[...] += y

        @pl.when(c == pl.num_programs(1) - 1)
        def _():
            o_ref[...] = o_ref[...] * gate_ref[...]

    @pl.when(jnp.logical_and(jnp.logical_not(used), c == 0))
    def _():
        o_ref[...] = jnp.zeros_like(o_ref)


def _expert_ffn(xs, slot_gate, be, n_used, w_gate_up, b_gate_up, w_down, b_down):
    n_slots, D = xs.shape
    E, _, F2 = w_gate_up.shape
    F = F2 // 2
    rows, fc = MOE_ROWS, MOE_FF_CHUNK
    nfc = F // fc
    n_blocks = n_slots // rows

    def chunk(b, c, nu):
        return jnp.where(b < nu[0], c, nfc - 1)

    grid_spec = pltpu.PrefetchScalarGridSpec(
        num_scalar_prefetch=2,
        grid=(n_blocks, nfc),
        in_specs=[
            pl.BlockSpec((rows, D), lambda b, c, be, nu: (b, 0)),
            pl.BlockSpec((1, D, fc), lambda b, c, be, nu: (be[b], 0, chunk(b, c, nu))),
            pl.BlockSpec((1, D, fc), lambda b, c, be, nu: (be[b], 0, nfc + chunk(b, c, nu))),
            pl.BlockSpec((1, 1, fc), lambda b, c, be, nu: (be[b], 0, chunk(b, c, nu))),
            pl.BlockSpec((1, 1, fc), lambda b, c, be, nu: (be[b], 0, nfc + chunk(b, c, nu))),
            pl.BlockSpec((1, fc, D), lambda b, c, be, nu: (be[b], chunk(b, c, nu), 0)),
            pl.BlockSpec((1, 1, D), lambda b, c, be, nu: (be[b], 0, 0)),
            pl.BlockSpec((rows, 1), lambda b, c, be, nu: (b, 0)),
        ],
        out_specs=pl.BlockSpec((rows, D), lambda b, c, be, nu: (b, 0)),
    )
    return pl.pallas_call(
        _ffn_kernel,
        grid_spec=grid_spec,
        out_shape=jax.ShapeDtypeStruct((n_slots, D), F32),
        compiler_params=_params("arbitrary", "arbitrary"),
        name="expert_ffn",
    )(be, n_used, xs, w_gate_up, w_gate_up, b_gate_up.reshape(E, 1, F2), b_gate_up.reshape(E, 1, F2),
      w_down, b_down.reshape(E, 1, D), slot_gate.reshape(n_slots, 1))


def _combine_ln_kernel(y_ref, x_ref, g_ref, b_ref, x2_ref, x2b_ref):
    f = (y_ref[0] + y_ref[1]) + (y_ref[2] + y_ref[3])
    xn = _layer_norm(DEEPNORM_ALPHA * x_ref[...] + f, g_ref, b_ref)
    x2_ref[...] = xn
    x2b_ref[...] = xn.astype(BF16)


def _combine_ln(yg, x1, ln_g, ln_b):
    N, D = x1.shape
    tm = 256
    row = pl.BlockSpec((tm, D), lambda m: (m, 0))
    vec = pl.BlockSpec((1, D), lambda m: (0, 0))
    return pl.pallas_call(
        _combine_ln_kernel,
        grid=(N // tm,),
        in_specs=[pl.BlockSpec((TOP_K, tm, D), lambda m: (0, m, 0)), row, vec, vec],
        out_specs=[row, row],
        out_shape=[jax.ShapeDtypeStruct((N, D), F32), jax.ShapeDtypeStruct((N, D), BF16)],
        compiler_params=_params("arbitrary"),
        name="moe_combine_ln",
    )(yg, x1, ln_g, ln_b)


def _moe_block(x1, x1b, topi, gates, w_gate_up, b_gate_up, w_down, b_down, ln_g, ln_b):
    N, D = x1.shape
    E = w_gate_up.shape[0]
    slot_tok, slot_gate, pos, be, n_used = _route(topi, gates, E, MOE_ROWS)
    xs = jnp.take(x1b, slot_tok, axis=0)
    ys = _expert_ffn(xs, slot_gate, be, n_used, w_gate_up, b_gate_up, w_down, b_down)
    yg = jnp.take(ys, pos, axis=0).reshape(TOP_K, N, D)
    return _combine_ln(yg, x1, ln_g, ln_b)


def _dn_inproj_kernel(x_ref, w_ref, cw_ref, o_ref, pbuf, *, tm, tn, seq_tiles, q_tiles, qk_tiles, conv_tiles):
    n = pl.program_id(0)
    m = pl.program_id(1)
    p = jnp.dot(x_ref[...], w_ref[...], preferred_element_type=F32)

    @pl.when(n >= conv_tiles)
    def _():
        o_ref[...] = p

    @pl.when(n < conv_tiles)
    def _():
        @pl.when(m % seq_tiles == 0)
        def _():
            pbuf[0:8, :] = jnp.zeros((8, tn), F32)

        pbuf[8:tm + 8, :] = p
        cw = cw_ref[...]
        y = cw[0:1] * pbuf[5:tm + 5, :]
        y = y + cw[1:2] * pbuf[6:tm + 6, :]
        y = y + cw[2:3] * pbuf[7:tm + 7, :]
        y = y + cw[3:4] * p
        pbuf[0:8, :] = pbuf[tm:tm + 8, :]
        y = y * jax.nn.sigmoid(y)

        @pl.when(n >= qk_tiles)
        def _():
            o_ref[...] = y

        @pl.when(n < qk_tiles)
        def _():
            sc = jnp.where(n < q_tiles, HEAD_DIM ** -0.5, 1.0).astype(F32)
            for hh in range(tn // HEAD_DIM):
                sl = slice(hh * HEAD_DIM, (hh + 1) * HEAD_DIM)
                ys = y[:, sl]
                ss = jnp.sum(ys * ys, axis=-1, keepdims=True)
                o_ref[:, sl] = ys * lax.rsqrt(ss + DN_EPS) * sc


def _dn_inproj(xb, w, conv_w, seq_len):
    N, K = xb.shape
    Wd = w.shape[1]
    tm, tn = 512, 1024
    q_w = DN_QK_HEADS * HEAD_DIM
    conv_cols = conv_w.shape[1]
    kern = functools.partial(_dn_inproj_kernel, tm=tm, tn=tn, seq_tiles=seq_len // tm, q_tiles=q_w // tn,
                             qk_tiles=2 * q_w // tn, conv_tiles=conv_cols // tn)
    last_conv = conv_cols // tn - 1
    return pl.pallas_call(
        kern,
        grid=(Wd // tn, N // tm),
        in_specs=[pl.BlockSpec((tm, K), lambda n, m: (m, 0)),
                  pl.BlockSpec((K, tn), lambda n, m: (0, n)),
                  pl.BlockSpec((DN_CONV, tn), lambda n, m: (0, jnp.minimum(n, last_conv)))],
        out_specs=pl.BlockSpec((tm, tn), lambda n, m: (m, n)),
        out_shape=jax.ShapeDtypeStruct((N, Wd), F32),
        scratch_shapes=[pltpu.VMEM((tm + 8, tn), F32)],
        compiler_params=_params("arbitrary", "arbitrary"),
        name="dn_inproj_conv",
    )(xb, w, conv_w)


def _dn_gates_kernel(x_ref, w_ref, alog_ref, dtb_ref, o_ref):
    r = jnp.dot(x_ref[...], w_ref[...], preferred_element_type=F32)
    lane = lax.broadcasted_iota(jnp.int32, r.shape, 1)
    row = lax.broadcasted_iota(jnp.int32, r.shape, 0)
    beta = jax.nn.sigmoid(r)
    z = r + dtb_ref[...]
    softplus = jnp.maximum(z, 0.0) + jnp.log1p(jnp.exp(-jnp.abs(z)))
    g = -jnp.exp(alog_ref[...]) * softplus
    in_chunk = row % DN_CHUNK
    s = 1
    while s < DN_CHUNK:
        g = g + jnp.where(in_chunk >= s, pltpu.roll(g, s, axis=0), 0.0)
        s *= 2
    o_ref[...] = jnp.where(lane < DN_V_HEADS, beta, g)


def _dn_gates(xb, w_ba, a_log, dt_bias):
    N, K = xb.shape
    H = DN_V_HEADS
    tm = 512
    lanes = 128
    w_pad = jnp.zeros((K, lanes), BF16).at[:, :2 * H].set(w_ba)
    alog_pad = jnp.zeros((1, lanes), F32).at[0, H:2 * H].set(a_log)
    dtb_pad = jnp.zeros((1, lanes), F32).at[0, H:2 * H].set(dt_bias)
    vec = pl.BlockSpec((1, lanes), lambda m: (0, 0))
    return pl.pallas_call(
        _dn_gates_kernel,
        grid=(N // tm,),
        in_specs=[pl.BlockSpec((tm, K), lambda m: (m, 0)), pl.BlockSpec((K, lanes), lambda m: (0, 0)), vec, vec],
        out_specs=pl.BlockSpec((tm, lanes), lambda m: (m, 0)),
        out_shape=jax.ShapeDtypeStruct((N, lanes), F32),
        compiler_params=_params("arbitrary"),
        name="dn_gates",
    )(xb, w_pad, alog_pad, dtb_pad)


DN_HEADS_PER_STEP = 4
DN_CHUNKS_PER_STEP = 4


def _delta_kernel(q_ref, k_ref, v_ref, z_ref, bg_ref, gt_ref, ng_ref, o_ref, s_ref):
    C = DN_CHUNK
    HD = HEAD_DIM

    @pl.when(pl.program_id(2) == 0)
    def _():
        s_ref[...] = jnp.zeros_like(s_ref)

    ri = lax.broadcasted_iota(jnp.int32, (C, C), 0)
    ci = lax.broadcasted_iota(jnp.int32, (C, C), 1)
    incl = ri >= ci
    strict = ri > ci
    eye = (ri == ci).astype(F32)
    nt = (((1,), (1,)), ((), ()))
    states = [s_ref[j] for j in range(DN_HEADS_PER_STEP)]
    ng = ng_ref[...]

    for c in range(DN_CHUNKS_PER_STEP):
        rows = slice(c * C, (c + 1) * C)
        for j in range(DN_HEADS_PER_STEP):
            qsl = slice((j // 2) * HD, (j // 2 + 1) * HD)
            vsl = slice(j * HD, (j + 1) * HD)
            q = q_ref[0, rows, qsl]
            k = k_ref[0, rows, qsl]
            v = v_ref[0, rows, vsl]
            bcol = bg_ref[0, 0, rows, j:j + 1]
            gcol = bg_ref[0, 0, rows, DN_HEADS_PER_STEP + j:DN_HEADS_PER_STEP + j + 1]
            grow = gt_ref[0, 0, j:j + 1, rows]
            glast = gcol[C - 1:C, :]
            decay = jnp.exp(jnp.where(incl, gcol - grow, -jnp.inf))
            kb = k * bcol
            m1 = lax.dot_general(jnp.concatenate([kb, q], axis=0).astype(BF16), k.astype(BF16), nt,
                                 preferred_element_type=F32)
            a_mat = jnp.where(strict, m1[:C] * decay, 0.0)
            qk = m1[C:] * decay
            t_inv = eye - a_mat
            a_pow = a_mat
            for _ in range(5):
                ab = a_pow.astype(BF16)
                a_pow = jnp.dot(ab, ab, preferred_element_type=F32)
                t_inv = t_inv + jnp.dot(t_inv.astype(BF16), a_pow.astype(BF16), preferred_element_type=F32)
            eg = jnp.exp(gcol)
            rhs = jnp.concatenate([v * bcol, kb * eg], axis=1).astype(BF16)
            uw = jnp.dot(t_inv.astype(BF16), rhs, preferred_element_type=F32)
            u, w = uw[:, :HD], uw[:, HD:]
            state = states[j]
            r1 = jnp.dot(jnp.concatenate([w, q * eg], axis=0).astype(BF16), state.astype(BF16),
                         preferred_element_type=F32)
            v_new = u - r1[:C]
            kdec_t = k.T * jnp.exp(glast - grow)
            r2 = jnp.dot(jnp.concatenate([qk, kdec_t], axis=0).astype(BF16), v_new.astype(BF16),
                         preferred_element_type=F32)
            o = r1[C:] + r2[:C]
            states[j] = state * jnp.exp(glast) + r2[C:]
            o = o * lax.rsqrt(jnp.mean(o * o, axis=-1, keepdims=True) + DN_EPS) * ng
            zz = z_ref[0, rows, vsl]
            o_ref[0, rows, vsl] = (o * (zz * jax.nn.sigmoid(zz))).astype(o_ref.dtype)

    for j in range(DN_HEADS_PER_STEP):
        s_ref[j] = states[j]


def _delta_rule(qkvz, bg, norm_g, B, S):
    HD = HEAD_DIM
    hps = DN_HEADS_PER_STEP
    T = DN_CHUNKS_PER_STEP * DN_CHUNK
    q_w = DN_QK_HEADS * HD
    v_w = DN_V_HEADS * HD
    qkw = hps // 2 * HD
    vw = hps * HD
    n_hg = DN_V_HEADS // hps
    x3 = qkvz.reshape(B, S, qkvz.shape[1])
    beta = jnp.transpose(bg[:, :DN_V_HEADS].reshape(B, S, n_hg, hps), (0, 2, 1, 3))
    gcum = jnp.transpose(bg[:, DN_V_HEADS:2 * DN_V_HEADS].reshape(B, S, n_hg, hps), (0, 2, 1, 3))
    bg4 = jnp.concatenate([beta, gcum], axis=-1)
    gt4 = jnp.transpose(gcum, (0, 1, 3, 2))
    return pl.pallas_call(
        _delta_kernel,
        grid=(B, n_hg, S // T),
        in_specs=[pl.BlockSpec((1, T, qkw), lambda b, h, t: (b, t, h)),
                  pl.BlockSpec((1, T, qkw), lambda b, h, t: (b, t, q_w // qkw + h)),
                  pl.BlockSpec((1, T, vw), lambda b, h, t: (b, t, 2 * q_w // vw + h)),
                  pl.BlockSpec((1, T, vw), lambda b, h, t: (b, t, (2 * q_w + v_w) // vw + h)),
                  pl.BlockSpec((1, 1, T, 2 * hps), lambda b, h, t: (b, h, t, 0)),
                  pl.BlockSpec((1, 1, hps, T), lambda b, h, t: (b, h, 0, t)),
                  pl.BlockSpec((1, HD), lambda b, h, t: (0, 0))],
        out_specs=pl.BlockSpec((1, T, vw), lambda b, h, t: (b, t, h)),
        out_shape=jax.ShapeDtypeStruct((B, S, v_w), BF16),
        scratch_shapes=[pltpu.VMEM((hps, HD, HD), F32)],
        compiler_params=_params("arbitrary", "arbitrary", "arbitrary"),
        name="delta_rule",
    )(x3, x3, x3, x3, bg4, gt4, norm_g.reshape(1, HD)).reshape(B * S, v_w)


def kernel(x, rel_bias, attn_w_in, attn_w_out, dn_w_in, dn_conv_w, dn_a_log, dn_dt_bias, dn_norm_g, dn_w_out,
           ln1_g, ln1_b, router_w, router_b, w_gate_up, b_gate_up, w_down, b_down, ln2_g, ln2_b):
    B, S, D = x.shape
    N = B * S
    E = router_w.shape[2]
    x2d = x.reshape(N, D)

    def vec(p):
        return p.reshape(1, -1)

    def router(i):
        return jnp.transpose(router_w[i]).astype(BF16), router_b[i].reshape(E, 1)

    def moe(i, post):
        x1, x1b, topi, gates = post
        return _moe_block(x1, x1b, topi, gates, w_gate_up[i], b_gate_up[i], w_down[i], b_down[i],
                          vec(ln2_g[i]), vec(ln2_b[i]))

    proj = _matmul(x2d, attn_w_in[0].astype(BF16), 512, 1536, BF16).reshape(B, S, -1)
    os_, lses = [], []
    for g, (window, dilation) in enumerate(DILATED_GROUPS):
        assert window // dilation == ATTN_BLOCK
        bias = _attn_bias(rel_bias[:, g * ATTN_HEADS:(g + 1) * ATTN_HEADS], dilation)
        o, lse = _dilated_group(proj, bias, g, dilation)
        os_.append(o.reshape(N, ATTN_WIDTH))
        lses.append(lse.reshape(N, ATTN_WIDTH))
    rwt, rb = router(0)
    post = _attn_out(os_, lses, attn_w_out[0].astype(BF16), x2d, vec(ln1_g[0]), vec(ln1_b[0]), rwt, rb)
    x2, x2b = moe(0, post)

    conv_cols = dn_conv_w.shape[2]
    main_cols = conv_cols + DN_V_HEADS * HEAD_DIM
    w_in = dn_w_in[0].astype(BF16)
    qkvz = _dn_inproj(x2b, w_in[:, :main_cols], dn_conv_w[0], S)
    bg = _dn_gates(x2b, w_in[:, main_cols:], dn_a_log[0], dn_dt_bias[0])
    o = _delta_rule(qkvz, bg, dn_norm_g[0], B, S)
    rwt, rb = router(1)
    post = _dn_out(o, dn_w_out[0].astype(BF16), x2, vec(ln1_g[1]), vec(ln1_b[1]), rwt, rb)
    x3, _ = moe(1, post)
    return x3.reshape(B, S, D)
```

```python
import functools
import math

import jax
import jax.numpy as jnp
from jax import lax
from jax.experimental import pallas as pl
from jax.experimental.pallas import tpu as pltpu

F32 = jnp.float32
BF16 = jnp.bfloat16
I32 = jnp.int32
U32 = jnp.uint32

DEPTH = 2
DEEPNORM_ALPHA = (2.0 * DEPTH) ** 0.25
LN_EPS = 1e-5
HEAD_DIM = 128
ATTN_HEADS = 8
ATTN_WIDTH = ATTN_HEADS * HEAD_DIM
DILATED_GROUPS = ((128, 1), (512, 4), (2048, 16))
ATTN_BLOCK = 128
MAX_DISTANCE = 2048
DN_QK_HEADS = 16
DN_V_HEADS = 32
DN_CONV = 4
DN_CHUNK = 64
DN_EPS = 1e-6
TOP_K = 4
SWIGLU_LIMIT = 7.0
SWIGLU_ALPHA = 1.702

MOE_ROWS = 1024
MOE_HALF = 512
MOE_FF_CHUNK = 256
ROW_TILE = 256
VMEM_LIMIT = 58 * 1024 * 1024


def _params(*semantics):
    return pltpu.CompilerParams(dimension_semantics=semantics, vmem_limit_bytes=VMEM_LIMIT)


def _pack_bf16_pair(lo, hi):
    lo_bits = lax.bitcast_convert_type(lo.astype(BF16).astype(F32), U32)
    hi_bits = lax.bitcast_convert_type(hi.astype(BF16).astype(F32), U32)
    return (hi_bits & jnp.uint32(0xFFFF0000)) | (lo_bits >> 16)


def _unpack_bf16_pair(u):
    lo = lax.bitcast_convert_type(u << 16, F32)
    hi = lax.bitcast_convert_type(u & jnp.uint32(0xFFFF0000), F32)
    return lo, hi


def _attn_inproj_kernel(a_ref, w_ref, o_ref, acc_ref, *, d, tm):
    res = jnp.dot(a_ref[...].astype(BF16), w_ref[...], preferred_element_type=F32)
    if d == 1:
        o_ref[0, 0] = res.astype(o_ref.dtype)
        return
    lanes = acc_ref.shape[2]
    for cb in range(acc_ref.shape[0]):
        acc_ref[cb] = res[:, cb * lanes:(cb + 1) * lanes]
    for r in range(d):
        for cb in range(acc_ref.shape[0]):
            o_ref[0, r, :, cb * lanes:(cb + 1) * lanes] = (
                acc_ref[cb, pl.ds(r, tm // d, stride=d), :].astype(o_ref.dtype))


def _attn_inproj(x2d, w, g, d, B, S):
    K = x2d.shape[1]
    tm, tn = 512, 1536
    gw = 3 * ATTN_WIDTH
    nt = gw // tn
    mt = S // tm
    return pl.pallas_call(
        functools.partial(_attn_inproj_kernel, d=d, tm=tm),
        grid=(nt, B, mt),
        in_specs=[pl.BlockSpec((tm, K), lambda n, b, m: (b * mt + m, 0)),
                  pl.BlockSpec((K, tn), lambda n, b, m: (0, g * nt + n))],
        out_specs=pl.BlockSpec((1, d, tm // d, tn), lambda n, b, m: (b, 0, m, n)),
        out_shape=jax.ShapeDtypeStruct((B, d, S // d, gw), BF16),
        scratch_shapes=[pltpu.VMEM((tn // HEAD_DIM, tm, HEAD_DIM), F32)],
        compiler_params=_params("arbitrary", "arbitrary", "arbitrary"),
        name=f"attn_inproj_g{g}",
    )(x2d, w)


def _t5_bucket(dist, num_buckets):
    max_exact = num_buckets // 2
    d_f = jnp.maximum(dist, 1).astype(F32)
    large = max_exact + (jnp.log(d_f / max_exact) / math.log(MAX_DISTANCE / max_exact)
                         * (num_buckets - max_exact)).astype(I32)
    large = jnp.minimum(large, num_buckets - 1)
    return jnp.where(dist < max_exact, dist, large)


def _attn_bias(rel_bias_g, dilation):
    bq = ATTN_BLOCK
    a = jnp.arange(bq, dtype=I32)[:, None]
    c = jnp.arange(2 * bq, dtype=I32)[None, :]
    steps = a + bq - c
    valid = (steps >= 0) & (steps <= bq)
    bucket = _t5_bucket(jnp.maximum(steps, 0) * dilation, rel_bias_g.shape[0])
    bias = jnp.transpose(rel_bias_g[bucket], (2, 0, 1)).astype(F32)
    return jnp.where(valid[None], bias, -jnp.inf)


def _attn_kernel(q_ref, kp_ref, kc_ref, vp_ref, vc_ref, bias_ref, o_ref, lse_ref):
    bq = ATTN_BLOCK
    first = pl.program_id(2) == 0
    scale = HEAD_DIM ** -0.5
    nt = (((1,), (1,)), ((), ()))
    lse_ref[0, 0] = jnp.zeros(lse_ref.shape[2:], F32)
    heads = range(ATTN_HEADS)
    sls = [slice(h * HEAD_DIM, (h + 1) * HEAD_DIM) for h in heads]
    s_p = [lax.dot_general(q_ref[0, 0, :, sls[h]], kp_ref[0, 0, :, sls[h]], nt, preferred_element_type=F32)
           for h in heads]
    s_c = [lax.dot_general(q_ref[0, 0, :, sls[h]], kc_ref[0, 0, :, sls[h]], nt, preferred_element_type=F32)
           for h in heads]
    p_p, p_c = [], []
    for h in heads:
        sp = jnp.where(first, -jnp.inf, s_p[h] * scale + bias_ref[h, :, :bq])
        sc = s_c[h] * scale + bias_ref[h, :, bq:]
        m = jnp.maximum(jnp.max(sp, axis=-1, keepdims=True), jnp.max(sc, axis=-1, keepdims=True))
        e_p = jnp.exp(sp - m)
        e_c = jnp.exp(sc - m)
        l = jnp.sum(e_p, axis=-1, keepdims=True) + jnp.sum(e_c, axis=-1, keepdims=True)
        inv = 1.0 / l
        p_p.append((e_p * inv).astype(BF16))
        p_c.append((e_c * inv).astype(BF16))
        lse_ref[0, 0, :, h:h + 1] = m + jnp.log(l)
    for h in heads:
        o_ref[0, 0, :, sls[h]] = (jnp.dot(p_p[h], vp_ref[0, 0, :, sls[h]], preferred_element_type=F32)
                                  + jnp.dot(p_c[h], vc_ref[0, 0, :, sls[h]], preferred_element_type=F32))


def _dilated_group(qkv, bias, g):
    B, d, L, _ = qkv.shape
    nb = L // ATTN_BLOCK
    blk = (1, 1, ATTN_BLOCK, ATTN_WIDTH)
    lse_blk = (1, 1, ATTN_BLOCK, HEAD_DIM)

    def spec(kind, prev):
        if prev:
            return pl.BlockSpec(blk, lambda b, r, i: (b, r, jnp.maximum(i - 1, 0), kind))
        return pl.BlockSpec(blk, lambda b, r, i: (b, r, i, kind))

    return pl.pallas_call(
        _attn_kernel,
        grid=(B, d, nb),
        in_specs=[spec(0, False), spec(1, True), spec(1, False), spec(2, True), spec(2, False),
                  pl.BlockSpec(bias.shape, lambda b, r, i: (0, 0, 0))],
        out_specs=[pl.BlockSpec(blk, lambda b, r, i: (b, r, i, 0)),
                   pl.BlockSpec(lse_blk, lambda b, r, i: (b, r, i, 0))],
        out_shape=[jax.ShapeDtypeStruct((B, d, L, ATTN_WIDTH), F32),
                   jax.ShapeDtypeStruct((B, d, L, HEAD_DIM), F32)],
        compiler_params=_params("arbitrary", "arbitrary", "arbitrary"),
        name=f"dilated_attn_g{g}",
    )(qkv, qkv, qkv, qkv, qkv, bias)


def _layer_norm(y, g_ref, b_ref):
    mu = jnp.mean(y, axis=-1, keepdims=True)
    yc = y - mu
    var = jnp.mean(yc * yc, axis=-1, keepdims=True)
    return yc * lax.rsqrt(var + LN_EPS) * g_ref[...] + b_ref[...]


def _residual_ln_router(h, x_ref, g_ref, b_ref, rwt_ref, rb_ref,
                        x1_ref, x1p_ref, topi_ref, gate_ref, rank_ref, cnt_ref, carry_ref, first_step):
    xn = _layer_norm(DEEPNORM_ALPHA * x_ref[...] + h, g_ref, b_ref)
    x1_ref[...] = xn
    half = xn.shape[1] // 2
    x1p_ref[...] = _pack_bf16_pair(xn[:, :half], xn[:, half:])
    logits = lax.dot_general(rwt_ref[...], xn.astype(BF16), (((1,), (1,)), ((), ())),
                             preferred_element_type=F32) + rb_ref[...]
    n_exp, tm = logits.shape
    eidx = lax.broadcasted_iota(I32, logits.shape, 0)
    vals, hots = [], []
    for k in range(TOP_K):
        mx = jnp.max(logits, axis=0, keepdims=True)
        idx = jnp.min(jnp.where(logits == mx, eidx, n_exp), axis=0, keepdims=True)
        topi_ref[k:k + 1, :] = idx
        vals.append(mx)
        hit = eidx == idx
        hots.append(hit.astype(F32))
        logits = jnp.where(hit, -jnp.inf, logits)
    exps = [jnp.exp(v - vals[0]) for v in vals]
    denom = exps[0] + exps[1] + exps[2] + exps[3]
    for k in range(TOP_K):
        gate_ref[k:k + 1, :] = exps[k] / denom

    @pl.when(first_step)
    def _():
        carry_ref[...] = jnp.zeros_like(carry_ref)

    tri = (lax.broadcasted_iota(I32, (tm, tm), 0) <= lax.broadcasted_iota(I32, (tm, tm), 1)).astype(BF16)
    cum = jnp.dot(jnp.concatenate(hots, axis=0).astype(BF16), tri, preferred_element_type=F32)
    base = carry_ref[...]
    for k in range(TOP_K):
        cum_k = cum[k * n_exp:(k + 1) * n_exp]
        before = cum_k - hots[k] + base
        rank_ref[k:k + 1, :] = jnp.sum(hots[k] * before, axis=0, keepdims=True).astype(I32)
        base = base + cum_k[:, tm - 1:tm]
    carry_ref[...] = base
    cnt_ref[...] = jnp.broadcast_to(base, cnt_ref.shape)


def _attn_out_kernel(o0_ref, o1_ref, o2_ref, l0_ref, l1_ref, l2_ref, w_ref, x_ref, g_ref, b_ref, rwt_ref, rb_ref,
                     x1_ref, x1p_ref, topi_ref, gate_ref, rank_ref, cnt_ref,
                     o_scr, l_scr, merged_ref, carry_ref, *, tm):
    o_refs = (o0_ref, o1_ref, o2_ref)
    l_refs = (l0_ref, l1_ref, l2_ref)
    H = ATTN_HEADS
    for g, (_, d) in enumerate(DILATED_GROUPS):
        for r in range(d):
            for h in range(H):
                o_scr[g * H + h, pl.ds(r, tm // d, stride=d), :] = o_refs[g][0, r, :, h * HEAD_DIM:(h + 1) * HEAD_DIM]
            l_scr[g, pl.ds(r, tm // d, stride=d), :] = l_refs[g][0, r]
    l0, l1, l2 = l_scr[0], l_scr[1], l_scr[2]
    m = jnp.maximum(jnp.maximum(l0, l1), l2)
    e = [jnp.exp(l0 - m), jnp.exp(l1 - m), jnp.exp(l2 - m)]
    den = e[0] + e[1] + e[2]
    wts = [ei / den for ei in e]
    for h in range(H):
        o = wts[0][:, h:h + 1] * o_scr[h]
        o = o + wts[1][:, h:h + 1] * o_scr[H + h]
        o = o + wts[2][:, h:h + 1] * o_scr[2 * H + h]
        merged_ref[:, h * HEAD_DIM:(h + 1) * HEAD_DIM] = o.astype(BF16)
    h_out = jnp.dot(merged_ref[...], w_ref[...], preferred_element_type=F32)
    _residual_ln_router(h_out, x_ref, g_ref, b_ref, rwt_ref, rb_ref, x1_ref, x1p_ref, topi_ref, gate_ref,
                        rank_ref, cnt_ref, carry_ref, pl.program_id(0) == 0)


def _dn_out_kernel(o_ref, w_ref, x_ref, g_ref, b_ref, rwt_ref, rb_ref,
                   x1_ref, x1p_ref, topi_ref, gate_ref, rank_ref, cnt_ref, acc_ref, carry_ref):
    kt = pl.program_id(1)
    part = jnp.dot(o_ref[...], w_ref[...], preferred_element_type=F32)

    @pl.when(kt == 0)
    def _():
        acc_ref[...] = part

    @pl.when(kt > 0)
    def _():
        acc_ref[...] += part

    @pl.when(kt == pl.num_programs(1) - 1)
    def _():
        _residual_ln_router(acc_ref[...], x_ref, g_ref, b_ref, rwt_ref, rb_ref, x1_ref, x1p_ref, topi_ref,
                            gate_ref, rank_ref, cnt_ref, carry_ref, pl.program_id(0) == 0)


def _post_specs(tm, D, E, idx):
    row = lambda *a: (idx(*a), 0)
    const = lambda *a: (0, 0)
    col = lambda *a: (0, idx(*a))
    ins = [pl.BlockSpec((tm, D), row),
           pl.BlockSpec((1, D), const), pl.BlockSpec((1, D), const),
           pl.BlockSpec((E, D), const), pl.BlockSpec((E, 1), const)]
    outs = [pl.BlockSpec((tm, D), row), pl.BlockSpec((tm, D // 2), row),
            pl.BlockSpec((TOP_K, tm), col), pl.BlockSpec((TOP_K, tm), col), pl.BlockSpec((TOP_K, tm), col),
            pl.BlockSpec((E, 128), const)]
    return ins, outs


def _post_shapes(N, D, E):
    return [jax.ShapeDtypeStruct((N, D), F32), jax.ShapeDtypeStruct((N, D // 2), U32),
            jax.ShapeDtypeStruct((TOP_K, N), I32), jax.ShapeDtypeStruct((TOP_K, N), F32),
            jax.ShapeDtypeStruct((TOP_K, N), I32), jax.ShapeDtypeStruct((E, 128), F32)]


def _attn_out(os_, lses, w_out, x2d, ln_g, ln_b, rwt, rb, S):
    N, D = x2d.shape
    E = rwt.shape[0]
    tm = 256
    mt = S // tm
    ins, outs = _post_specs(tm, D, E, lambda m: m)
    grp_specs = []
    for width in (ATTN_WIDTH, HEAD_DIM):
        for _, d in DILATED_GROUPS:
            grp_specs.append(pl.BlockSpec((1, d, tm // d, width), lambda m: (m // mt, 0, m % mt, 0)))
    return pl.pallas_call(
        functools.partial(_attn_out_kernel, tm=tm),
        grid=(N // tm,),
        in_specs=grp_specs + [pl.BlockSpec(w_out.shape, lambda m: (0, 0))] + ins,
        out_specs=outs,
        out_shape=_post_shapes(N, D, E),
        scratch_shapes=[pltpu.VMEM((len(DILATED_GROUPS) * ATTN_HEADS, tm, HEAD_DIM), F32),
                        pltpu.VMEM((len(DILATED_GROUPS), tm, HEAD_DIM), F32),
                        pltpu.VMEM((tm, ATTN_WIDTH), BF16),
                        pltpu.VMEM((E, 1), F32)],
        compiler_params=_params("arbitrary"),
        name="attn_out_ln_router",
    )(*os_, *lses, w_out, x2d, ln_g, ln_b, rwt, rb)


def _dn_out(o, w_out, x2d, ln_g, ln_b, rwt, rb):
    N, D = x2d.shape
    E = rwt.shape[0]
    Kin = o.shape[1]
    tm, tk = 512, 1024
    ins, outs = _post_specs(tm, D, E, lambda m, k: m)
    return pl.pallas_call(
        _dn_out_kernel,
        grid=(N // tm, Kin // tk),
        in_specs=[pl.BlockSpec((tm, tk), lambda m, k: (m, k)),
                  pl.BlockSpec((tk, D), lambda m, k: (k, 0))] + ins,
        out_specs=outs,
        out_shape=_post_shapes(N, D, E),
        scratch_shapes=[pltpu.VMEM((tm, D), F32), pltpu.VMEM((E, 1), F32)],
        compiler_params=_params("arbitrary", "arbitrary"),
        name="dn_out_ln_router",
    )(o, w_out, x2d, ln_g, ln_b, rwt, rb)


def _slot_layout(topi, rank, counts, rows):
    K, N = topi.shape
    n_exp = counts.shape[0]
    padded = (counts + rows - 1) // rows * rows
    padded_end = jnp.cumsum(padded)
    padded_start = padded_end - padded
    sel = topi[None] == jnp.arange(n_exp, dtype=I32)[:, None, None]
    pos = jnp.sum(jnp.where(sel, padded_start[:, None, None], 0), axis=0) + rank
    n_blocks = -(-(K * N + n_exp * (rows - 1)) // rows)
    n_used = padded_end[-1] // rows
    blk = jnp.arange(n_blocks, dtype=I32)
    be = jnp.sum((blk[:, None] * rows >= padded_end[None, :]).astype(I32), axis=1)
    be = jnp.minimum(be, n_exp - 1)
    be_last = jnp.sum(jnp.where(blk == jnp.maximum(n_used - 1, 0), be, 0))
    used = blk < n_used
    be = jnp.where(used, be, be_last)
    start_b = jnp.sum(jnp.where(be[:, None] == jnp.arange(n_exp)[None, :], padded_start[None, :], 0), axis=1)
    cnt_e = jnp.sum(jnp.where(be[:, None] == jnp.arange(n_exp)[None, :], counts[None, :], 0), axis=1)
    valid = jnp.where(used, jnp.clip(cnt_e - (blk * rows - start_b), 0, rows), 0).astype(I32)
    n_slots = jnp.full((1,), n_blocks * rows, I32)
    pad_lo = jnp.concatenate([padded_start + counts, padded_end[-1:]]).astype(I32)
    pad_hi = jnp.concatenate([padded_end, n_slots]).astype(I32)
    return pos.astype(I32), be.astype(I32), valid, pad_lo, pad_hi, n_blocks


def _tile_major(pos, tm):
    K, N = pos.shape
    return jnp.transpose(pos.reshape(K, N // tm, tm), (1, 0, 2)).reshape(N // tm, K * tm)


def _dispatch_kernel(lo_ref, hi_ref, pos_hbm, x_ref, xs_hbm, idx_smem, zbuf, sem_idx, sem_rows, sem_zero, *, tm):
    i = pl.program_id(0)

    @pl.when(i == 0)
    def _():
        zbuf[...] = jnp.zeros_like(zbuf)
        one = zbuf.at[pl.ds(0, 1)]

        def per_expert(e, carry):
            lo = lo_ref[e]
            hi = hi_ref[e]
            lo_al = jnp.minimum((lo + 7) // 8 * 8, hi)

            def row_start(r, c):
                pltpu.make_async_copy(one, xs_hbm.at[pl.ds(r, 1)], sem_zero).start()
                return c

            def grp_start(q, c):
                pltpu.make_async_copy(zbuf, xs_hbm.at[pl.ds(pl.multiple_of(q * 8, 8), 8)], sem_zero).start()
                return c

            def row_wait(r, c):
                pltpu.make_async_copy(one, xs_hbm.at[pl.ds(0, 1)], sem_zero).wait()
                return c

            def grp_wait(q, c):
                pltpu.make_async_copy(zbuf, xs_hbm.at[pl.ds(0, 8)], sem_zero).wait()
                return c

            lax.fori_loop(lo, lo_al, row_start, 0)
            lax.fori_loop(lo_al // 8, hi // 8, grp_start, 0)
            lax.fori_loop(lo, lo_al, row_wait, 0)
            lax.fori_loop(lo_al // 8, hi // 8, grp_wait, 0)
            return carry

        lax.fori_loop(0, lo_ref.shape[0], per_expert, 0)

    cp = pltpu.make_async_copy(pos_hbm.at[i], idx_smem, sem_idx)
    cp.start()
    cp.wait()

    def issue(jj, carry):
        for u in range(8):
            j = jj * 8 + u
            for k in range(TOP_K):
                dst = idx_smem[k * tm + j]
                pltpu.make_async_copy(x_ref.at[pl.ds(j, 1)], xs_hbm.at[pl.ds(dst, 1)], sem_rows).start()
        return carry

    lax.fori_loop(0, tm // 8, issue, 0)
    for k in range(TOP_K):
        pltpu.make_async_copy(x_ref, xs_hbm.at[pl.ds(0, tm)], sem_rows).wait()


def _dispatch(x1p, pos_tiles, pad_lo, pad_hi, n_slots):
    N, W = x1p.shape
    tm = ROW_TILE
    grid_spec = pltpu.PrefetchScalarGridSpec(
        num_scalar_prefetch=2,
        grid=(N // tm,),
        in_specs=[pl.BlockSpec(memory_space=pl.ANY),
                  pl.BlockSpec((tm, W), lambda i, lo, hi: (i, 0))],
        out_specs=pl.BlockSpec(memory_space=pl.ANY),
        scratch_shapes=[pltpu.SMEM((TOP_K * tm,), I32), pltpu.VMEM((8, W), U32),
                        pltpu.SemaphoreType.DMA, pltpu.SemaphoreType.DMA, pltpu.SemaphoreType.DMA],
    )
    return pl.pallas_call(
        functools.partial(_dispatch_kernel, tm=tm),
        grid_spec=grid_spec,
        out_shape=jax.ShapeDtypeStruct((n_slots, W), U32),
        compiler_params=_params("arbitrary"),
        name="moe_dispatch",
    )(pad_lo, pad_hi, pos_tiles, x1p)


def _ffn_kernel(be_ref, cnt_ref, xs_ref, wg_ref, wl_ref, bg_ref, bl_ref, wd_ref, bd_ref, o_ref, xb_ref):
    b = pl.program_id(0)
    c = pl.program_id(1)
    last = pl.num_programs(1) - 1
    cnt = cnt_ref[b]
    rows, D = o_ref.shape
    half = D // 2

    @pl.when(jnp.logical_and(cnt > 0, c == 0))
    def _():
        lo, hi = _unpack_bf16_pair(xs_ref[...])
        xb_ref[:, :half] = lo.astype(BF16)
        xb_ref[:, half:] = hi.astype(BF16)

    for p in range(rows // MOE_HALF):
        rs = slice(p * MOE_HALF, (p + 1) * MOE_HALF)
        live = cnt > p * MOE_HALF

        @pl.when(live)
        def _():
            x = xb_ref[rs, :]
            hg = jnp.dot(x, wg_ref[0, 0].astype(BF16), preferred_element_type=F32) + bg_ref[0, 0]
            hl = jnp.dot(x, wl_ref[0, 0].astype(BF16), preferred_element_type=F32) + bl_ref[0, 0]
            glu = jnp.minimum(hg, SWIGLU_LIMIT)
            lin = jnp.clip(hl, -SWIGLU_LIMIT, SWIGLU_LIMIT)
            act = glu * jax.nn.sigmoid(SWIGLU_ALPHA * glu) * (lin + 1.0)
            y = jnp.dot(act.astype(BF16), wd_ref[0, 0].astype(BF16), preferred_element_type=F32)

            @pl.when(c == 0)
            def _():
                o_ref[rs, :] = y + bd_ref[0, 0]

            @pl.when(c > 0)
            def _():
                o_ref[rs, :] += y

        @pl.when(jnp.logical_and(jnp.logical_not(live), c == 0))
        def _():
            o_ref[rs, :] = jnp.zeros((MOE_HALF, D), F32)


def _expert_ffn(xs, be, valid, layer, w_gate_up, b_gate_up, w_down, b_down):
    n_slots, half = xs.shape
    D = 2 * half
    _, E, _, F2 = w_gate_up.shape
    F = F2 // 2
    rows, fc = MOE_ROWS, MOE_FF_CHUNK
    nfc = F // fc
    n_blocks = n_slots // rows

    def chunk(b, c, cnt):
        return jnp.where(cnt[b] > 0, c, nfc - 1)

    grid_spec = pltpu.PrefetchScalarGridSpec(
        num_scalar_prefetch=2,
        grid=(n_blocks, nfc),
        in_specs=[
            pl.BlockSpec((rows, half), lambda b, c, be, cnt: (b, 0)),
            pl.BlockSpec((1, 1, D, fc), lambda b, c, be, cnt: (layer, be[b], 0, chunk(b, c, cnt))),
            pl.BlockSpec((1, 1, D, fc), lambda b, c, be, cnt: (layer, be[b], 0, nfc + chunk(b, c, cnt))),
            pl.BlockSpec((1, 1, 1, fc), lambda b, c, be, cnt: (layer, be[b], 0, chunk(b, c, cnt))),
            pl.BlockSpec((1, 1, 1, fc), lambda b, c, be, cnt: (layer, be[b], 0, nfc + chunk(b, c, cnt))),
            pl.BlockSpec((1, 1, fc, D), lambda b, c, be, cnt: (layer, be[b], chunk(b, c, cnt), 0)),
            pl.BlockSpec((1, 1, 1, D), lambda b, c, be, cnt: (layer, be[b], 0, 0)),
        ],
        out_specs=pl.BlockSpec((rows, D), lambda b, c, be, cnt: (b, 0)),
        scratch_shapes=[pltpu.VMEM((rows, D), BF16)],
    )
    L = w_gate_up.shape[0]
    return pl.pallas_call(
        _ffn_kernel,
        grid_spec=grid_spec,
        out_shape=jax.ShapeDtypeStruct((n_slots, D), F32),
        compiler_params=_params("arbitrary", "arbitrary"),
        name="expert_ffn",
    )(be, valid, xs, w_gate_up, w_gate_up, b_gate_up.reshape(L, E, 1, F2), b_gate_up.reshape(L, E, 1, F2),
      w_down, b_down.reshape(L, E, 1, D))


def _combine_kernel(pos_hbm, ys_hbm, gate_ref, x_ref, g_ref, b_ref, x2_ref, x2b_ref,
                    idx_smem, buf, sem_idx, sem_rows, *, tm):
    i = pl.program_id(0)
    n = pl.num_programs(0)

    def gather(tile, slot):
        cp = pltpu.make_async_copy(pos_hbm.at[tile], idx_smem.at[slot], sem_idx.at[slot])
        cp.start()
        cp.wait()

        def issue(jj, carry):
            for u in range(8):
                j = jj * 8 + u
                for k in range(TOP_K):
                    src = idx_smem[slot, k * tm + j]
                    pltpu.make_async_copy(ys_hbm.at[pl.ds(src, 1)], buf.at[slot, k, pl.ds(j, 1)],
                                          sem_rows.at[slot]).start()
            return carry

        lax.fori_loop(0, tm // 8, issue, 0)

    @pl.when(i == 0)
    def _():
        gather(0, 0)

    slot = i % 2

    @pl.when(i + 1 < n)
    def _():
        gather(i + 1, 1 - slot)

    for k in range(TOP_K):
        pltpu.make_async_copy(ys_hbm.at[pl.ds(0, tm)], buf.at[slot, k], sem_rows.at[slot]).wait()
    gate = gate_ref[...]
    f = buf[slot, 0] * gate[:, 0:1]
    for k in range(1, TOP_K):
        f = f + buf[slot, k] * gate[:, k:k + 1]
    xn = _layer_norm(DEEPNORM_ALPHA * x_ref[...] + f, g_ref, b_ref)
    x2_ref[...] = xn
    x2b_ref[...] = xn.astype(BF16)


def _combine_ln(ys, pos_tiles, gates_t, x1, ln_g, ln_b):
    N, D = x1.shape
    tm = ROW_TILE
    row = pl.BlockSpec((tm, D), lambda m: (m, 0))
    vec = pl.BlockSpec((1, D), lambda m: (0, 0))
    anyspec = pl.BlockSpec(memory_space=pl.ANY)
    return pl.pallas_call(
        functools.partial(_combine_kernel, tm=tm),
        grid=(N // tm,),
        in_specs=[anyspec, anyspec, pl.BlockSpec((tm, TOP_K), lambda m: (m, 0)), row, vec, vec],
        out_specs=[row, row],
        out_shape=[jax.ShapeDtypeStruct((N, D), F32), jax.ShapeDtypeStruct((N, D), BF16)],
        scratch_shapes=[pltpu.SMEM((2, TOP_K * tm), I32), pltpu.VMEM((2, TOP_K, tm, D), F32),
                        pltpu.SemaphoreType.DMA((2,)), pltpu.SemaphoreType.DMA((2,))],
        compiler_params=_params("arbitrary"),
        name="moe_combine_ln",
    )(pos_tiles, ys, gates_t, x1, ln_g, ln_b)


def _moe_block(post, layer, w_gate_up, b_gate_up, w_down, b_down, ln_g, ln_b):
    x1, x1p, topi, gates, rank, cnt = post
    counts = cnt[:, 0].astype(I32)
    pos, be, valid, pad_lo, pad_hi, n_blocks = _slot_layout(topi, rank, counts, MOE_ROWS)
    pos_tiles = _tile_major(pos, ROW_TILE)
    xs = _dispatch(x1p, pos_tiles, pad_lo, pad_hi, n_blocks * MOE_ROWS)
    ys = _expert_ffn(xs, be, valid, layer, w_gate_up, b_gate_up, w_down, b_down)
    return _combine_ln(ys, pos_tiles, jnp.transpose(gates), x1, ln_g, ln_b)


def _dn_inproj_kernel(x_ref, w_ref, cw_ref, o_ref, pbuf, *, tm, tn, seq_tiles, q_tiles, qk_tiles, conv_tiles):
    n = pl.program_id(0)
    m = pl.program_id(1)
    p = jnp.dot(x_ref[...], w_ref[...], preferred_element_type=F32)

    @pl.when(n >= conv_tiles)
    def _():
        o_ref[...] = p

    @pl.when(n < conv_tiles)
    def _():
        @pl.when(m % seq_tiles == 0)
        def _():
            pbuf[0:8, :] = jnp.zeros((8, tn), F32)

        pbuf[8:tm + 8, :] = p
        cw = cw_ref[...]
        y = cw[0:1] * pbuf[5:tm + 5, :]
        y = y + cw[1:2] * pbuf[6:tm + 6, :]
        y = y + cw[2:3] * pbuf[7:tm + 7, :]
        y = y + cw[3:4] * p
        pbuf[0:8, :] = pbuf[tm:tm + 8, :]
        y = y * jax.nn.sigmoid(y)

        @pl.when(n >= qk_tiles)
        def _():
            o_ref[...] = y

        @pl.when(n < qk_tiles)
        def _():
            sc = jnp.where(n < q_tiles, HEAD_DIM ** -0.5, 1.0).astype(F32)
            for hh in range(tn // HEAD_DIM):
                sl = slice(hh * HEAD_DIM, (hh + 1) * HEAD_DIM)
                ys = y[:, sl]
                ss = jnp.sum(ys * ys, axis=-1, keepdims=True)
                o_ref[:, sl] = ys * lax.rsqrt(ss + DN_EPS) * sc


def _dn_inproj(xb, w, conv_w, seq_len):
    N, K = xb.shape
    Wd = w.shape[1]
    tm, tn = 512, 1024
    q_w = DN_QK_HEADS * HEAD_DIM
    conv_cols = conv_w.shape[1]
    kern = functools.partial(_dn_inproj_kernel, tm=tm, tn=tn, seq_tiles=seq_len // tm, q_tiles=q_w // tn,
                             qk_tiles=2 * q_w // tn, conv_tiles=conv_cols // tn)
    last_conv = conv_cols // tn - 1
    return pl.pallas_call(
        kern,
        grid=(Wd // tn, N // tm),
        in_specs=[pl.BlockSpec((tm, K), lambda n, m: (m, 0)),
                  pl.BlockSpec((K, tn), lambda n, m: (0, n)),
                  pl.BlockSpec((DN_CONV, tn), lambda n, m: (0, jnp.minimum(n, last_conv)))],
        out_specs=pl.BlockSpec((tm, tn), lambda n, m: (m, n)),
        out_shape=jax.ShapeDtypeStruct((N, Wd), F32),
        scratch_shapes=[pltpu.VMEM((tm + 8, tn), F32)],
        compiler_params=_params("arbitrary", "arbitrary"),
        name="dn_inproj_conv",
    )(xb, w, conv_w)


def _dn_gates_kernel(x_ref, w_ref, alog_ref, dtb_ref, o_ref):
    r = jnp.dot(x_ref[...], w_ref[...], preferred_element_type=F32)
    lane = lax.broadcasted_iota(I32, r.shape, 1)
    row = lax.broadcasted_iota(I32, r.shape, 0)
    beta = jax.nn.sigmoid(r)
    z = r + dtb_ref[...]
    softplus = jnp.maximum(z, 0.0) + jnp.log1p(jnp.exp(-jnp.abs(z)))
    g = -jnp.exp(alog_ref[...]) * softplus
    in_chunk = row % DN_CHUNK
    s = 1
    while s < DN_CHUNK:
        g = g + jnp.where(in_chunk >= s, pltpu.roll(g, s, axis=0), 0.0)
        s *= 2
    o_ref[...] = jnp.where(lane < DN_V_HEADS, beta, g)


def _dn_gates(xb, w_ba, a_log, dt_bias):
    N, K = xb.shape
    H = DN_V_HEADS
    tm = 512
    lanes = 128
    w_pad = jnp.zeros((K, lanes), BF16).at[:, :2 * H].set(w_ba)
    alog_pad = jnp.zeros((1, lanes), F32).at[0, H:2 * H].set(a_log)
    dtb_pad = jnp.zeros((1, lanes), F32).at[0, H:2 * H].set(dt_bias)
    vec = pl.BlockSpec((1, lanes), lambda m: (0, 0))
    return pl.pallas_call(
        _dn_gates_kernel,
        grid=(N // tm,),
        in_specs=[pl.BlockSpec((tm, K), lambda m: (m, 0)), pl.BlockSpec((K, lanes), lambda m: (0, 0)), vec, vec],
        out_specs=pl.BlockSpec((tm, lanes), lambda m: (m, 0)),
        out_shape=jax.ShapeDtypeStruct((N, lanes), F32),
        compiler_params=_params("arbitrary"),
        name="dn_gates",
    )(xb, w_pad, alog_pad, dtb_pad)


DN_HEADS_PER_STEP = 8
DN_CHUNKS_PER_STEP = 4


def _delta_kernel(q_ref, k_ref, v_ref, z_ref, bg_ref, gt_ref, ng_ref, o_ref, s_ref):
    C = DN_CHUNK
    HD = HEAD_DIM
    nh, nc = DN_HEADS_PER_STEP, DN_CHUNKS_PER_STEP
    items = [(c, j) for c in range(nc) for j in range(nh)]

    @pl.when(pl.program_id(2) == 0)
    def _():
        s_ref[...] = jnp.zeros_like(s_ref)

    ri = lax.broadcasted_iota(I32, (C, C), 0)
    ci = lax.broadcasted_iota(I32, (C, C), 1)
    incl = ri >= ci
    strict = ri > ci
    eye = (ri == ci).astype(F32)
    nt = (((1,), (1,)), ((), ()))

    k_t = {}
    for c in range(nc):
        rows = slice(c * C, (c + 1) * C)
        for jq in range(nh // 2):
            k_t[c, jq] = k_ref[0, rows, jq * HD:(jq + 1) * HD].T

    bmat, qmat, qk, rhs, qg, kdec_t, glast = {}, {}, {}, {}, {}, {}, {}
    for it in items:
        c, j = it
        rows = slice(c * C, (c + 1) * C)
        qsl = slice((j // 2) * HD, (j // 2 + 1) * HD)
        q = q_ref[0, rows, qsl]
        k = k_ref[0, rows, qsl]
        v = v_ref[0, rows, j * HD:(j + 1) * HD]
        bcol = bg_ref[0, 0, rows, j:j + 1]
        gcol = bg_ref[0, 0, rows, nh + j:nh + j + 1]
        grow = gt_ref[0, 0, j:j + 1, rows]
        glast[it] = gcol[C - 1:C, :]
        decay = jnp.exp(jnp.where(incl, gcol - grow, -jnp.inf))
        kb = k * bcol
        eg = jnp.exp(gcol)
        m1 = lax.dot_general(jnp.concatenate([kb, q], axis=0).astype(BF16), k.astype(BF16), nt,
                             preferred_element_type=F32)
        bmat[it] = jnp.where(strict, -(m1[:C] * decay), 0.0)
        qk[it] = (m1[C:] * decay).astype(BF16)
        rhs[it] = jnp.concatenate([v * bcol, kb * eg], axis=1).astype(BF16)
        qg[it] = (q * eg).astype(BF16)
        kdec_t[it] = (k_t[c, j // 2] * jnp.exp(glast[it] - grow)).astype(BF16)

    for it in items:
        b0 = bmat[it].astype(BF16)
        qmat[it] = eye + bmat[it]
        bmat[it] = jnp.dot(b0, b0, preferred_element_type=F32)
    for _ in range(4):
        for it in items:
            r = jnp.dot(jnp.concatenate([qmat[it], bmat[it]], axis=0).astype(BF16), bmat[it].astype(BF16),
                        preferred_element_type=F32)
            qmat[it] = qmat[it] + r[:C]
            bmat[it] = r[C:]
    uw = {}
    for it in items:
        t_inv = qmat[it] + jnp.dot(qmat[it].astype(BF16), bmat[it].astype(BF16), preferred_element_type=F32)
        uw[it] = jnp.dot(t_inv.astype(BF16), rhs[it], preferred_element_type=F32)

    states = [s_ref[j] for j in range(nh)]
    ng = ng_ref[...]
    for c in range(nc):
        rows = slice(c * C, (c + 1) * C)
        r1 = {}
        for j in range(nh):
            lhs = jnp.concatenate([uw[c, j][:, HD:].astype(BF16), qg[c, j]], axis=0)
            r1[j] = jnp.dot(lhs, states[j].astype(BF16), preferred_element_type=F32)
        r2 = {}
        for j in range(nh):
            v_new = uw[c, j][:, :HD] - r1[j][:C]
            r2[j] = jnp.dot(jnp.concatenate([qk[c, j], kdec_t[c, j]], axis=0), v_new.astype(BF16),
                            preferred_element_type=F32)
        for j in range(nh):
            o = r1[j][C:] + r2[j][:C]
            states[j] = states[j] * jnp.exp(glast[c, j]) + r2[j][C:]
            o = o * lax.rsqrt(jnp.mean(o * o, axis=-1, keepdims=True) + DN_EPS) * ng
            zz = z_ref[0, rows, j * HD:(j + 1) * HD]
            o_ref[0, rows, j * HD:(j + 1) * HD] = (o * (zz * jax.nn.sigmoid(zz))).astype(o_ref.dtype)

    for j in range(nh):
        s_ref[j] = states[j]


def _delta_rule(qkvz, bg, norm_g, B, S):
    HD = HEAD_DIM
    hps = DN_HEADS_PER_STEP
    T = DN_CHUNKS_PER_STEP * DN_CHUNK
    q_w = DN_QK_HEADS * HD
    v_w = DN_V_HEADS * HD
    qkw = hps // 2 * HD
    vw = hps * HD
    n_hg = DN_V_HEADS // hps
    x3 = qkvz.reshape(B, S, qkvz.shape[1])
    beta = jnp.transpose(bg[:, :DN_V_HEADS].reshape(B, S, n_hg, hps), (0, 2, 1, 3))
    gcum = jnp.transpose(bg[:, DN_V_HEADS:2 * DN_V_HEADS].reshape(B, S, n_hg, hps), (0, 2, 1, 3))
    bg4 = jnp.concatenate([beta, gcum], axis=-1)
    gt4 = jnp.transpose(gcum, (0, 1, 3, 2))
    return pl.pallas_call(
        _delta_kernel,
        grid=(B, n_hg, S // T),
        in_specs=[pl.BlockSpec((1, T, qkw), lambda b, h, t: (b, t, h)),
                  pl.BlockSpec((1, T, qkw), lambda b, h, t: (b, t, q_w // qkw + h)),
                  pl.BlockSpec((1, T, vw), lambda b, h, t: (b, t, 2 * q_w // vw + h)),
                  pl.BlockSpec((1, T, vw), lambda b, h, t: (b, t, (2 * q_w + v_w) // vw + h)),
                  pl.BlockSpec((1, 1, T, 2 * hps), lambda b, h, t: (b, h, t, 0)),
                  pl.BlockSpec((1, 1, hps, T), lambda b, h, t: (b, h, 0, t)),
                  pl.BlockSpec((1, HD), lambda b, h, t: (0, 0))],
        out_specs=pl.BlockSpec((1, T, vw), lambda b, h, t: (b, t, h)),
        out_shape=jax.ShapeDtypeStruct((B, S, v_w), BF16),
        scratch_shapes=[pltpu.VMEM((hps, HD, HD), F32)],
        compiler_params=_params("arbitrary", "arbitrary", "arbitrary"),
        name="delta_rule",
    )(x3, x3, x3, x3, bg4, gt4, norm_g.reshape(1, HD)).reshape(B * S, v_w)


def kernel(x, rel_bias, attn_w_in, attn_w_out, dn_w_in, dn_conv_w, dn_a_log, dn_dt_bias, dn_norm_g, dn_w_out,
           ln1_g, ln1_b, router_w, router_b, w_gate_up, b_gate_up, w_down, b_down, ln2_g, ln2_b):
    B, S, D = x.shape
    N = B * S
    E = router_w.shape[2]
    x2d = x.reshape(N, D)

    def vec(p):
        return p.reshape(1, -1)

    def router(i):
        return jnp.transpose(router_w[i]).astype(BF16), router_b[i].reshape(E, 1)

    def moe(i, post):
        return _moe_block(post, i, w_gate_up, b_gate_up, w_down, b_down, vec(ln2_g[i]), vec(ln2_b[i]))

    w_in = attn_w_in[0].astype(BF16)
    os_, lses = [], []
    for g, (window, dilation) in enumerate(DILATED_GROUPS):
        assert window // dilation == ATTN_BLOCK
        qkv = _attn_inproj(x2d, w_in, g, dilation, B, S)
        bias = _attn_bias(rel_bias[:, g * ATTN_HEADS:(g + 1) * ATTN_HEADS], dilation)
        o, lse = _dilated_group(qkv, bias, g)
        os_.append(o)
        lses.append(lse)
    rwt, rb = router(0)
    post = _attn_out(os_, lses, attn_w_out[0].astype(BF16), x2d, vec(ln1_g[0]), vec(ln1_b[0]), rwt, rb, S)
    x2, x2b = moe(0, post)

    conv_cols = dn_conv_w.shape[2]
    main_cols = conv_cols + DN_V_HEADS * HEAD_DIM
    w_in = dn_w_in[0].astype(BF16)
    qkvz = _dn_inproj(x2b, w_in[:, :main_cols], dn_conv_w[0], S)
    bg = _dn_gates(x2b, w_in[:, main_cols:], dn_a_log[0], dn_dt_bias[0])
    o = _delta_rule(qkvz, bg, dn_norm_g[0], B, S)
    rwt, rb = router(1)
    post = _dn_out(o, dn_w_out[0].astype(BF16), x2, vec(ln1_g[1]), vec(ln1_b[1]), rwt, rb)
    x3, _ = moe(1, post)
    return x3.reshape(B, S, D)
```

```python
import functools
import math

import jax
import jax.numpy as jnp
from jax import lax
from jax.experimental import pallas as pl
from jax.experimental.pallas import tpu as pltpu

F32 = jnp.float32
BF16 = jnp.bfloat16
I32 = jnp.int32
U32 = jnp.uint32

DEPTH = 2
DEEPNORM_ALPHA = (2.0 * DEPTH) ** 0.25
LN_EPS = 1e-5
HEAD_DIM = 128
ATTN_HEADS = 8
ATTN_WIDTH = ATTN_HEADS * HEAD_DIM
DILATED_GROUPS = ((128, 1), (512, 4), (2048, 16))
ATTN_BLOCK = 128
MAX_DISTANCE = 2048
DN_QK_HEADS = 16
DN_V_HEADS = 32
DN_CONV = 4
DN_CHUNK = 64
DN_EPS = 1e-6
TOP_K = 4
SWIGLU_LIMIT = 7.0
SWIGLU_ALPHA = 1.702

MOE_ROWS = 1024
MOE_HALF = 512
MOE_FF_CHUNK = 256
ROW_TILE = 256
VMEM_LIMIT = 58 * 1024 * 1024


def _params(*semantics):
    return pltpu.CompilerParams(dimension_semantics=semantics, vmem_limit_bytes=VMEM_LIMIT)


def _pack_bf16_pair(lo, hi):
    lo_bits = lax.bitcast_convert_type(lo.astype(BF16).astype(F32), U32)
    hi_bits = lax.bitcast_convert_type(hi.astype(BF16).astype(F32), U32)
    return (hi_bits & jnp.uint32(0xFFFF0000)) | (lo_bits >> 16)


def _unpack_bf16_pair(u):
    lo = lax.bitcast_convert_type(u << 16, F32)
    hi = lax.bitcast_convert_type(u & jnp.uint32(0xFFFF0000), F32)
    return lo, hi


def _attn_inproj_kernel(a_ref, w_ref, o_ref, acc_ref, *, d, tm):
    res = jnp.dot(a_ref[...].astype(BF16), w_ref[...], preferred_element_type=F32)
    if d == 1:
        o_ref[0, 0] = res.astype(o_ref.dtype)
        return
    lanes = acc_ref.shape[2]
    for cb in range(acc_ref.shape[0]):
        acc_ref[cb] = res[:, cb * lanes:(cb + 1) * lanes]
    for r in range(d):
        for cb in range(acc_ref.shape[0]):
            o_ref[0, r, :, cb * lanes:(cb + 1) * lanes] = (
                acc_ref[cb, pl.ds(r, tm // d, stride=d), :].astype(o_ref.dtype))


def _attn_inproj(x2d, w, g, d, B, S):
    K = x2d.shape[1]
    tm, tn = 512, 1536
    gw = 3 * ATTN_WIDTH
    nt = gw // tn
    mt = S // tm
    return pl.pallas_call(
        functools.partial(_attn_inproj_kernel, d=d, tm=tm),
        grid=(nt, B, mt),
        in_specs=[pl.BlockSpec((tm, K), lambda n, b, m: (b * mt + m, 0)),
                  pl.BlockSpec((K, tn), lambda n, b, m: (0, g * nt + n))],
        out_specs=pl.BlockSpec((1, d, tm // d, tn), lambda n, b, m: (b, 0, m, n)),
        out_shape=jax.ShapeDtypeStruct((B, d, S // d, gw), BF16),
        scratch_shapes=[pltpu.VMEM((tn // HEAD_DIM, tm, HEAD_DIM), F32)],
        compiler_params=_params("arbitrary", "arbitrary", "arbitrary"),
        name=f"attn_inproj_g{g}",
    )(x2d, w)


def _t5_bucket(dist, num_buckets):
    max_exact = num_buckets // 2
    d_f = jnp.maximum(dist, 1).astype(F32)
    large = max_exact + (jnp.log(d_f / max_exact) / math.log(MAX_DISTANCE / max_exact)
                         * (num_buckets - max_exact)).astype(I32)
    large = jnp.minimum(large, num_buckets - 1)
    return jnp.where(dist < max_exact, dist, large)


def _attn_bias(rel_bias_g, dilation):
    bq = ATTN_BLOCK
    a = jnp.arange(bq, dtype=I32)[:, None]
    c = jnp.arange(2 * bq, dtype=I32)[None, :]
    steps = a + bq - c
    valid = (steps >= 0) & (steps <= bq)
    bucket = _t5_bucket(jnp.maximum(steps, 0) * dilation, rel_bias_g.shape[0])
    bias = jnp.transpose(rel_bias_g[bucket], (2, 0, 1)).astype(F32)
    return jnp.where(valid[None], bias, -jnp.inf)


def _attn_kernel(q_ref, kp_ref, kc_ref, vp_ref, vc_ref, bias_ref, o_ref, lse_ref):
    bq = ATTN_BLOCK
    first = pl.program_id(2) == 0
    scale = HEAD_DIM ** -0.5
    nt = (((1,), (1,)), ((), ()))
    lse_ref[0, 0] = jnp.zeros(lse_ref.shape[2:], F32)
    heads = range(ATTN_HEADS)
    sls = [slice(h * HEAD_DIM, (h + 1) * HEAD_DIM) for h in heads]
    s_p = [lax.dot_general(q_ref[0, 0, :, sls[h]], kp_ref[0, 0, :, sls[h]], nt, preferred_element_type=F32)
           for h in heads]
    s_c = [lax.dot_general(q_ref[0, 0, :, sls[h]], kc_ref[0, 0, :, sls[h]], nt, preferred_element_type=F32)
           for h in heads]
    p_p, p_c = [], []
    for h in heads:
        sp = jnp.where(first, -jnp.inf, s_p[h] * scale + bias_ref[h, :, :bq])
        sc = s_c[h] * scale + bias_ref[h, :, bq:]
        m = jnp.maximum(jnp.max(sp, axis=-1, keepdims=True), jnp.max(sc, axis=-1, keepdims=True))
        e_p = jnp.exp(sp - m)
        e_c = jnp.exp(sc - m)
        l = jnp.sum(e_p, axis=-1, keepdims=True) + jnp.sum(e_c, axis=-1, keepdims=True)
        inv = 1.0 / l
        p_p.append((e_p * inv).astype(BF16))
        p_c.append((e_c * inv).astype(BF16))
        lse_ref[0, 0, :, h:h + 1] = m + jnp.log(l)
    for h in heads:
        o_ref[0, 0, :, sls[h]] = (jnp.dot(p_p[h], vp_ref[0, 0, :, sls[h]], preferred_element_type=F32)
                                  + jnp.dot(p_c[h], vc_ref[0, 0, :, sls[h]], preferred_element_type=F32))


def _dilated_group(qkv, bias, g):
    B, d, L, _ = qkv.shape
    nb = L // ATTN_BLOCK
    blk = (1, 1, ATTN_BLOCK, ATTN_WIDTH)
    lse_blk = (1, 1, ATTN_BLOCK, HEAD_DIM)

    def spec(kind, prev):
        if prev:
            return pl.BlockSpec(blk, lambda b, r, i: (b, r, jnp.maximum(i - 1, 0), kind))
        return pl.BlockSpec(blk, lambda b, r, i: (b, r, i, kind))

    return pl.pallas_call(
        _attn_kernel,
        grid=(B, d, nb),
        in_specs=[spec(0, False), spec(1, True), spec(1, False), spec(2, True), spec(2, False),
                  pl.BlockSpec(bias.shape, lambda b, r, i: (0, 0, 0))],
        out_specs=[pl.BlockSpec(blk, lambda b, r, i: (b, r, i, 0)),
                   pl.BlockSpec(lse_blk, lambda b, r, i: (b, r, i, 0))],
        out_shape=[jax.ShapeDtypeStruct((B, d, L, ATTN_WIDTH), F32),
                   jax.ShapeDtypeStruct((B, d, L, HEAD_DIM), F32)],
        compiler_params=_params("arbitrary", "arbitrary", "arbitrary"),
        name=f"dilated_attn_g{g}",
    )(qkv, qkv, qkv, qkv, qkv, bias)


def _layer_norm(y, g_ref, b_ref):
    mu = jnp.mean(y, axis=-1, keepdims=True)
    yc = y - mu
    var = jnp.mean(yc * yc, axis=-1, keepdims=True)
    return yc * lax.rsqrt(var + LN_EPS) * g_ref[...] + b_ref[...]


def _residual_ln_router(h, x_ref, g_ref, b_ref, rwt_ref, rb_ref,
                        x1_ref, x1p_ref, topi_ref, gate_ref, rank_ref, cnt_ref, carry_ref, first_step):
    xn = _layer_norm(DEEPNORM_ALPHA * x_ref[...] + h, g_ref, b_ref)
    x1_ref[...] = xn
    half = xn.shape[1] // 2
    x1p_ref[...] = _pack_bf16_pair(xn[:, :half], xn[:, half:])
    logits = lax.dot_general(rwt_ref[...], xn.astype(BF16), (((1,), (1,)), ((), ())),
                             preferred_element_type=F32) + rb_ref[...]
    n_exp, tm = logits.shape
    eidx = lax.broadcasted_iota(I32, logits.shape, 0)
    vals, hots = [], []
    for k in range(TOP_K):
        mx = jnp.max(logits, axis=0, keepdims=True)
        idx = jnp.min(jnp.where(logits == mx, eidx, n_exp), axis=0, keepdims=True)
        topi_ref[k:k + 1, :] = idx
        vals.append(mx)
        hit = eidx == idx
        hots.append(hit.astype(F32))
        logits = jnp.where(hit, -jnp.inf, logits)
    exps = [jnp.exp(v - vals[0]) for v in vals]
    denom = exps[0] + exps[1] + exps[2] + exps[3]
    for k in range(TOP_K):
        gate_ref[k:k + 1, :] = exps[k] / denom

    @pl.when(first_step)
    def _():
        carry_ref[...] = jnp.zeros_like(carry_ref)

    tri = (lax.broadcasted_iota(I32, (tm, tm), 0) <= lax.broadcasted_iota(I32, (tm, tm), 1)).astype(BF16)
    cum = jnp.dot(jnp.concatenate(hots, axis=0).astype(BF16), tri, preferred_element_type=F32)
    base = carry_ref[...]
    for k in range(TOP_K):
        cum_k = cum[k * n_exp:(k + 1) * n_exp]
        before = cum_k - hots[k] + base
        rank_ref[k:k + 1, :] = jnp.sum(hots[k] * before, axis=0, keepdims=True).astype(I32)
        base = base + cum_k[:, tm - 1:tm]
    carry_ref[...] = base
    cnt_ref[...] = jnp.broadcast_to(base, cnt_ref.shape)


def _attn_out_kernel(o0_ref, o1_ref, o2_ref, l0_ref, l1_ref, l2_ref, w_ref, x_ref, g_ref, b_ref, rwt_ref, rb_ref,
                     x1_ref, x1p_ref, topi_ref, gate_ref, rank_ref, cnt_ref,
                     o_scr, l_scr, merged_ref, carry_ref, *, tm):
    o_refs = (o0_ref, o1_ref, o2_ref)
    l_refs = (l0_ref, l1_ref, l2_ref)
    H = ATTN_HEADS
    for g, (_, d) in enumerate(DILATED_GROUPS):
        for r in range(d):
            for h in range(H):
                o_scr[g * H + h, pl.ds(r, tm // d, stride=d), :] = o_refs[g][0, r, :, h * HEAD_DIM:(h + 1) * HEAD_DIM]
            l_scr[g, pl.ds(r, tm // d, stride=d), :] = l_refs[g][0, r]
    l0, l1, l2 = l_scr[0], l_scr[1], l_scr[2]
    m = jnp.maximum(jnp.maximum(l0, l1), l2)
    e = [jnp.exp(l0 - m), jnp.exp(l1 - m), jnp.exp(l2 - m)]
    den = e[0] + e[1] + e[2]
    wts = [ei / den for ei in e]
    for h in range(H):
        o = wts[0][:, h:h + 1] * o_scr[h]
        o = o + wts[1][:, h:h + 1] * o_scr[H + h]
        o = o + wts[2][:, h:h + 1] * o_scr[2 * H + h]
        merged_ref[:, h * HEAD_DIM:(h + 1) * HEAD_DIM] = o.astype(BF16)
    h_out = jnp.dot(merged_ref[...], w_ref[...], preferred_element_type=F32)
    _residual_ln_router(h_out, x_ref, g_ref, b_ref, rwt_ref, rb_ref, x1_ref, x1p_ref, topi_ref, gate_ref,
                        rank_ref, cnt_ref, carry_ref, pl.program_id(0) == 0)


def _dn_out_kernel(o_ref, w_ref, x_ref, g_ref, b_ref, rwt_ref, rb_ref,
                   x1_ref, x1p_ref, topi_ref, gate_ref, rank_ref, cnt_ref, carry_ref):
    h = jnp.dot(o_ref[...], w_ref[...], preferred_element_type=F32)
    _residual_ln_router(h, x_ref, g_ref, b_ref, rwt_ref, rb_ref, x1_ref, x1p_ref, topi_ref,
                        gate_ref, rank_ref, cnt_ref, carry_ref, pl.program_id(0) == 0)


def _post_specs(tm, D, E, idx):
    row = lambda *a: (idx(*a), 0)
    const = lambda *a: (0, 0)
    col = lambda *a: (0, idx(*a))
    ins = [pl.BlockSpec((tm, D), row),
           pl.BlockSpec((1, D), const), pl.BlockSpec((1, D), const),
           pl.BlockSpec((E, D), const), pl.BlockSpec((E, 1), const)]
    outs = [pl.BlockSpec((tm, D), row), pl.BlockSpec((tm, D // 2), row),
            pl.BlockSpec((TOP_K, tm), col), pl.BlockSpec((TOP_K, tm), col), pl.BlockSpec((TOP_K, tm), col),
            pl.BlockSpec((E, 128), const)]
    return ins, outs


def _post_shapes(N, D, E):
    return [jax.ShapeDtypeStruct((N, D), F32), jax.ShapeDtypeStruct((N, D // 2), U32),
            jax.ShapeDtypeStruct((TOP_K, N), I32), jax.ShapeDtypeStruct((TOP_K, N), F32),
            jax.ShapeDtypeStruct((TOP_K, N), I32), jax.ShapeDtypeStruct((E, 128), F32)]


def _attn_out(os_, lses, w_out, x2d, ln_g, ln_b, rwt, rb, S):
    N, D = x2d.shape
    E = rwt.shape[0]
    tm = 256
    mt = S // tm
    ins, outs = _post_specs(tm, D, E, lambda m: m)
    grp_specs = []
    for width in (ATTN_WIDTH, HEAD_DIM):
        for _, d in DILATED_GROUPS:
            grp_specs.append(pl.BlockSpec((1, d, tm // d, width), lambda m: (m // mt, 0, m % mt, 0)))
    return pl.pallas_call(
        functools.partial(_attn_out_kernel, tm=tm),
        grid=(N // tm,),
        in_specs=grp_specs + [pl.BlockSpec(w_out.shape, lambda m: (0, 0))] + ins,
        out_specs=outs,
        out_shape=_post_shapes(N, D, E),
        scratch_shapes=[pltpu.VMEM((len(DILATED_GROUPS) * ATTN_HEADS, tm, HEAD_DIM), F32),
                        pltpu.VMEM((len(DILATED_GROUPS), tm, HEAD_DIM), F32),
                        pltpu.VMEM((tm, ATTN_WIDTH), BF16),
                        pltpu.VMEM((E, 1), F32)],
        compiler_params=_params("arbitrary"),
        name="attn_out_ln_router",
    )(*os_, *lses, w_out, x2d, ln_g, ln_b, rwt, rb)


def _dn_out(o, w_out, x2d, ln_g, ln_b, rwt, rb):
    N, D = x2d.shape
    E = rwt.shape[0]
    Kin = o.shape[1]
    tm = 256
    ins, outs = _post_specs(tm, D, E, lambda m: m)
    return pl.pallas_call(
        _dn_out_kernel,
        grid=(N // tm,),
        in_specs=[pl.BlockSpec((tm, Kin), lambda m: (m, 0)),
                  pl.BlockSpec((Kin, D), lambda m: (0, 0), pipeline_mode=pl.Buffered(1))] + ins,
        out_specs=outs,
        out_shape=_post_shapes(N, D, E),
        scratch_shapes=[pltpu.VMEM((E, 1), F32)],
        compiler_params=_params("arbitrary"),
        name="dn_out_ln_router",
    )(o, w_out, x2d, ln_g, ln_b, rwt, rb)


def _slot_layout(topi, rank, counts, rows):
    K, N = topi.shape
    n_exp = counts.shape[0]
    padded = (counts + rows - 1) // rows * rows
    padded_end = jnp.cumsum(padded)
    padded_start = padded_end - padded
    sel = topi[None] == jnp.arange(n_exp, dtype=I32)[:, None, None]
    pos = jnp.sum(jnp.where(sel, padded_start[:, None, None], 0), axis=0) + rank
    n_blocks = -(-(K * N + n_exp * (rows - 1)) // rows)
    n_used = padded_end[-1] // rows
    blk = jnp.arange(n_blocks, dtype=I32)
    be = jnp.sum((blk[:, None] * rows >= padded_end[None, :]).astype(I32), axis=1)
    be = jnp.minimum(be, n_exp - 1)
    be_last = jnp.sum(jnp.where(blk == jnp.maximum(n_used - 1, 0), be, 0))
    used = blk < n_used
    be = jnp.where(used, be, be_last)
    start_b = jnp.sum(jnp.where(be[:, None] == jnp.arange(n_exp)[None, :], padded_start[None, :], 0), axis=1)
    cnt_e = jnp.sum(jnp.where(be[:, None] == jnp.arange(n_exp)[None, :], counts[None, :], 0), axis=1)
    valid = jnp.where(used, jnp.clip(cnt_e - (blk * rows - start_b), 0, rows), 0).astype(I32)
    n_slots = jnp.full((1,), n_blocks * rows, I32)
    pad_lo = jnp.concatenate([padded_start + counts, padded_end[-1:]]).astype(I32)
    pad_hi = jnp.concatenate([padded_end, n_slots]).astype(I32)
    return pos.astype(I32), be.astype(I32), valid, pad_lo, pad_hi, n_blocks


def _tile_major(pos, tm):
    K, N = pos.shape
    return jnp.transpose(pos.reshape(K, N // tm, tm), (1, 0, 2)).reshape(N // tm, K * tm)


def _dispatch_kernel(lo_ref, hi_ref, pos_hbm, x_ref, xs_hbm, idx_smem, zbuf, sem_idx, sem_rows, sem_zero, *, tm):
    i = pl.program_id(0)

    @pl.when(i == 0)
    def _():
        zbuf[...] = jnp.zeros_like(zbuf)
        one = zbuf.at[pl.ds(0, 1)]

        def per_expert(e, carry):
            lo = lo_ref[e]
            hi = hi_ref[e]
            lo_al = jnp.minimum((lo + 7) // 8 * 8, hi)

            def row_start(r, c):
                pltpu.make_async_copy(one, xs_hbm.at[pl.ds(r, 1)], sem_zero).start()
                return c

            def grp_start(q, c):
                pltpu.make_async_copy(zbuf, xs_hbm.at[pl.ds(pl.multiple_of(q * 8, 8), 8)], sem_zero).start()
                return c

            def row_wait(r, c):
                pltpu.make_async_copy(one, xs_hbm.at[pl.ds(0, 1)], sem_zero).wait()
                return c

            def grp_wait(q, c):
                pltpu.make_async_copy(zbuf, xs_hbm.at[pl.ds(0, 8)], sem_zero).wait()
                return c

            lax.fori_loop(lo, lo_al, row_start, 0)
            lax.fori_loop(lo_al // 8, hi // 8, grp_start, 0)
            lax.fori_loop(lo, lo_al, row_wait, 0)
            lax.fori_loop(lo_al // 8, hi // 8, grp_wait, 0)
            return carry

        lax.fori_loop(0, lo_ref.shape[0], per_expert, 0)

    cp = pltpu.make_async_copy(pos_hbm.at[i], idx_smem, sem_idx)
    cp.start()
    cp.wait()

    def issue(jj, carry):
        for u in range(8):
            j = jj * 8 + u
            for k in range(TOP_K):
                dst = idx_smem[k * tm + j]
                pltpu.make_async_copy(x_ref.at[pl.ds(j, 1)], xs_hbm.at[pl.ds(dst, 1)], sem_rows).start()
        return carry

    lax.fori_loop(0, tm // 8, issue, 0)
    for k in range(TOP_K):
        pltpu.make_async_copy(x_ref, xs_hbm.at[pl.ds(0, tm)], sem_rows).wait()


def _dispatch(x1p, pos_tiles, pad_lo, pad_hi, n_slots):
    N, W = x1p.shape
    tm = ROW_TILE
    grid_spec = pltpu.PrefetchScalarGridSpec(
        num_scalar_prefetch=2,
        grid=(N // tm,),
        in_specs=[pl.BlockSpec(memory_space=pl.ANY),
                  pl.BlockSpec((tm, W), lambda i, lo, hi: (i, 0))],
        out_specs=pl.BlockSpec(memory_space=pl.ANY),
        scratch_shapes=[pltpu.SMEM((TOP_K * tm,), I32), pltpu.VMEM((8, W), U32),
                        pltpu.SemaphoreType.DMA, pltpu.SemaphoreType.DMA, pltpu.SemaphoreType.DMA],
    )
    return pl.pallas_call(
        functools.partial(_dispatch_kernel, tm=tm),
        grid_spec=grid_spec,
        out_shape=jax.ShapeDtypeStruct((n_slots, W), U32),
        compiler_params=_params("arbitrary"),
        name="moe_dispatch",
    )(pad_lo, pad_hi, pos_tiles, x1p)


def _ffn_kernel(be_ref, cnt_ref, xs_ref, wg_ref, wl_ref, bg_ref, bl_ref, wd_ref, bd_ref, o_ref, xb_ref, act_ref, *, nfc):
    b = pl.program_id(0)
    p = pl.program_id(1)
    cnt = cnt_ref[b]
    rows = xb_ref.shape[0]
    half = xs_ref.shape[1]
    fc = wg_ref.shape[3]
    n_half = rows // MOE_HALF
    up = p < nfc

    @pl.when(jnp.logical_and(cnt > 0, p == 0))
    def _():
        lo, hi = _unpack_bf16_pair(xs_ref[...])
        xb_ref[:, :half] = lo.astype(BF16)
        xb_ref[:, half:] = hi.astype(BF16)

    def up_phase(live):
        wg = wg_ref[0, 0].astype(BF16)
        wl = wl_ref[0, 0].astype(BF16)
        acts = []
        for h in range(live):
            x = xb_ref[h * MOE_HALF:(h + 1) * MOE_HALF, :]
            hg = jnp.dot(x, wg, preferred_element_type=F32) + bg_ref[0, 0]
            hl = jnp.dot(x, wl, preferred_element_type=F32) + bl_ref[0, 0]
            glu = jnp.minimum(hg, SWIGLU_LIMIT)
            lin = jnp.clip(hl, -SWIGLU_LIMIT, SWIGLU_LIMIT)
            acts.append((glu * jax.nn.sigmoid(SWIGLU_ALPHA * glu) * (lin + 1.0)).astype(BF16))
        for c in range(nfc):
            @pl.when(p == c)
            def _():
                for h in range(live):
                    act_ref[h * MOE_HALF:(h + 1) * MOE_HALF, c * fc:(c + 1) * fc] = acts[h]

    def down_phase(live):
        wd = wd_ref[0, 0].astype(BF16)
        for h in range(live):
            rs = slice(h * MOE_HALF, (h + 1) * MOE_HALF)
            o_ref[rs, :] = jnp.dot(act_ref[rs, :], wd, preferred_element_type=F32) + bd_ref[0, 0]
        for h in range(live, n_half):
            rs = slice(h * MOE_HALF, (h + 1) * MOE_HALF)
            o_ref[rs, :] = jnp.zeros((MOE_HALF, fc), F32)

    for live in range(1, n_half + 1):
        in_range = cnt > (live - 1) * MOE_HALF
        if live < n_half:
            in_range = jnp.logical_and(in_range, cnt <= live * MOE_HALF)

        @pl.when(jnp.logical_and(in_range, up))
        def _():
            up_phase(live)

        @pl.when(jnp.logical_and(in_range, jnp.logical_not(up)))
        def _():
            down_phase(live)

    @pl.when(jnp.logical_and(cnt == 0, jnp.logical_not(up)))
    def _():
        down_phase(0)


def _expert_ffn(xs, be, valid, layer, w_gate_up, b_gate_up, w_down, b_down):
    n_slots, half = xs.shape
    D = 2 * half
    _, E, _, F2 = w_gate_up.shape
    F = F2 // 2
    rows, fc = MOE_ROWS, MOE_FF_CHUNK
    nfc = F // fc
    ndc = D // fc
    n_blocks = n_slots // rows

    def up_chunk(b, p, cnt):
        return jnp.where(cnt[b] > 0, jnp.minimum(p, nfc - 1), nfc - 1)

    def down_chunk(b, p, cnt):
        return jnp.where(cnt[b] > 0, jnp.maximum(p - nfc, 0), ndc - 1)

    grid_spec = pltpu.PrefetchScalarGridSpec(
        num_scalar_prefetch=2,
        grid=(n_blocks, nfc + ndc),
        in_specs=[
            pl.BlockSpec((rows, half), lambda b, p, be, cnt: (b, 0)),
            pl.BlockSpec((1, 1, D, fc), lambda b, p, be, cnt: (layer, be[b], 0, up_chunk(b, p, cnt))),
            pl.BlockSpec((1, 1, D, fc), lambda b, p, be, cnt: (layer, be[b], 0, nfc + up_chunk(b, p, cnt))),
            pl.BlockSpec((1, 1, 1, fc), lambda b, p, be, cnt: (layer, be[b], 0, up_chunk(b, p, cnt))),
            pl.BlockSpec((1, 1, 1, fc), lambda b, p, be, cnt: (layer, be[b], 0, nfc + up_chunk(b, p, cnt))),
            pl.BlockSpec((1, 1, F, fc), lambda b, p, be, cnt: (layer, be[b], 0, down_chunk(b, p, cnt))),
            pl.BlockSpec((1, 1, 1, fc), lambda b, p, be, cnt: (layer, be[b], 0, down_chunk(b, p, cnt))),
        ],
        out_specs=pl.BlockSpec((rows, fc), lambda b, p, be, cnt: (b, jnp.maximum(p - nfc, 0))),
        scratch_shapes=[pltpu.VMEM((rows, D), BF16), pltpu.VMEM((rows, F), BF16)],
    )
    L = w_gate_up.shape[0]
    return pl.pallas_call(
        functools.partial(_ffn_kernel, nfc=nfc),
        grid_spec=grid_spec,
        out_shape=jax.ShapeDtypeStruct((n_slots, D), F32),
        compiler_params=_params("arbitrary", "arbitrary"),
        name="expert_ffn",
    )(be, valid, xs, w_gate_up, w_gate_up, b_gate_up.reshape(L, E, 1, F2), b_gate_up.reshape(L, E, 1, F2),
      w_down, b_down.reshape(L, E, 1, D))


def _combine_kernel(pos_hbm, ys_hbm, gate_ref, x_ref, g_ref, b_ref, x2_ref, x2b_ref,
                    idx_smem, buf, sem_idx, sem_rows, *, tm):
    i = pl.program_id(0)
    n = pl.num_programs(0)

    def gather(tile, slot):
        cp = pltpu.make_async_copy(pos_hbm.at[tile], idx_smem.at[slot], sem_idx.at[slot])
        cp.start()
        cp.wait()

        def issue(jj, carry):
            for u in range(8):
                j = jj * 8 + u
                for k in range(TOP_K):
                    src = idx_smem[slot, k * tm + j]
                    pltpu.make_async_copy(ys_hbm.at[pl.ds(src, 1)], buf.at[slot, k, pl.ds(j, 1)],
                                          sem_rows.at[slot]).start()
            return carry

        lax.fori_loop(0, tm // 8, issue, 0)

    @pl.when(i == 0)
    def _():
        gather(0, 0)

    slot = i % 2

    @pl.when(i + 1 < n)
    def _():
        gather(i + 1, 1 - slot)

    for k in range(TOP_K):
        pltpu.make_async_copy(ys_hbm.at[pl.ds(0, tm)], buf.at[slot, k], sem_rows.at[slot]).wait()
    gate = gate_ref[...]
    f = buf[slot, 0] * gate[:, 0:1]
    for k in range(1, TOP_K):
        f = f + buf[slot, k] * gate[:, k:k + 1]
    xn = _layer_norm(DEEPNORM_ALPHA * x_ref[...] + f, g_ref, b_ref)
    x2_ref[...] = xn
    x2b_ref[...] = xn.astype(BF16)


def _combine_ln(ys, pos_tiles, gates_t, x1, ln_g, ln_b):
    N, D = x1.shape
    tm = ROW_TILE
    row = pl.BlockSpec((tm, D), lambda m: (m, 0))
    vec = pl.BlockSpec((1, D), lambda m: (0, 0))
    anyspec = pl.BlockSpec(memory_space=pl.ANY)
    return pl.pallas_call(
        functools.partial(_combine_kernel, tm=tm),
        grid=(N // tm,),
        in_specs=[anyspec, anyspec, pl.BlockSpec((tm, TOP_K), lambda m: (m, 0)), row, vec, vec],
        out_specs=[row, row],
        out_shape=[jax.ShapeDtypeStruct((N, D), F32), jax.ShapeDtypeStruct((N, D), BF16)],
        scratch_shapes=[pltpu.SMEM((2, TOP_K * tm), I32), pltpu.VMEM((2, TOP_K, tm, D), F32),
                        pltpu.SemaphoreType.DMA((2,)), pltpu.SemaphoreType.DMA((2,))],
        compiler_params=_params("arbitrary"),
        name="moe_combine_ln",
    )(pos_tiles, ys, gates_t, x1, ln_g, ln_b)


def _moe_block(post, layer, w_gate_up, b_gate_up, w_down, b_down, ln_g, ln_b):
    x1, x1p, topi, gates, rank, cnt = post
    counts = cnt[:, 0].astype(I32)
    pos, be, valid, pad_lo, pad_hi, n_blocks = _slot_layout(topi, rank, counts, MOE_ROWS)
    pos_tiles = _tile_major(pos, ROW_TILE)
    xs = _dispatch(x1p, pos_tiles, pad_lo, pad_hi, n_blocks * MOE_ROWS)
    ys = _expert_ffn(xs, be, valid, layer, w_gate_up, b_gate_up, w_down, b_down)
    return _combine_ln(ys, pos_tiles, jnp.transpose(gates), x1, ln_g, ln_b)


DN_COL_SUB = 256


def _dn_inproj_kernel(x_ref, w_ref, cw_ref, o_ref, pbuf, *, mode, tm, tn, seq_tiles, q_tiles):
    n = pl.program_id(0)
    m = pl.program_id(1)
    if mode != "z":
        @pl.when(m % seq_tiles == 0)
        def _():
            pbuf[0:8, :] = jnp.zeros((8, tn), F32)

    x = x_ref[...]
    sc = jnp.where(n < q_tiles, HEAD_DIM ** -0.5, 1.0).astype(F32)
    for cs in range(tn // DN_COL_SUB):
        cols = slice(cs * DN_COL_SUB, (cs + 1) * DN_COL_SUB)
        p = jnp.dot(x, w_ref[:, cols], preferred_element_type=F32)
        if mode == "z":
            o_ref[:, cols] = p
            continue
        pbuf[8:tm + 8, cols] = p
        cw = cw_ref[:, cols]
        y = cw[0:1] * pbuf[5:tm + 5, cols]
        y = y + cw[1:2] * pbuf[6:tm + 6, cols]
        y = y + cw[2:3] * pbuf[7:tm + 7, cols]
        y = y + cw[3:4] * p
        pbuf[0:8, cols] = pbuf[tm:tm + 8, cols]
        y = y * jax.nn.sigmoid(y)
        if mode == "v":
            o_ref[:, cols] = y
            continue
        for hh in range(DN_COL_SUB // HEAD_DIM):
            sl = slice(hh * HEAD_DIM, (hh + 1) * HEAD_DIM)
            ys = y[:, sl]
            ss = jnp.sum(ys * ys, axis=-1, keepdims=True)
            o_ref[:, cs * DN_COL_SUB + hh * HEAD_DIM:cs * DN_COL_SUB + (hh + 1) * HEAD_DIM] = (
                ys * lax.rsqrt(ss + DN_EPS) * sc)


def _dn_inproj(xb, w, conv_w, seq_len, mode, col0, width):
    N, K = xb.shape
    tm, tn = 512, 1024
    t0 = col0 // tn
    q_w = DN_QK_HEADS * HEAD_DIM
    conv_tiles = conv_w.shape[1] // tn
    kern = functools.partial(_dn_inproj_kernel, mode=mode, tm=tm, tn=tn, seq_tiles=seq_len // tm,
                             q_tiles=q_w // tn if mode == "qk" else 0)
    return pl.pallas_call(
        kern,
        grid=(width // tn, N // tm),
        in_specs=[pl.BlockSpec((tm, K), lambda n, m: (m, 0)),
                  pl.BlockSpec((K, tn), lambda n, m: (0, t0 + n)),
                  pl.BlockSpec((DN_CONV, tn), lambda n, m: (0, jnp.minimum(t0 + n, conv_tiles - 1)))],
        out_specs=pl.BlockSpec((tm, tn), lambda n, m: (m, n)),
        out_shape=jax.ShapeDtypeStruct((N, width), F32),
        scratch_shapes=[pltpu.VMEM((tm + 8, tn), F32)],
        compiler_params=_params("arbitrary", "arbitrary"),
        name=f"dn_inproj_{mode}",
    )(xb, w, conv_w)


def _dn_gates_kernel(x_ref, w_ref, alog_ref, dtb_ref, o_ref):
    r = jnp.dot(x_ref[...], w_ref[...], preferred_element_type=F32)
    lane = lax.broadcasted_iota(I32, r.shape, 1)
    row = lax.broadcasted_iota(I32, r.shape, 0)
    beta = jax.nn.sigmoid(r)
    z = r + dtb_ref[...]
    softplus = jnp.maximum(z, 0.0) + jnp.log1p(jnp.exp(-jnp.abs(z)))
    g = -jnp.exp(alog_ref[...]) * softplus
    in_chunk = row % DN_CHUNK
    s = 1
    while s < DN_CHUNK:
        g = g + jnp.where(in_chunk >= s, pltpu.roll(g, s, axis=0), 0.0)
        s *= 2
    o_ref[...] = jnp.where(lane < DN_V_HEADS, beta, g)


def _dn_gates(xb, w_ba, a_log, dt_bias):
    N, K = xb.shape
    H = DN_V_HEADS
    tm = 512
    lanes = 128
    w_pad = jnp.zeros((K, lanes), BF16).at[:, :2 * H].set(w_ba)
    alog_pad = jnp.zeros((1, lanes), F32).at[0, H:2 * H].set(a_log)
    dtb_pad = jnp.zeros((1, lanes), F32).at[0, H:2 * H].set(dt_bias)
    vec = pl.BlockSpec((1, lanes), lambda m: (0, 0))
    return pl.pallas_call(
        _dn_gates_kernel,
        grid=(N // tm,),
        in_specs=[pl.BlockSpec((tm, K), lambda m: (m, 0)), pl.BlockSpec((K, lanes), lambda m: (0, 0)), vec, vec],
        out_specs=pl.BlockSpec((tm, lanes), lambda m: (m, 0)),
        out_shape=jax.ShapeDtypeStruct((N, lanes), F32),
        compiler_params=_params("arbitrary"),
        name="dn_gates",
    )(xb, w_pad, alog_pad, dtb_pad)


DN_HEADS_PER_STEP = 8
DN_CHUNKS_PER_STEP = 4


def _delta_kernel(q_ref, k_ref, v_ref, z_ref, bg_ref, gtp_ref, ng_ref, o_ref, s_ref):
    C = DN_CHUNK
    HD = HEAD_DIM
    nh, nc = DN_HEADS_PER_STEP, DN_CHUNKS_PER_STEP
    npair = nh // 2
    ngrp = nh // 4
    nt = (((1,), (1,)), ((), ()))

    @pl.when(pl.program_id(2) == 0)
    def _():
        s_ref[...] = jnp.zeros_like(s_ref)

    ri2 = lax.broadcasted_iota(I32, (C, 2 * C), 0)
    li2 = lax.broadcasted_iota(I32, (C, 2 * C), 1)
    left = li2 < C
    ci2 = li2 % C
    incl2 = ri2 >= ci2
    strict2 = ri2 > ci2
    left_row = lax.broadcasted_iota(I32, (1, 2 * C), 1) < C
    ri4 = lax.broadcasted_iota(I32, (C, 4 * C), 0)
    li4 = lax.broadcasted_iota(I32, (C, 4 * C), 1)
    eye4 = (ri4 == li4 % C).astype(F32)
    blk4 = (lax.broadcasted_iota(I32, (4 * C, 4 * C), 0) // C) == (lax.broadcasted_iota(I32, (4 * C, 4 * C), 1) // C)
    zero_hd = jnp.zeros((C, HD), BF16)
    zero_s = jnp.zeros((HD, HD), BF16)

    def block_diag4(x):
        return jnp.where(blk4, jnp.concatenate([x, x, x, x], axis=0), 0.0).astype(BF16)

    b2, lhs2, rhs, qg, glast, bcat, qcat = {}, {}, {}, {}, {}, {}, {}
    for c in range(nc):
        rows = slice(c * C, (c + 1) * C)
        for pj in range(npair):
            csl = slice(pj * HD, (pj + 1) * HD)
            q = q_ref[0, rows, csl]
            k = k_ref[0, rows, csl]
            kk = jnp.concatenate([k, k], axis=0)
            m1 = lax.dot_general(jnp.concatenate([k, q], axis=0).astype(BF16), kk.astype(BF16), nt,
                                 preferred_element_type=F32)
            ja, jb = 2 * pj, 2 * pj + 1
            bcol = [bg_ref[0, 0, rows, j:j + 1] for j in (ja, jb)]
            gcol = [bg_ref[0, 0, rows, nh + j:nh + j + 1] for j in (ja, jb)]
            grow2 = gtp_ref[0, 0, pj:pj + 1, c * 2 * C:(c + 1) * 2 * C]
            gcol2 = jnp.where(left, gcol[0], gcol[1])
            bcol2 = jnp.where(left, bcol[0], bcol[1])
            decay2 = jnp.exp(jnp.where(incl2, gcol2 - grow2, -jnp.inf))
            b2[c, pj] = jnp.where(strict2, -(m1[:C] * bcol2 * decay2), 0.0)
            qk2 = (m1[C:] * decay2).astype(BF16)
            gl = [g[C - 1:C, :] for g in gcol]
            glast2 = jnp.where(left_row, gl[0], gl[1])
            kdec_t2 = (kk.T * jnp.exp(glast2 - grow2)).astype(BF16)
            lhs2[c, pj] = jnp.concatenate([qk2, kdec_t2], axis=0)
            for i, j in enumerate((ja, jb)):
                eg = jnp.exp(gcol[i])
                v = v_ref[0, rows, j * HD:(j + 1) * HD]
                kb = k * bcol[i]
                rhs[c, j] = jnp.concatenate([v * bcol[i], kb * eg], axis=1).astype(BF16)
                qg[c, j] = (q * eg).astype(BF16)
                glast[c, j] = gl[i]

    groups = [(c, g) for c in range(nc) for g in range(ngrp)]
    for c, g in groups:
        bcat[c, g] = jnp.concatenate([b2[c, 2 * g], b2[c, 2 * g + 1]], axis=1)
        qcat[c, g] = eye4 + bcat[c, g]
    for it in groups:
        bcat[it] = jnp.dot(bcat[it].astype(BF16), block_diag4(bcat[it]), preferred_element_type=F32)
    for _ in range(4):
        for it in groups:
            r = jnp.dot(jnp.concatenate([qcat[it], bcat[it]], axis=0).astype(BF16), block_diag4(bcat[it]),
                        preferred_element_type=F32)
            qcat[it] = qcat[it] + r[:C]
            bcat[it] = r[C:]
    uw = {}
    for c, g in groups:
        it = (c, g)
        t_inv = qcat[it] + jnp.dot(qcat[it].astype(BF16), block_diag4(bcat[it]), preferred_element_type=F32)
        rstack = jnp.concatenate([rhs[c, 4 * g + i] for i in range(4)], axis=0)
        r = jnp.dot(block_diag4(t_inv), rstack, preferred_element_type=F32)
        for i in range(4):
            uw[c, 4 * g + i] = r[i * C:(i + 1) * C]

    states = [s_ref[j] for j in range(nh)]
    ng = ng_ref[...]
    for c in range(nc):
        rows = slice(c * C, (c + 1) * C)
        r1 = {}
        for pj in range(npair):
            ja, jb = 2 * pj, 2 * pj + 1
            lhs = jnp.concatenate(
                [jnp.concatenate([uw[c, j][:, HD:].astype(BF16), qg[c, j]], axis=0) for j in (ja, jb)], axis=1)
            sa, sb = states[ja].astype(BF16), states[jb].astype(BF16)
            sbd = jnp.concatenate([jnp.concatenate([sa, zero_s], axis=1),
                                   jnp.concatenate([zero_s, sb], axis=1)], axis=0)
            r1[pj] = jnp.dot(lhs, sbd, preferred_element_type=F32)
        r2 = {}
        for pj in range(npair):
            ja, jb = 2 * pj, 2 * pj + 1
            va = (uw[c, ja][:, :HD] - r1[pj][:C, :HD]).astype(BF16)
            vb = (uw[c, jb][:, :HD] - r1[pj][:C, HD:]).astype(BF16)
            vbd = jnp.concatenate([jnp.concatenate([va, zero_hd], axis=1),
                                   jnp.concatenate([zero_hd, vb], axis=1)], axis=0)
            r2[pj] = jnp.dot(lhs2[c, pj], vbd, preferred_element_type=F32)
        for pj in range(npair):
            for i, j in enumerate((2 * pj, 2 * pj + 1)):
                hs = slice(i * HD, (i + 1) * HD)
                o = r1[pj][C:, hs] + r2[pj][:C, hs]
                states[j] = states[j] * jnp.exp(glast[c, j]) + r2[pj][C:, hs]
                o = o * lax.rsqrt(jnp.mean(o * o, axis=-1, keepdims=True) + DN_EPS) * ng
                zz = z_ref[0, rows, j * HD:(j + 1) * HD]
                o_ref[0, rows, j * HD:(j + 1) * HD] = (o * (zz * jax.nn.sigmoid(zz))).astype(o_ref.dtype)

    for j in range(nh):
        s_ref[j] = states[j]


def _delta_rule(qk, v, z, bg, norm_g, B, S):
    HD = HEAD_DIM
    C = DN_CHUNK
    hps = DN_HEADS_PER_STEP
    T = DN_CHUNKS_PER_STEP * C
    q_w = DN_QK_HEADS * HD
    v_w = DN_V_HEADS * HD
    qkw = hps // 2 * HD
    vw = hps * HD
    n_hg = DN_V_HEADS // hps
    qk3 = qk.reshape(B, S, 2 * q_w)
    v3 = v.reshape(B, S, v_w)
    z3 = z.reshape(B, S, v_w)
    beta = jnp.transpose(bg[:, :DN_V_HEADS].reshape(B, S, n_hg, hps), (0, 2, 1, 3))
    gcum = jnp.transpose(bg[:, DN_V_HEADS:2 * DN_V_HEADS].reshape(B, S, n_hg, hps), (0, 2, 1, 3))
    bg4 = jnp.concatenate([beta, gcum], axis=-1)
    gt4 = jnp.transpose(gcum.reshape(B, n_hg, S // C, C, hps // 2, 2), (0, 1, 4, 2, 5, 3)).reshape(
        B, n_hg, hps // 2, 2 * S)
    return pl.pallas_call(
        _delta_kernel,
        grid=(B, n_hg, S // T),
        in_specs=[pl.BlockSpec((1, T, qkw), lambda b, h, t: (b, t, h)),
                  pl.BlockSpec((1, T, qkw), lambda b, h, t: (b, t, q_w // qkw + h)),
                  pl.BlockSpec((1, T, vw), lambda b, h, t: (b, t, h)),
                  pl.BlockSpec((1, T, vw), lambda b, h, t: (b, t, h)),
                  pl.BlockSpec((1, 1, T, 2 * hps), lambda b, h, t: (b, h, t, 0)),
                  pl.BlockSpec((1, 1, hps // 2, 2 * T), lambda b, h, t: (b, h, 0, t)),
                  pl.BlockSpec((1, HD), lambda b, h, t: (0, 0))],
        out_specs=pl.BlockSpec((1, T, vw), lambda b, h, t: (b, t, h)),
        out_shape=jax.ShapeDtypeStruct((B, S, v_w), BF16),
        scratch_shapes=[pltpu.VMEM((hps, HD, HD), F32)],
        compiler_params=_params("arbitrary", "arbitrary", "arbitrary"),
        name="delta_rule",
    )(qk3, qk3, v3, z3, bg4, gt4, norm_g.reshape(1, HD)).reshape(B * S, v_w)


def kernel(x, rel_bias, attn_w_in, attn_w_out, dn_w_in, dn_conv_w, dn_a_log, dn_dt_bias, dn_norm_g, dn_w_out,
           ln1_g, ln1_b, router_w, router_b, w_gate_up, b_gate_up, w_down, b_down, ln2_g, ln2_b):
    B, S, D = x.shape
    N = B * S
    E = router_w.shape[2]
    x2d = x.reshape(N, D)

    def vec(p):
        return p.reshape(1, -1)

    def router(i):
        return jnp.transpose(router_w[i]).astype(BF16), router_b[i].reshape(E, 1)

    def moe(i, post):
        return _moe_block(post, i, w_gate_up, b_gate_up, w_down, b_down, vec(ln2_g[i]), vec(ln2_b[i]))

    w_in = attn_w_in[0].astype(BF16)
    os_, lses = [], []
    for g, (window, dilation) in enumerate(DILATED_GROUPS):
        assert window // dilation == ATTN_BLOCK
        qkv = _attn_inproj(x2d, w_in, g, dilation, B, S)
        bias = _attn_bias(rel_bias[:, g * ATTN_HEADS:(g + 1) * ATTN_HEADS], dilation)
        o, lse = _dilated_group(qkv, bias, g)
        os_.append(o)
        lses.append(lse)
    rwt, rb = router(0)
    post = _attn_out(os_, lses, attn_w_out[0].astype(BF16), x2d, vec(ln1_g[0]), vec(ln1_b[0]), rwt, rb, S)
    x2, x2b = moe(0, post)

    conv_cols = dn_conv_w.shape[2]
    main_cols = conv_cols + DN_V_HEADS * HEAD_DIM
    w_in = dn_w_in[0].astype(BF16)
    q_w = DN_QK_HEADS * HEAD_DIM
    v_w = DN_V_HEADS * HEAD_DIM
    qk = _dn_inproj(x2b, w_in, dn_conv_w[0], S, "qk", 0, 2 * q_w)
    v = _dn_inproj(x2b, w_in, dn_conv_w[0], S, "v", 2 * q_w, v_w)
    z = _dn_inproj(x2b, w_in, dn_conv_w[0], S, "z", conv_cols, v_w)
    bg = _dn_gates(x2b, w_in[:, main_cols:], dn_a_log[0], dn_dt_bias[0])
    o = _delta_rule(qk, v, z, bg, dn_norm_g[0], B, S)
    rwt, rb = router(1)
    post = _dn_out(o, dn_w_out[0].astype(BF16), x2, vec(ln1_g[1]), vec(ln1_b[1]), rwt, rb)
    x3, _ = moe(1, post)
    return x3.reshape(B, S, D)
```

```python
import functools
import math

import jax
import jax.numpy as jnp
from jax import lax
from jax.experimental import pallas as pl
from jax.experimental.pallas import tpu as pltpu

F32 = jnp.float32
BF16 = jnp.bfloat16
I32 = jnp.int32
U32 = jnp.uint32

DEPTH = 2
DEEPNORM_ALPHA = (2.0 * DEPTH) ** 0.25
LN_EPS = 1e-5
HEAD_DIM = 128
ATTN_HEADS = 8
ATTN_WIDTH = ATTN_HEADS * HEAD_DIM
DILATED_GROUPS = ((128, 1), (512, 4), (2048, 16))
ATTN_BLOCK = 128
MAX_DISTANCE = 2048
DN_QK_HEADS = 16
DN_V_HEADS = 32
DN_CONV = 4
DN_CHUNK = 64
DN_EPS = 1e-6
TOP_K = 4
SWIGLU_LIMIT = 7.0
SWIGLU_ALPHA = 1.702

MOE_ROWS = 1024
MOE_HALF = 512
MOE_FF_CHUNK = 512
ROW_TILE = 256
DISPATCH_TILE = 512
VMEM_LIMIT = 58 * 1024 * 1024


def _params(*semantics):
    return pltpu.CompilerParams(dimension_semantics=semantics, vmem_limit_bytes=VMEM_LIMIT)


def _pack_bf16_pair(lo, hi):
    lo_bits = lax.bitcast_convert_type(lo.astype(BF16).astype(F32), U32)
    hi_bits = lax.bitcast_convert_type(hi.astype(BF16).astype(F32), U32)
    return (hi_bits & jnp.uint32(0xFFFF0000)) | (lo_bits >> 16)


def _unpack_bf16_pair(u):
    lo = lax.bitcast_convert_type(u << 16, F32)
    hi = lax.bitcast_convert_type(u & jnp.uint32(0xFFFF0000), F32)
    return lo, hi


def _attn_inproj_kernel(a_ref, w_ref, o_ref, acc_ref, *, d, tm):
    res = jnp.dot(a_ref[...].astype(BF16), w_ref[...], preferred_element_type=F32)
    if d == 1:
        o_ref[0, 0] = res.astype(o_ref.dtype)
        return
    lanes = acc_ref.shape[2]
    for cb in range(acc_ref.shape[0]):
        acc_ref[cb] = res[:, cb * lanes:(cb + 1) * lanes]
    for r in range(d):
        for cb in range(acc_ref.shape[0]):
            o_ref[0, r, :, cb * lanes:(cb + 1) * lanes] = (
                acc_ref[cb, pl.ds(r, tm // d, stride=d), :].astype(o_ref.dtype))


def _attn_inproj(x2d, w, g, d, B, S):
    K = x2d.shape[1]
    tm, tn = 512, 1536
    gw = 3 * ATTN_WIDTH
    nt = gw // tn
    mt = S // tm
    return pl.pallas_call(
        functools.partial(_attn_inproj_kernel, d=d, tm=tm),
        grid=(nt, B, mt),
        in_specs=[pl.BlockSpec((tm, K), lambda n, b, m: (b * mt + m, 0)),
                  pl.BlockSpec((K, tn), lambda n, b, m: (0, g * nt + n))],
        out_specs=pl.BlockSpec((1, d, tm // d, tn), lambda n, b, m: (b, 0, m, n)),
        out_shape=jax.ShapeDtypeStruct((B, d, S // d, gw), BF16),
        scratch_shapes=[pltpu.VMEM((tn // HEAD_DIM, tm, HEAD_DIM), F32)],
        compiler_params=_params("arbitrary", "arbitrary", "arbitrary"),
        name=f"attn_inproj_g{g}",
    )(x2d, w)


def _t5_bucket(dist, num_buckets):
    max_exact = num_buckets // 2
    d_f = jnp.maximum(dist, 1).astype(F32)
    large = max_exact + (jnp.log(d_f / max_exact) / math.log(MAX_DISTANCE / max_exact)
                         * (num_buckets - max_exact)).astype(I32)
    large = jnp.minimum(large, num_buckets - 1)
    return jnp.where(dist < max_exact, dist, large)


def _attn_bias(rel_bias_g, dilation):
    bq = ATTN_BLOCK
    a = jnp.arange(bq, dtype=I32)[:, None]
    c = jnp.arange(2 * bq, dtype=I32)[None, :]
    steps = a + bq - c
    valid = (steps >= 0) & (steps <= bq)
    bucket = _t5_bucket(jnp.maximum(steps, 0) * dilation, rel_bias_g.shape[0])
    bias = jnp.transpose(rel_bias_g[bucket], (2, 0, 1)).astype(F32)
    return jnp.where(valid[None], bias, -jnp.inf)


def _attn_kernel(q_ref, kp_ref, kc_ref, vp_ref, vc_ref, bias_ref, o_ref, lse_ref):
    bq = ATTN_BLOCK
    first = pl.program_id(2) == 0
    scale = HEAD_DIM ** -0.5
    nt = (((1,), (1,)), ((), ()))
    lse_ref[0, 0] = jnp.zeros(lse_ref.shape[2:], F32)
    heads = range(ATTN_HEADS)
    sls = [slice(h * HEAD_DIM, (h + 1) * HEAD_DIM) for h in heads]
    s_p = [lax.dot_general(q_ref[0, 0, :, sls[h]], kp_ref[0, 0, :, sls[h]], nt, preferred_element_type=F32)
           for h in heads]
    s_c = [lax.dot_general(q_ref[0, 0, :, sls[h]], kc_ref[0, 0, :, sls[h]], nt, preferred_element_type=F32)
           for h in heads]
    p_p, p_c = [], []
    for h in heads:
        sp = jnp.where(first, -jnp.inf, s_p[h] * scale + bias_ref[h, :, :bq])
        sc = s_c[h] * scale + bias_ref[h, :, bq:]
        m = jnp.maximum(jnp.max(sp, axis=-1, keepdims=True), jnp.max(sc, axis=-1, keepdims=True))
        e_p = jnp.exp(sp - m)
        e_c = jnp.exp(sc - m)
        l = jnp.sum(e_p, axis=-1, keepdims=True) + jnp.sum(e_c, axis=-1, keepdims=True)
        inv = 1.0 / l
        p_p.append((e_p * inv).astype(BF16))
        p_c.append((e_c * inv).astype(BF16))
        lse_ref[0, 0, :, h:h + 1] = m + jnp.log(l)
    for h in heads:
        o_ref[0, 0, :, sls[h]] = (jnp.dot(p_p[h], vp_ref[0, 0, :, sls[h]], preferred_element_type=F32)
                                  + jnp.dot(p_c[h], vc_ref[0, 0, :, sls[h]], preferred_element_type=F32))


def _dilated_group(qkv, bias, g):
    B, d, L, _ = qkv.shape
    nb = L // ATTN_BLOCK
    blk = (1, 1, ATTN_BLOCK, ATTN_WIDTH)
    lse_blk = (1, 1, ATTN_BLOCK, HEAD_DIM)

    def spec(kind, prev):
        if prev:
            return pl.BlockSpec(blk, lambda b, r, i: (b, r, jnp.maximum(i - 1, 0), kind))
        return pl.BlockSpec(blk, lambda b, r, i: (b, r, i, kind))

    return pl.pallas_call(
        _attn_kernel,
        grid=(B, d, nb),
        in_specs=[spec(0, False), spec(1, True), spec(1, False), spec(2, True), spec(2, False),
                  pl.BlockSpec(bias.shape, lambda b, r, i: (0, 0, 0))],
        out_specs=[pl.BlockSpec(blk, lambda b, r, i: (b, r, i, 0)),
                   pl.BlockSpec(lse_blk, lambda b, r, i: (b, r, i, 0))],
        out_shape=[jax.ShapeDtypeStruct((B, d, L, ATTN_WIDTH), F32),
                   jax.ShapeDtypeStruct((B, d, L, HEAD_DIM), F32)],
        compiler_params=_params("arbitrary", "arbitrary", "arbitrary"),
        name=f"dilated_attn_g{g}",
    )(qkv, qkv, qkv, qkv, qkv, bias)


def _layer_norm(y, g_ref, b_ref):
    mu = jnp.mean(y, axis=-1, keepdims=True)
    yc = y - mu
    var = jnp.mean(yc * yc, axis=-1, keepdims=True)
    return yc * lax.rsqrt(var + LN_EPS) * g_ref[...] + b_ref[...]


def _residual_ln_router(h, x_ref, g_ref, b_ref, rwt_ref, rb_ref,
                        x1_ref, x1p_ref, topi_ref, gate_ref, rank_ref, cnt_ref, carry_ref, first_step):
    xn = _layer_norm(DEEPNORM_ALPHA * x_ref[...] + h, g_ref, b_ref)
    x1_ref[...] = xn
    half = xn.shape[1] // 2
    x1p_ref[...] = _pack_bf16_pair(xn[:, :half], xn[:, half:])
    logits = lax.dot_general(rwt_ref[...], xn.astype(BF16), (((1,), (1,)), ((), ())),
                             preferred_element_type=F32) + rb_ref[...]
    n_exp, tm = logits.shape
    eidx = lax.broadcasted_iota(I32, logits.shape, 0)
    vals, hots = [], []
    for k in range(TOP_K):
        mx = jnp.max(logits, axis=0, keepdims=True)
        idx = jnp.min(jnp.where(logits == mx, eidx, n_exp), axis=0, keepdims=True)
        topi_ref[k:k + 1, :] = idx
        vals.append(mx)
        hit = eidx == idx
        hots.append(hit.astype(F32))
        logits = jnp.where(hit, -jnp.inf, logits)
    exps = [jnp.exp(v - vals[0]) for v in vals]
    denom = exps[0] + exps[1] + exps[2] + exps[3]
    for k in range(TOP_K):
        gate_ref[k:k + 1, :] = exps[k] / denom

    @pl.when(first_step)
    def _():
        carry_ref[...] = jnp.zeros_like(carry_ref)

    tri = (lax.broadcasted_iota(I32, (tm, tm), 0) <= lax.broadcasted_iota(I32, (tm, tm), 1)).astype(BF16)
    cum = jnp.dot(jnp.concatenate(hots, axis=0).astype(BF16), tri, preferred_element_type=F32)
    base = carry_ref[...]
    for k in range(TOP_K):
        cum_k = cum[k * n_exp:(k + 1) * n_exp]
        before = cum_k - hots[k] + base
        rank_ref[k:k + 1, :] = jnp.sum(hots[k] * before, axis=0, keepdims=True).astype(I32)
        base = base + cum_k[:, tm - 1:tm]
    carry_ref[...] = base
    cnt_ref[...] = jnp.broadcast_to(base, cnt_ref.shape)


def _attn_out_kernel(o0_ref, o1_ref, o2_ref, l0_ref, l1_ref, l2_ref, w_ref, x_ref, g_ref, b_ref, rwt_ref, rb_ref,
                     x1_ref, x1p_ref, topi_ref, gate_ref, rank_ref, cnt_ref,
                     o_scr, l_scr, merged_ref, carry_ref, *, tm):
    o_refs = (o0_ref, o1_ref, o2_ref)
    l_refs = (l0_ref, l1_ref, l2_ref)
    H = ATTN_HEADS
    for g, (_, d) in enumerate(DILATED_GROUPS):
        for r in range(d):
            for h in range(H):
                o_scr[g * H + h, pl.ds(r, tm // d, stride=d), :] = o_refs[g][0, r, :, h * HEAD_DIM:(h + 1) * HEAD_DIM]
            l_scr[g, pl.ds(r, tm // d, stride=d), :] = l_refs[g][0, r]
    l0, l1, l2 = l_scr[0], l_scr[1], l_scr[2]
    m = jnp.maximum(jnp.maximum(l0, l1), l2)
    e = [jnp.exp(l0 - m), jnp.exp(l1 - m), jnp.exp(l2 - m)]
    den = e[0] + e[1] + e[2]
    wts = [ei / den for ei in e]
    for h in range(H):
        o = wts[0][:, h:h + 1] * o_scr[h]
        o = o + wts[1][:, h:h + 1] * o_scr[H + h]
        o = o + wts[2][:, h:h + 1] * o_scr[2 * H + h]
        merged_ref[:, h * HEAD_DIM:(h + 1) * HEAD_DIM] = o.astype(BF16)
    h_out = jnp.dot(merged_ref[...], w_ref[...], preferred_element_type=F32)
    _residual_ln_router(h_out, x_ref, g_ref, b_ref, rwt_ref, rb_ref, x1_ref, x1p_ref, topi_ref, gate_ref,
                        rank_ref, cnt_ref, carry_ref, pl.program_id(0) == 0)


def _dn_out_kernel(o_ref, w_ref, x_ref, g_ref, b_ref, rwt_ref, rb_ref,
                   x1_ref, x1p_ref, topi_ref, gate_ref, rank_ref, cnt_ref, carry_ref):
    h = jnp.dot(o_ref[...], w_ref[...], preferred_element_type=F32)
    _residual_ln_router(h, x_ref, g_ref, b_ref, rwt_ref, rb_ref, x1_ref, x1p_ref, topi_ref,
                        gate_ref, rank_ref, cnt_ref, carry_ref, pl.program_id(0) == 0)


def _post_specs(tm, D, E, idx):
    row = lambda *a: (idx(*a), 0)
    const = lambda *a: (0, 0)
    col = lambda *a: (0, idx(*a))
    ins = [pl.BlockSpec((tm, D), row),
           pl.BlockSpec((1, D), const), pl.BlockSpec((1, D), const),
           pl.BlockSpec((E, D), const), pl.BlockSpec((E, 1), const)]
    outs = [pl.BlockSpec((tm, D), row), pl.BlockSpec((tm, D // 2), row),
            pl.BlockSpec((TOP_K, tm), col), pl.BlockSpec((TOP_K, tm), col), pl.BlockSpec((TOP_K, tm), col),
            pl.BlockSpec((E, 128), const)]
    return ins, outs


def _post_shapes(N, D, E):
    return [jax.ShapeDtypeStruct((N, D), F32), jax.ShapeDtypeStruct((N, D // 2), U32),
            jax.ShapeDtypeStruct((TOP_K, N), I32), jax.ShapeDtypeStruct((TOP_K, N), F32),
            jax.ShapeDtypeStruct((TOP_K, N), I32), jax.ShapeDtypeStruct((E, 128), F32)]


def _attn_out(os_, lses, w_out, x2d, ln_g, ln_b, rwt, rb, S):
    N, D = x2d.shape
    E = rwt.shape[0]
    tm = 256
    mt = S // tm
    ins, outs = _post_specs(tm, D, E, lambda m: m)
    grp_specs = []
    for width in (ATTN_WIDTH, HEAD_DIM):
        for _, d in DILATED_GROUPS:
            grp_specs.append(pl.BlockSpec((1, d, tm // d, width), lambda m: (m // mt, 0, m % mt, 0)))
    return pl.pallas_call(
        functools.partial(_attn_out_kernel, tm=tm),
        grid=(N // tm,),
        in_specs=grp_specs + [pl.BlockSpec(w_out.shape, lambda m: (0, 0))] + ins,
        out_specs=outs,
        out_shape=_post_shapes(N, D, E),
        scratch_shapes=[pltpu.VMEM((len(DILATED_GROUPS) * ATTN_HEADS, tm, HEAD_DIM), F32),
                        pltpu.VMEM((len(DILATED_GROUPS), tm, HEAD_DIM), F32),
                        pltpu.VMEM((tm, ATTN_WIDTH), BF16),
                        pltpu.VMEM((E, 1), F32)],
        compiler_params=_params("arbitrary"),
        name="attn_out_ln_router",
    )(*os_, *lses, w_out, x2d, ln_g, ln_b, rwt, rb)


def _dn_out(o, w_out, x2d, ln_g, ln_b, rwt, rb):
    N, D = x2d.shape
    E = rwt.shape[0]
    Kin = o.shape[1]
    tm = 256
    ins, outs = _post_specs(tm, D, E, lambda m: m)
    return pl.pallas_call(
        _dn_out_kernel,
        grid=(N // tm,),
        in_specs=[pl.BlockSpec((tm, Kin), lambda m: (m, 0)),
                  pl.BlockSpec((Kin, D), lambda m: (0, 0), pipeline_mode=pl.Buffered(1))] + ins,
        out_specs=outs,
        out_shape=_post_shapes(N, D, E),
        scratch_shapes=[pltpu.VMEM((E, 1), F32)],
        compiler_params=_params("arbitrary"),
        name="dn_out_ln_router",
    )(o, w_out, x2d, ln_g, ln_b, rwt, rb)


def _slot_layout(topi, rank, counts, rows):
    K, N = topi.shape
    n_exp = counts.shape[0]
    padded = (counts + rows - 1) // rows * rows
    padded_end = jnp.cumsum(padded)
    padded_start = padded_end - padded
    sel = topi[None] == jnp.arange(n_exp, dtype=I32)[:, None, None]
    pos = jnp.sum(jnp.where(sel, padded_start[:, None, None], 0), axis=0) + rank
    n_blocks = -(-(K * N + n_exp * (rows - 1)) // rows)
    n_used = padded_end[-1] // rows
    blk = jnp.arange(n_blocks, dtype=I32)
    be = jnp.sum((blk[:, None] * rows >= padded_end[None, :]).astype(I32), axis=1)
    be = jnp.minimum(be, n_exp - 1)
    be_last = jnp.sum(jnp.where(blk == jnp.maximum(n_used - 1, 0), be, 0))
    used = blk < n_used
    be = jnp.where(used, be, be_last)
    start_b = jnp.sum(jnp.where(be[:, None] == jnp.arange(n_exp)[None, :], padded_start[None, :], 0), axis=1)
    cnt_e = jnp.sum(jnp.where(be[:, None] == jnp.arange(n_exp)[None, :], counts[None, :], 0), axis=1)
    valid = jnp.where(used, jnp.clip(cnt_e - (blk * rows - start_b), 0, rows), 0).astype(I32)
    n_slots = jnp.full((1,), n_blocks * rows, I32)
    pad_lo = jnp.concatenate([padded_start + counts, padded_end[-1:]]).astype(I32)
    pad_hi = jnp.concatenate([padded_end, n_slots]).astype(I32)
    return pos.astype(I32), be.astype(I32), valid, pad_lo, pad_hi, n_blocks


def _tile_major(pos, tm):
    K, N = pos.shape
    return jnp.transpose(pos.reshape(K, N // tm, tm), (1, 0, 2)).reshape(N // tm, K * tm)


def _dispatch_kernel(lo_ref, hi_ref, pos_hbm, x_ref, xs_hbm, idx_smem, zbuf, sem_idx, sem_rows, sem_zero, *, tm):
    i = pl.program_id(0)

    @pl.when(i == 0)
    def _():
        zbuf[...] = jnp.zeros_like(zbuf)
        one = zbuf.at[pl.ds(0, 1)]

        def per_expert(e, carry):
            lo = lo_ref[e]
            hi = hi_ref[e]
            lo_al = jnp.minimum((lo + 7) // 8 * 8, hi)

            def row_start(r, c):
                pltpu.make_async_copy(one, xs_hbm.at[pl.ds(r, 1)], sem_zero).start()
                return c

            def grp_start(q, c):
                pltpu.make_async_copy(zbuf, xs_hbm.at[pl.ds(pl.multiple_of(q * 8, 8), 8)], sem_zero).start()
                return c

            def row_wait(r, c):
                pltpu.make_async_copy(one, xs_hbm.at[pl.ds(0, 1)], sem_zero).wait()
                return c

            def grp_wait(q, c):
                pltpu.make_async_copy(zbuf, xs_hbm.at[pl.ds(0, 8)], sem_zero).wait()
                return c

            lax.fori_loop(lo, lo_al, row_start, 0)
            lax.fori_loop(lo_al // 8, hi // 8, grp_start, 0)
            lax.fori_loop(lo, lo_al, row_wait, 0)
            lax.fori_loop(lo_al // 8, hi // 8, grp_wait, 0)
            return carry

        lax.fori_loop(0, lo_ref.shape[0], per_expert, 0)

    cp = pltpu.make_async_copy(pos_hbm.at[i], idx_smem, sem_idx)
    cp.start()
    cp.wait()

    def issue(jj, carry):
        for u in range(8):
            j = jj * 8 + u
            for k in range(TOP_K):
                dst = idx_smem[k * tm + j]
                pltpu.make_async_copy(x_ref.at[pl.ds(j, 1)], xs_hbm.at[pl.ds(dst, 1)], sem_rows).start()
        return carry

    lax.fori_loop(0, tm // 8, issue, 0)
    for k in range(TOP_K):
        pltpu.make_async_copy(x_ref, xs_hbm.at[pl.ds(0, tm)], sem_rows).wait()


def _dispatch(x1p, pos_tiles, pad_lo, pad_hi, n_slots):
    N, W = x1p.shape
    tm = DISPATCH_TILE
    grid_spec = pltpu.PrefetchScalarGridSpec(
        num_scalar_prefetch=2,
        grid=(N // tm,),
        in_specs=[pl.BlockSpec(memory_space=pl.ANY),
                  pl.BlockSpec((tm, W), lambda i, lo, hi: (i, 0))],
        out_specs=pl.BlockSpec(memory_space=pl.ANY),
        scratch_shapes=[pltpu.SMEM((TOP_K * tm,), I32), pltpu.VMEM((8, W), U32),
                        pltpu.SemaphoreType.DMA, pltpu.SemaphoreType.DMA, pltpu.SemaphoreType.DMA],
    )
    return pl.pallas_call(
        functools.partial(_dispatch_kernel, tm=tm),
        grid_spec=grid_spec,
        out_shape=jax.ShapeDtypeStruct((n_slots, W), U32),
        compiler_params=_params("arbitrary"),
        name="moe_dispatch",
    )(pad_lo, pad_hi, pos_tiles, x1p)


def _ffn_kernel(be_ref, cnt_ref, xs_ref, wg_ref, wl_ref, bg_ref, bl_ref, wd_ref, bd_ref, o_ref, xb_ref, act_ref, *, nfc):
    b = pl.program_id(0)
    p = pl.program_id(1)
    cnt = cnt_ref[b]
    rows = xb_ref.shape[0]
    half = xs_ref.shape[1]
    fc = wg_ref.shape[3]
    n_half = rows // MOE_HALF
    up = p < nfc

    @pl.when(jnp.logical_and(cnt > 0, p == 0))
    def _():
        lo, hi = _unpack_bf16_pair(xs_ref[...])
        xb_ref[:, :half] = lo.astype(BF16)
        xb_ref[:, half:] = hi.astype(BF16)

    def up_phase(live):
        wg = wg_ref[0, 0].astype(BF16)
        wl = wl_ref[0, 0].astype(BF16)
        acts = []
        for h in range(live):
            x = xb_ref[h * MOE_HALF:(h + 1) * MOE_HALF, :]
            hg = jnp.dot(x, wg, preferred_element_type=F32) + bg_ref[0, 0]
            hl = jnp.dot(x, wl, preferred_element_type=F32) + bl_ref[0, 0]
            glu = jnp.minimum(hg, SWIGLU_LIMIT)
            lin = jnp.clip(hl, -SWIGLU_LIMIT, SWIGLU_LIMIT)
            acts.append((glu * jax.nn.sigmoid(SWIGLU_ALPHA * glu) * (lin + 1.0)).astype(BF16))
        for c in range(nfc):
            @pl.when(p == c)
            def _():
                for h in range(live):
                    act_ref[h * MOE_HALF:(h + 1) * MOE_HALF, c * fc:(c + 1) * fc] = acts[h]

    def down_phase(live):
        wd = wd_ref[0, 0].astype(BF16)
        for h in range(live):
            rs = slice(h * MOE_HALF, (h + 1) * MOE_HALF)
            o_ref[rs, :] = jnp.dot(act_ref[rs, :], wd, preferred_element_type=F32) + bd_ref[0, 0]
        for h in range(live, n_half):
            rs = slice(h * MOE_HALF, (h + 1) * MOE_HALF)
            o_ref[rs, :] = jnp.zeros((MOE_HALF, fc), F32)

    for live in range(1, n_half + 1):
        in_range = cnt > (live - 1) * MOE_HALF
        if live < n_half:
            in_range = jnp.logical_and(in_range, cnt <= live * MOE_HALF)

        @pl.when(jnp.logical_and(in_range, up))
        def _():
            up_phase(live)

        @pl.when(jnp.logical_and(in_range, jnp.logical_not(up)))
        def _():
            down_phase(live)

    @pl.when(jnp.logical_and(cnt == 0, jnp.logical_not(up)))
    def _():
        down_phase(0)


def _expert_ffn(xs, be, valid, layer, w_gate_up, b_gate_up, w_down, b_down):
    n_slots, half = xs.shape
    D = 2 * half
    _, E, _, F2 = w_gate_up.shape
    F = F2 // 2
    rows, fc = MOE_ROWS, MOE_FF_CHUNK
    assert F % fc == 0 and D % fc == 0
    nfc = F // fc
    ndc = D // fc
    n_blocks = n_slots // rows

    def up_chunk(b, p, cnt):
        return jnp.where(cnt[b] > 0, jnp.minimum(p, nfc - 1), nfc - 1)

    def down_chunk(b, p, cnt):
        return jnp.where(cnt[b] > 0, jnp.maximum(p - nfc, 0), ndc - 1)

    grid_spec = pltpu.PrefetchScalarGridSpec(
        num_scalar_prefetch=2,
        grid=(n_blocks, nfc + ndc),
        in_specs=[
            pl.BlockSpec((rows, half), lambda b, p, be, cnt: (b, 0)),
            pl.BlockSpec((1, 1, D, fc), lambda b, p, be, cnt: (layer, be[b], 0, up_chunk(b, p, cnt))),
            pl.BlockSpec((1, 1, D, fc), lambda b, p, be, cnt: (layer, be[b], 0, nfc + up_chunk(b, p, cnt))),
            pl.BlockSpec((1, 1, 1, fc), lambda b, p, be, cnt: (layer, be[b], 0, up_chunk(b, p, cnt))),
            pl.BlockSpec((1, 1, 1, fc), lambda b, p, be, cnt: (layer, be[b], 0, nfc + up_chunk(b, p, cnt))),
            pl.BlockSpec((1, 1, F, fc), lambda b, p, be, cnt: (layer, be[b], 0, down_chunk(b, p, cnt))),
            pl.BlockSpec((1, 1, 1, fc), lambda b, p, be, cnt: (layer, be[b], 0, down_chunk(b, p, cnt))),
        ],
        out_specs=pl.BlockSpec((rows, fc), lambda b, p, be, cnt: (b, jnp.maximum(p - nfc, 0))),
        scratch_shapes=[pltpu.VMEM((rows, D), BF16), pltpu.VMEM((rows, F), BF16)],
    )
    L = w_gate_up.shape[0]
    return pl.pallas_call(
        functools.partial(_ffn_kernel, nfc=nfc),
        grid_spec=grid_spec,
        out_shape=jax.ShapeDtypeStruct((n_slots, D), F32),
        compiler_params=_params("arbitrary", "arbitrary"),
        name="expert_ffn",
    )(be, valid, xs, w_gate_up, w_gate_up, b_gate_up.reshape(L, E, 1, F2), b_gate_up.reshape(L, E, 1, F2),
      w_down, b_down.reshape(L, E, 1, D))


def _combine_kernel(pos_hbm, ys_hbm, gate_ref, x_ref, g_ref, b_ref, x2_ref, x2b_ref,
                    idx_smem, buf, sem_idx, sem_rows, *, tm):
    i = pl.program_id(0)
    n_tiles = 2 * pl.num_programs(0)

    def idx_copy(tile, slot):
        return pltpu.make_async_copy(pos_hbm.at[tile], idx_smem.at[slot], sem_idx.at[slot])

    def gather(tile, slot):
        idx_copy(tile, slot).wait()

        def issue(jj, carry):
            for u in range(8):
                j = jj * 8 + u
                for k in range(TOP_K):
                    src = idx_smem[slot, k * tm + j]
                    pltpu.make_async_copy(ys_hbm.at[pl.ds(src, 1)], buf.at[slot, k, pl.ds(j, 1)],
                                          sem_rows.at[slot]).start()
            return carry

        lax.fori_loop(0, tm // 8, issue, 0)

        @pl.when(tile + 2 < n_tiles)
        def _():
            idx_copy(tile + 2, slot).start()

    def reduce(slot):
        rs = slice(slot * tm, (slot + 1) * tm)
        for k in range(TOP_K):
            pltpu.make_async_copy(ys_hbm.at[pl.ds(0, tm)], buf.at[slot, k], sem_rows.at[slot]).wait()
        gate = gate_ref[rs, :]
        f = buf[slot, 0] * gate[:, 0:1]
        for k in range(1, TOP_K):
            f = f + buf[slot, k] * gate[:, k:k + 1]
        xn = _layer_norm(DEEPNORM_ALPHA * x_ref[rs, :] + f, g_ref, b_ref)
        x2_ref[rs, :] = xn
        x2b_ref[rs, :] = xn.astype(BF16)

    @pl.when(i == 0)
    def _():
        idx_copy(0, 0).start()
        idx_copy(1, 1).start()
        gather(0, 0)

    gather(2 * i + 1, 1)
    reduce(0)

    @pl.when(2 * i + 2 < n_tiles)
    def _():
        gather(2 * i + 2, 0)

    reduce(1)


def _combine_ln(ys, pos_tiles, gates_t, x1, ln_g, ln_b):
    N, D = x1.shape
    tm = ROW_TILE
    row = pl.BlockSpec((2 * tm, D), lambda m: (m, 0))
    vec = pl.BlockSpec((1, D), lambda m: (0, 0))
    anyspec = pl.BlockSpec(memory_space=pl.ANY)
    return pl.pallas_call(
        functools.partial(_combine_kernel, tm=tm),
        grid=(N // (2 * tm),),
        in_specs=[anyspec, anyspec, pl.BlockSpec((2 * tm, TOP_K), lambda m: (m, 0)), row, vec, vec],
        out_specs=[row, row],
        out_shape=[jax.ShapeDtypeStruct((N, D), F32), jax.ShapeDtypeStruct((N, D), BF16)],
        scratch_shapes=[pltpu.SMEM((2, TOP_K * tm), I32), pltpu.VMEM((2, TOP_K, tm, D), F32),
                        pltpu.SemaphoreType.DMA((2,)), pltpu.SemaphoreType.DMA((2,))],
        compiler_params=_params("arbitrary"),
        name="moe_combine_ln",
    )(pos_tiles, ys, gates_t, x1, ln_g, ln_b)


def _moe_block(post, layer, w_gate_up, b_gate_up, w_down, b_down, ln_g, ln_b):
    x1, x1p, topi, gates, rank, cnt = post
    counts = cnt[:, 0].astype(I32)
    pos, be, valid, pad_lo, pad_hi, n_blocks = _slot_layout(topi, rank, counts, MOE_ROWS)
    xs = _dispatch(x1p, _tile_major(pos, DISPATCH_TILE), pad_lo, pad_hi, n_blocks * MOE_ROWS)
    ys = _expert_ffn(xs, be, valid, layer, w_gate_up, b_gate_up, w_down, b_down)
    return _combine_ln(ys, _tile_major(pos, ROW_TILE), jnp.transpose(gates), x1, ln_g, ln_b)


DN_COL_SUB = 256


def _dn_inproj_kernel(x_ref, w_ref, cw_ref, o_ref, pbuf, *, mode, tm, tn, seq_tiles, q_tiles):
    n = pl.program_id(0)
    m = pl.program_id(1)
    if mode != "z":
        @pl.when(m % seq_tiles == 0)
        def _():
            pbuf[0:8, :] = jnp.zeros((8, tn), F32)

    x = x_ref[...]
    sc = jnp.where(n < q_tiles, HEAD_DIM ** -0.5, 1.0).astype(F32)
    for cs in range(tn // DN_COL_SUB):
        cols = slice(cs * DN_COL_SUB, (cs + 1) * DN_COL_SUB)
        p = jnp.dot(x, w_ref[:, cols], preferred_element_type=F32)
        if mode == "z":
            o_ref[:, cols] = p
            continue
        pbuf[8:tm + 8, cols] = p
        cw = cw_ref[:, cols]
        y = cw[0:1] * pbuf[5:tm + 5, cols]
        y = y + cw[1:2] * pbuf[6:tm + 6, cols]
        y = y + cw[2:3] * pbuf[7:tm + 7, cols]
        y = y + cw[3:4] * p
        pbuf[0:8, cols] = pbuf[tm:tm + 8, cols]
        y = y * jax.nn.sigmoid(y)
        if mode == "v":
            o_ref[:, cols] = y
            continue
        for hh in range(DN_COL_SUB // HEAD_DIM):
            sl = slice(hh * HEAD_DIM, (hh + 1) * HEAD_DIM)
            ys = y[:, sl]
            ss = jnp.sum(ys * ys, axis=-1, keepdims=True)
            o_ref[:, cs * DN_COL_SUB + hh * HEAD_DIM:cs * DN_COL_SUB + (hh + 1) * HEAD_DIM] = (
                ys * lax.rsqrt(ss + DN_EPS) * sc)


def _dn_inproj(xb, w, conv_w, seq_len, mode, col0, width):
    N, K = xb.shape
    tm, tn = 512, 1024
    t0 = col0 // tn
    q_w = DN_QK_HEADS * HEAD_DIM
    conv_tiles = conv_w.shape[1] // tn
    kern = functools.partial(_dn_inproj_kernel, mode=mode, tm=tm, tn=tn, seq_tiles=seq_len // tm,
                             q_tiles=q_w // tn if mode == "qk" else 0)
    return pl.pallas_call(
        kern,
        grid=(width // tn, N // tm),
        in_specs=[pl.BlockSpec((tm, K), lambda n, m: (m, 0)),
                  pl.BlockSpec((K, tn), lambda n, m: (0, t0 + n)),
                  pl.BlockSpec((DN_CONV, tn), lambda n, m: (0, jnp.minimum(t0 + n, conv_tiles - 1)))],
        out_specs=pl.BlockSpec((tm, tn), lambda n, m: (m, n)),
        out_shape=jax.ShapeDtypeStruct((N, width), F32),
        scratch_shapes=[pltpu.VMEM((tm + 8, tn), F32)],
        compiler_params=_params("arbitrary", "arbitrary"),
        name=f"dn_inproj_{mode}",
    )(xb, w, conv_w)


def _dn_gates_kernel(x_ref, w_ref, alog_ref, dtb_ref, o_ref):
    r = jnp.dot(x_ref[...], w_ref[...], preferred_element_type=F32)
    lane = lax.broadcasted_iota(I32, r.shape, 1)
    row = lax.broadcasted_iota(I32, r.shape, 0)
    beta = jax.nn.sigmoid(r)
    z = r + dtb_ref[...]
    softplus = jnp.maximum(z, 0.0) + jnp.log1p(jnp.exp(-jnp.abs(z)))
    g = -jnp.exp(alog_ref[...]) * softplus
    in_chunk = row % DN_CHUNK
    s = 1
    while s < DN_CHUNK:
        g = g + jnp.where(in_chunk >= s, pltpu.roll(g, s, axis=0), 0.0)
        s *= 2
    o_ref[...] = jnp.where(lane < DN_V_HEADS, beta, g)


def _dn_gates(xb, w_ba, a_log, dt_bias):
    N, K = xb.shape
    H = DN_V_HEADS
    tm = 512
    lanes = 128
    w_pad = jnp.zeros((K, lanes), BF16).at[:, :2 * H].set(w_ba)
    alog_pad = jnp.zeros((1, lanes), F32).at[0, H:2 * H].set(a_log)
    dtb_pad = jnp.zeros((1, lanes), F32).at[0, H:2 * H].set(dt_bias)
    vec = pl.BlockSpec((1, lanes), lambda m: (0, 0))
    return pl.pallas_call(
        _dn_gates_kernel,
        grid=(N // tm,),
        in_specs=[pl.BlockSpec((tm, K), lambda m: (m, 0)), pl.BlockSpec((K, lanes), lambda m: (0, 0)), vec, vec],
        out_specs=pl.BlockSpec((tm, lanes), lambda m: (m, 0)),
        out_shape=jax.ShapeDtypeStruct((N, lanes), F32),
        compiler_params=_params("arbitrary"),
        name="dn_gates",
    )(xb, w_pad, alog_pad, dtb_pad)


DN_HEADS_PER_STEP = 8
DN_CHUNKS_PER_STEP = 4


def _delta_kernel(q_ref, k_ref, v_ref, z_ref, bg_ref, gtp_ref, ng_ref, o_ref, s_ref):
    C = DN_CHUNK
    HD = HEAD_DIM
    nh, nc = DN_HEADS_PER_STEP, DN_CHUNKS_PER_STEP
    npair = nh // 2
    ngrp = nh // 4
    nt = (((1,), (1,)), ((), ()))

    @pl.when(pl.program_id(2) == 0)
    def _():
        s_ref[...] = jnp.zeros_like(s_ref)

    ri2 = lax.broadcasted_iota(I32, (C, 2 * C), 0)
    li2 = lax.broadcasted_iota(I32, (C, 2 * C), 1)
    left = li2 < C
    ci2 = li2 % C
    incl2 = ri2 >= ci2
    strict2 = ri2 > ci2
    left_row = lax.broadcasted_iota(I32, (1, 2 * C), 1) < C
    ri4 = lax.broadcasted_iota(I32, (C, 4 * C), 0)
    li4 = lax.broadcasted_iota(I32, (C, 4 * C), 1)
    eye4 = (ri4 == li4 % C).astype(F32)
    blk4 = (lax.broadcasted_iota(I32, (4 * C, 4 * C), 0) // C) == (lax.broadcasted_iota(I32, (4 * C, 4 * C), 1) // C)
    zero_hd = jnp.zeros((C, HD), BF16)
    zero_s = jnp.zeros((HD, HD), BF16)

    def block_diag4(x):
        return jnp.where(blk4, jnp.concatenate([x, x, x, x], axis=0), 0.0).astype(BF16)

    b2, lhs2, rhs, qg, glast, bcat, qcat = {}, {}, {}, {}, {}, {}, {}
    for c in range(nc):
        rows = slice(c * C, (c + 1) * C)
        for pj in range(npair):
            csl = slice(pj * HD, (pj + 1) * HD)
            q = q_ref[0, rows, csl]
            k = k_ref[0, rows, csl]
            kk = jnp.concatenate([k, k], axis=0)
            m1 = lax.dot_general(jnp.concatenate([k, q], axis=0).astype(BF16), kk.astype(BF16), nt,
                                 preferred_element_type=F32)
            ja, jb = 2 * pj, 2 * pj + 1
            bcol = [bg_ref[0, 0, rows, j:j + 1] for j in (ja, jb)]
            gcol = [bg_ref[0, 0, rows, nh + j:nh + j + 1] for j in (ja, jb)]
            grow2 = gtp_ref[0, 0, pj:pj + 1, c * 2 * C:(c + 1) * 2 * C]
            gcol2 = jnp.where(left, gcol[0], gcol[1])
            bcol2 = jnp.where(left, bcol[0], bcol[1])
            decay2 = jnp.exp(jnp.where(incl2, gcol2 - grow2, -jnp.inf))
            b2[c, pj] = jnp.where(strict2, -(m1[:C] * bcol2 * decay2), 0.0)
            qk2 = (m1[C:] * decay2).astype(BF16)
            gl = [g[C - 1:C, :] for g in gcol]
            glast2 = jnp.where(left_row, gl[0], gl[1])
            kdec_t2 = (kk.T * jnp.exp(glast2 - grow2)).astype(BF16)
            lhs2[c, pj] = jnp.concatenate([qk2, kdec_t2], axis=0)
            for i, j in enumerate((ja, jb)):
                eg = jnp.exp(gcol[i])
                v = v_ref[0, rows, j * HD:(j + 1) * HD]
                kb = k * bcol[i]
                rhs[c, j] = jnp.concatenate([v * bcol[i], kb * eg], axis=1).astype(BF16)
                qg[c, j] = (q * eg).astype(BF16)
                glast[c, j] = gl[i]

    groups = [(c, g) for c in range(nc) for g in range(ngrp)]
    for c, g in groups:
        bcat[c, g] = jnp.concatenate([b2[c, 2 * g], b2[c, 2 * g + 1]], axis=1)
        qcat[c, g] = eye4 + bcat[c, g]
    for it in groups:
        bcat[it] = jnp.dot(bcat[it].astype(BF16), block_diag4(bcat[it]), preferred_element_type=F32)
    for _ in range(4):
        for it in groups:
            r = jnp.dot(jnp.concatenate([qcat[it], bcat[it]], axis=0).astype(BF16), block_diag4(bcat[it]),
                        preferred_element_type=F32)
            qcat[it] = qcat[it] + r[:C]
            bcat[it] = r[C:]
    uw = {}
    for c, g in groups:
        it = (c, g)
        t_inv = qcat[it] + jnp.dot(qcat[it].astype(BF16), block_diag4(bcat[it]), preferred_element_type=F32)
        rstack = jnp.concatenate([rhs[c, 4 * g + i] for i in range(4)], axis=0)
        r = jnp.dot(block_diag4(t_inv), rstack, preferred_element_type=F32)
        for i in range(4):
            uw[c, 4 * g + i] = r[i * C:(i + 1) * C]

    states = [s_ref[j] for j in range(nh)]
    ng = ng_ref[...]
    for c in range(nc):
        rows = slice(c * C, (c + 1) * C)
        r1 = {}
        for pj in range(npair):
            ja, jb = 2 * pj, 2 * pj + 1
            lhs = jnp.concatenate(
                [jnp.concatenate([uw[c, j][:, HD:].astype(BF16), qg[c, j]], axis=0) for j in (ja, jb)], axis=1)
            sa, sb = states[ja].astype(BF16), states[jb].astype(BF16)
            sbd = jnp.concatenate([jnp.concatenate([sa, zero_s], axis=1),
                                   jnp.concatenate([zero_s, sb], axis=1)], axis=0)
            r1[pj] = jnp.dot(lhs, sbd, preferred_element_type=F32)
        r2 = {}
        for pj in range(npair):
            ja, jb = 2 * pj, 2 * pj + 1
            va = (uw[c, ja][:, :HD] - r1[pj][:C, :HD]).astype(BF16)
            vb = (uw[c, jb][:, :HD] - r1[pj][:C, HD:]).astype(BF16)
            vbd = jnp.concatenate([jnp.concatenate([va, zero_hd], axis=1),
                                   jnp.concatenate([zero_hd, vb], axis=1)], axis=0)
            r2[pj] = jnp.dot(lhs2[c, pj], vbd, preferred_element_type=F32)
        for pj in range(npair):
            for i, j in enumerate((2 * pj, 2 * pj + 1)):
                hs = slice(i * HD, (i + 1) * HD)
                o = r1[pj][C:, hs] + r2[pj][:C, hs]
                states[j] = states[j] * jnp.exp(glast[c, j]) + r2[pj][C:, hs]
                o = o * lax.rsqrt(jnp.mean(o * o, axis=-1, keepdims=True) + DN_EPS) * ng
                zz = z_ref[0, rows, j * HD:(j + 1) * HD]
                o_ref[0, rows, j * HD:(j + 1) * HD] = (o * (zz * jax.nn.sigmoid(zz))).astype(o_ref.dtype)

    for j in range(nh):
        s_ref[j] = states[j]


def _delta_rule(qk, v, z, bg, norm_g, B, S):
    HD = HEAD_DIM
    C = DN_CHUNK
    hps = DN_HEADS_PER_STEP
    T = DN_CHUNKS_PER_STEP * C
    q_w = DN_QK_HEADS * HD
    v_w = DN_V_HEADS * HD
    qkw = hps // 2 * HD
    vw = hps * HD
    n_hg = DN_V_HEADS // hps
    qk3 = qk.reshape(B, S, 2 * q_w)
    v3 = v.reshape(B, S, v_w)
    z3 = z.reshape(B, S, v_w)
    beta = jnp.transpose(bg[:, :DN_V_HEADS].reshape(B, S, n_hg, hps), (0, 2, 1, 3))
    gcum = jnp.transpose(bg[:, DN_V_HEADS:2 * DN_V_HEADS].reshape(B, S, n_hg, hps), (0, 2, 1, 3))
    bg4 = jnp.concatenate([beta, gcum], axis=-1)
    gt4 = jnp.transpose(gcum.reshape(B, n_hg, S // C, C, hps // 2, 2), (0, 1, 4, 2, 5, 3)).reshape(
        B, n_hg, hps // 2, 2 * S)
    return pl.pallas_call(
        _delta_kernel,
        grid=(B, n_hg, S // T),
        in_specs=[pl.BlockSpec((1, T, qkw), lambda b, h, t: (b, t, h)),
                  pl.BlockSpec((1, T, qkw), lambda b, h, t: (b, t, q_w // qkw + h)),
                  pl.BlockSpec((1, T, vw), lambda b, h, t: (b, t, h)),
                  pl.BlockSpec((1, T, vw), lambda b, h, t: (b, t, h)),
                  pl.BlockSpec((1, 1, T, 2 * hps), lambda b, h, t: (b, h, t, 0)),
                  pl.BlockSpec((1, 1, hps // 2, 2 * T), lambda b, h, t: (b, h, 0, t)),
                  pl.BlockSpec((1, HD), lambda b, h, t: (0, 0))],
        out_specs=pl.BlockSpec((1, T, vw), lambda b, h, t: (b, t, h)),
        out_shape=jax.ShapeDtypeStruct((B, S, v_w), BF16),
        scratch_shapes=[pltpu.VMEM((hps, HD, HD), F32)],
        compiler_params=_params("arbitrary", "arbitrary", "arbitrary"),
        name="delta_rule",
    )(qk3, qk3, v3, z3, bg4, gt4, norm_g.reshape(1, HD)).reshape(B * S, v_w)


def kernel(x, rel_bias, attn_w_in, attn_w_out, dn_w_in, dn_conv_w, dn_a_log, dn_dt_bias, dn_norm_g, dn_w_out,
           ln1_g, ln1_b, router_w, router_b, w_gate_up, b_gate_up, w_down, b_down, ln2_g, ln2_b):
    B, S, D = x.shape
    N = B * S
    E = router_w.shape[2]
    x2d = x.reshape(N, D)

    def vec(p):
        return p.reshape(1, -1)

    def router(i):
        return jnp.transpose(router_w[i]).astype(BF16), router_b[i].reshape(E, 1)

    def moe(i, post):
        return _moe_block(post, i, w_gate_up, b_gate_up, w_down, b_down, vec(ln2_g[i]), vec(ln2_b[i]))

    w_in = attn_w_in[0].astype(BF16)
    os_, lses = [], []
    for g, (window, dilation) in enumerate(DILATED_GROUPS):
        assert window // dilation == ATTN_BLOCK
        qkv = _attn_inproj(x2d, w_in, g, dilation, B, S)
        bias = _attn_bias(rel_bias[:, g * ATTN_HEADS:(g + 1) * ATTN_HEADS], dilation)
        o, lse = _dilated_group(qkv, bias, g)
        os_.append(o)
        lses.append(lse)
    rwt, rb = router(0)
    post = _attn_out(os_, lses, attn_w_out[0].astype(BF16), x2d, vec(ln1_g[0]), vec(ln1_b[0]), rwt, rb, S)
    x2, x2b = moe(0, post)

    conv_cols = dn_conv_w.shape[2]
    main_cols = conv_cols + DN_V_HEADS * HEAD_DIM
    w_in = dn_w_in[0].astype(BF16)
    q_w = DN_QK_HEADS * HEAD_DIM
    v_w = DN_V_HEADS * HEAD_DIM
    qk = _dn_inproj(x2b, w_in, dn_conv_w[0], S, "qk", 0, 2 * q_w)
    v = _dn_inproj(x2b, w_in, dn_conv_w[0], S, "v", 2 * q_w, v_w)
    z = _dn_inproj(x2b, w_in, dn_conv_w[0], S, "z", conv_cols, v_w)
    bg = _dn_gates(x2b, w_in[:, main_cols:], dn_a_log[0], dn_dt_bias[0])
    o = _delta_rule(qk, v, z, bg, dn_norm_g[0], B, S)
    rwt, rb = router(1)
    post = _dn_out(o, dn_w_out[0].astype(BF16), x2, vec(ln1_g[1]), vec(ln1_b[1]), rwt, rb)
    x3, _ = moe(1, post)
    return x3.reshape(B, S, D)
```

```python
import functools
import math

import jax
import jax.numpy as jnp
from jax import lax
from jax.experimental import pallas as pl
from jax.experimental.pallas import tpu as pltpu

F32 = jnp.float32
BF16 = jnp.bfloat16
I32 = jnp.int32
U32 = jnp.uint32

DEPTH = 2
DEEPNORM_ALPHA = (2.0 * DEPTH) ** 0.25
LN_EPS = 1e-5
HEAD_DIM = 128
ATTN_HEADS = 8
ATTN_WIDTH = ATTN_HEADS * HEAD_DIM
DILATED_GROUPS = ((128, 1), (512, 4), (2048, 16))
ATTN_BLOCK = 128
MAX_DISTANCE = 2048
DN_QK_HEADS = 16
DN_V_HEADS = 32
DN_CONV = 4
DN_CHUNK = 64
DN_EPS = 1e-6
TOP_K = 4
SWIGLU_LIMIT = 7.0
SWIGLU_ALPHA = 1.702

MOE_ROWS = 1024
MOE_HALF = 512
MOE_FF_CHUNK = 512
MOE_W_SPLIT = 2
ROW_TILE = 256
DISPATCH_TILE = 512
VMEM_LIMIT = 58 * 1024 * 1024


def _params(*semantics):
    return pltpu.CompilerParams(dimension_semantics=semantics, vmem_limit_bytes=VMEM_LIMIT)


def _pack_bf16_pair(lo, hi):
    lo_bits = lax.bitcast_convert_type(lo.astype(BF16).astype(F32), U32)
    hi_bits = lax.bitcast_convert_type(hi.astype(BF16).astype(F32), U32)
    return (hi_bits & jnp.uint32(0xFFFF0000)) | (lo_bits >> 16)


def _unpack_bf16_pair(u):
    lo = lax.bitcast_convert_type(u << 16, F32)
    hi = lax.bitcast_convert_type(u & jnp.uint32(0xFFFF0000), F32)
    return lo, hi


def _attn_inproj_kernel(*refs, d, tm, nx):
    a_refs = refs[:nx]
    w_ref, o_ref, xb_ref, acc_ref = refs[nx:]
    for s, a_ref in enumerate(a_refs):
        xb_ref[s * (tm // nx):(s + 1) * (tm // nx), :] = a_ref[...].astype(BF16)
    res = jnp.dot(xb_ref[...], w_ref[...], preferred_element_type=F32)
    if d == 1:
        o_ref[0, 0] = res.astype(o_ref.dtype)
        return
    lanes = acc_ref.shape[2]
    for cb in range(acc_ref.shape[0]):
        acc_ref[cb] = res[:, cb * lanes:(cb + 1) * lanes]
    for r in range(d):
        for cb in range(acc_ref.shape[0]):
            o_ref[0, r, :, cb * lanes:(cb + 1) * lanes] = (
                acc_ref[cb, pl.ds(r, tm // d, stride=d), :].astype(o_ref.dtype))


def _attn_inproj(x2d, w, g, d, B, S):
    K = x2d.shape[1]
    tm, tn = 512, 1536
    gw = 3 * ATTN_WIDTH
    nt = gw // tn
    mt = S // tm
    nx = 4

    def x_slab(s):
        return pl.BlockSpec((tm // nx, K), lambda n, b, m: ((b * mt + m) * nx + s, 0))

    return pl.pallas_call(
        functools.partial(_attn_inproj_kernel, d=d, tm=tm, nx=nx),
        grid=(nt, B, mt),
        in_specs=[x_slab(s) for s in range(nx)] + [pl.BlockSpec((K, tn), lambda n, b, m: (0, g * nt + n))],
        out_specs=pl.BlockSpec((1, d, tm // d, tn), lambda n, b, m: (b, 0, m, n)),
        out_shape=jax.ShapeDtypeStruct((B, d, S // d, gw), BF16),
        scratch_shapes=[pltpu.VMEM((tm, K), BF16), pltpu.VMEM((tn // HEAD_DIM, tm, HEAD_DIM), F32)],
        compiler_params=_params("arbitrary", "arbitrary", "arbitrary"),
        name=f"attn_inproj_g{g}",
    )(*([x2d] * nx), w)


def _t5_bucket(dist, num_buckets):
    max_exact = num_buckets // 2
    d_f = jnp.maximum(dist, 1).astype(F32)
    large = max_exact + (jnp.log(d_f / max_exact) / math.log(MAX_DISTANCE / max_exact)
                         * (num_buckets - max_exact)).astype(I32)
    large = jnp.minimum(large, num_buckets - 1)
    return jnp.where(dist < max_exact, dist, large)


def _attn_bias(rel_bias_g, dilation):
    bq = ATTN_BLOCK
    a = jnp.arange(bq, dtype=I32)[:, None]
    c = jnp.arange(2 * bq, dtype=I32)[None, :]
    steps = a + bq - c
    valid = (steps >= 0) & (steps <= bq)
    bucket = _t5_bucket(jnp.maximum(steps, 0) * dilation, rel_bias_g.shape[0])
    bias = jnp.transpose(rel_bias_g[bucket], (2, 0, 1)).astype(F32)
    return jnp.where(valid[None], bias, -jnp.inf)


def _attn_kernel(q_ref, kp_ref, kc_ref, vp_ref, vc_ref, bias_ref, o_ref, lse_ref):
    bq = ATTN_BLOCK
    first = pl.program_id(2) == 0
    scale = HEAD_DIM ** -0.5
    nt = (((1,), (1,)), ((), ()))
    lse_ref[0, 0] = jnp.zeros(lse_ref.shape[2:], F32)
    heads = range(ATTN_HEADS)
    sls = [slice(h * HEAD_DIM, (h + 1) * HEAD_DIM) for h in heads]
    s_p = [lax.dot_general(q_ref[0, 0, :, sls[h]], kp_ref[0, 0, :, sls[h]], nt, preferred_element_type=F32)
           for h in heads]
    s_c = [lax.dot_general(q_ref[0, 0, :, sls[h]], kc_ref[0, 0, :, sls[h]], nt, preferred_element_type=F32)
           for h in heads]
    p_p, p_c = [], []
    for h in heads:
        sp = jnp.where(first, -jnp.inf, s_p[h] * scale + bias_ref[h, :, :bq])
        sc = s_c[h] * scale + bias_ref[h, :, bq:]
        m = jnp.maximum(jnp.max(sp, axis=-1, keepdims=True), jnp.max(sc, axis=-1, keepdims=True))
        e_p = jnp.exp(sp - m)
        e_c = jnp.exp(sc - m)
        l = jnp.sum(e_p, axis=-1, keepdims=True) + jnp.sum(e_c, axis=-1, keepdims=True)
        inv = 1.0 / l
        p_p.append((e_p * inv).astype(BF16))
        p_c.append((e_c * inv).astype(BF16))
        lse_ref[0, 0, :, h:h + 1] = m + jnp.log(l)
    for h in heads:
        o_ref[0, 0, :, sls[h]] = (jnp.dot(p_p[h], vp_ref[0, 0, :, sls[h]], preferred_element_type=F32)
                                  + jnp.dot(p_c[h], vc_ref[0, 0, :, sls[h]], preferred_element_type=F32))


def _dilated_group(qkv, bias, g):
    B, d, L, _ = qkv.shape
    nb = L // ATTN_BLOCK
    blk = (1, 1, ATTN_BLOCK, ATTN_WIDTH)
    lse_blk = (1, 1, ATTN_BLOCK, HEAD_DIM)

    def spec(kind, prev):
        if prev:
            return pl.BlockSpec(blk, lambda b, r, i: (b, r, jnp.maximum(i - 1, 0), kind))
        return pl.BlockSpec(blk, lambda b, r, i: (b, r, i, kind))

    return pl.pallas_call(
        _attn_kernel,
        grid=(B, d, nb),
        in_specs=[spec(0, False), spec(1, True), spec(1, False), spec(2, True), spec(2, False),
                  pl.BlockSpec(bias.shape, lambda b, r, i: (0, 0, 0))],
        out_specs=[pl.BlockSpec(blk, lambda b, r, i: (b, r, i, 0)),
                   pl.BlockSpec(lse_blk, lambda b, r, i: (b, r, i, 0))],
        out_shape=[jax.ShapeDtypeStruct((B, d, L, ATTN_WIDTH), F32),
                   jax.ShapeDtypeStruct((B, d, L, HEAD_DIM), F32)],
        compiler_params=_params("arbitrary", "arbitrary", "arbitrary"),
        name=f"dilated_attn_g{g}",
    )(qkv, qkv, qkv, qkv, qkv, bias)


def _layer_norm(y, g_ref, b_ref):
    mu = jnp.mean(y, axis=-1, keepdims=True)
    yc = y - mu
    var = jnp.mean(yc * yc, axis=-1, keepdims=True)
    return yc * lax.rsqrt(var + LN_EPS) * g_ref[...] + b_ref[...]


def _residual_ln_router(h, x_ref, g_ref, b_ref, rwt_ref, rb_ref,
                        x1_ref, x1p_ref, topi_ref, gate_ref, rank_ref, cnt_ref, carry_ref, first_step):
    xn = _layer_norm(DEEPNORM_ALPHA * x_ref[...] + h, g_ref, b_ref)
    x1_ref[...] = xn
    half = xn.shape[1] // 2
    x1p_ref[...] = _pack_bf16_pair(xn[:, :half], xn[:, half:])
    logits = lax.dot_general(rwt_ref[...], xn.astype(BF16), (((1,), (1,)), ((), ())),
                             preferred_element_type=F32) + rb_ref[...]
    n_exp, tm = logits.shape
    eidx = lax.broadcasted_iota(I32, logits.shape, 0)
    vals, hots = [], []
    for k in range(TOP_K):
        mx = jnp.max(logits, axis=0, keepdims=True)
        idx = jnp.min(jnp.where(logits == mx, eidx, n_exp), axis=0, keepdims=True)
        topi_ref[k:k + 1, :] = idx
        vals.append(mx)
        hit = eidx == idx
        hots.append(hit.astype(F32))
        logits = jnp.where(hit, -jnp.inf, logits)
    exps = [jnp.exp(v - vals[0]) for v in vals]
    denom = exps[0] + exps[1] + exps[2] + exps[3]
    for k in range(TOP_K):
        gate_ref[k:k + 1, :] = exps[k] / denom

    @pl.when(first_step)
    def _():
        carry_ref[...] = jnp.zeros_like(carry_ref)

    tri = (lax.broadcasted_iota(I32, (tm, tm), 0) <= lax.broadcasted_iota(I32, (tm, tm), 1)).astype(BF16)
    cum = jnp.dot(jnp.concatenate(hots, axis=0).astype(BF16), tri, preferred_element_type=F32)
    base = carry_ref[...]
    for k in range(TOP_K):
        cum_k = cum[k * n_exp:(k + 1) * n_exp]
        before = cum_k - hots[k] + base
        rank_ref[k:k + 1, :] = jnp.sum(hots[k] * before, axis=0, keepdims=True).astype(I32)
        base = base + cum_k[:, tm - 1:tm]
    carry_ref[...] = base
    cnt_ref[...] = jnp.broadcast_to(base, cnt_ref.shape)


def _attn_out_kernel(o0_ref, o1_ref, o2_ref, l0_ref, l1_ref, l2_ref, w_ref, x_ref, g_ref, b_ref, rwt_ref, rb_ref,
                     x1_ref, x1p_ref, topi_ref, gate_ref, rank_ref, cnt_ref,
                     o_scr, l_scr, merged_ref, carry_ref, *, tm):
    o_refs = (o0_ref, o1_ref, o2_ref)
    l_refs = (l0_ref, l1_ref, l2_ref)
    H = ATTN_HEADS
    for g, (_, d) in enumerate(DILATED_GROUPS):
        for r in range(d):
            for h in range(H):
                o_scr[g * H + h, pl.ds(r, tm // d, stride=d), :] = o_refs[g][0, r, :, h * HEAD_DIM:(h + 1) * HEAD_DIM]
            l_scr[g, pl.ds(r, tm // d, stride=d), :] = l_refs[g][0, r]
    l0, l1, l2 = l_scr[0], l_scr[1], l_scr[2]
    m = jnp.maximum(jnp.maximum(l0, l1), l2)
    e = [jnp.exp(l0 - m), jnp.exp(l1 - m), jnp.exp(l2 - m)]
    den = e[0] + e[1] + e[2]
    wts = [ei / den for ei in e]
    for h in range(H):
        o = wts[0][:, h:h + 1] * o_scr[h]
        o = o + wts[1][:, h:h + 1] * o_scr[H + h]
        o = o + wts[2][:, h:h + 1] * o_scr[2 * H + h]
        merged_ref[:, h * HEAD_DIM:(h + 1) * HEAD_DIM] = o.astype(BF16)
    h_out = jnp.dot(merged_ref[...], w_ref[...], preferred_element_type=F32)
    _residual_ln_router(h_out, x_ref, g_ref, b_ref, rwt_ref, rb_ref, x1_ref, x1p_ref, topi_ref, gate_ref,
                        rank_ref, cnt_ref, carry_ref, pl.program_id(0) == 0)


def _dn_out_kernel(o_ref, w_ref, x_ref, g_ref, b_ref, rwt_ref, rb_ref,
                   x1_ref, x1p_ref, topi_ref, gate_ref, rank_ref, cnt_ref, carry_ref):
    h = jnp.dot(o_ref[...], w_ref[...], preferred_element_type=F32)
    _residual_ln_router(h, x_ref, g_ref, b_ref, rwt_ref, rb_ref, x1_ref, x1p_ref, topi_ref,
                        gate_ref, rank_ref, cnt_ref, carry_ref, pl.program_id(0) == 0)


def _post_specs(tm, D, E, idx):
    row = lambda *a: (idx(*a), 0)
    const = lambda *a: (0, 0)
    col = lambda *a: (0, idx(*a))
    ins = [pl.BlockSpec((tm, D), row),
           pl.BlockSpec((1, D), const), pl.BlockSpec((1, D), const),
           pl.BlockSpec((E, D), const), pl.BlockSpec((E, 1), const)]
    outs = [pl.BlockSpec((tm, D), row), pl.BlockSpec((tm, D // 2), row),
            pl.BlockSpec((TOP_K, tm), col), pl.BlockSpec((TOP_K, tm), col), pl.BlockSpec((TOP_K, tm), col),
            pl.BlockSpec((E, 128), const)]
    return ins, outs


def _post_shapes(N, D, E):
    return [jax.ShapeDtypeStruct((N, D), F32), jax.ShapeDtypeStruct((N, D // 2), U32),
            jax.ShapeDtypeStruct((TOP_K, N), I32), jax.ShapeDtypeStruct((TOP_K, N), F32),
            jax.ShapeDtypeStruct((TOP_K, N), I32), jax.ShapeDtypeStruct((E, 128), F32)]


def _attn_out(os_, lses, w_out, x2d, ln_g, ln_b, rwt, rb, S):
    N, D = x2d.shape
    E = rwt.shape[0]
    tm = 256
    mt = S // tm
    ins, outs = _post_specs(tm, D, E, lambda m: m)
    grp_specs = []
    for width in (ATTN_WIDTH, HEAD_DIM):
        for _, d in DILATED_GROUPS:
            grp_specs.append(pl.BlockSpec((1, d, tm // d, width), lambda m: (m // mt, 0, m % mt, 0)))
    return pl.pallas_call(
        functools.partial(_attn_out_kernel, tm=tm),
        grid=(N // tm,),
        in_specs=grp_specs + [pl.BlockSpec(w_out.shape, lambda m: (0, 0))] + ins,
        out_specs=outs,
        out_shape=_post_shapes(N, D, E),
        scratch_shapes=[pltpu.VMEM((len(DILATED_GROUPS) * ATTN_HEADS, tm, HEAD_DIM), F32),
                        pltpu.VMEM((len(DILATED_GROUPS), tm, HEAD_DIM), F32),
                        pltpu.VMEM((tm, ATTN_WIDTH), BF16),
                        pltpu.VMEM((E, 1), F32)],
        compiler_params=_params("arbitrary"),
        name="attn_out_ln_router",
    )(*os_, *lses, w_out, x2d, ln_g, ln_b, rwt, rb)


def _dn_out(o, w_out, x2d, ln_g, ln_b, rwt, rb):
    N, D = x2d.shape
    E = rwt.shape[0]
    Kin = o.shape[1]
    tm = 256
    ins, outs = _post_specs(tm, D, E, lambda m: m)
    return pl.pallas_call(
        _dn_out_kernel,
        grid=(N // tm,),
        in_specs=[pl.BlockSpec((tm, Kin), lambda m: (m, 0)),
                  pl.BlockSpec((Kin, D), lambda m: (0, 0), pipeline_mode=pl.Buffered(1))] + ins,
        out_specs=outs,
        out_shape=_post_shapes(N, D, E),
        scratch_shapes=[pltpu.VMEM((E, 1), F32)],
        compiler_params=_params("arbitrary"),
        name="dn_out_ln_router",
    )(o, w_out, x2d, ln_g, ln_b, rwt, rb)


def _slot_layout(topi, rank, counts, rows):
    K, N = topi.shape
    n_exp = counts.shape[0]
    padded = (counts + rows - 1) // rows * rows
    padded_end = jnp.cumsum(padded)
    padded_start = padded_end - padded
    sel = topi[None] == jnp.arange(n_exp, dtype=I32)[:, None, None]
    pos = jnp.sum(jnp.where(sel, padded_start[:, None, None], 0), axis=0) + rank
    n_blocks = -(-(K * N + n_exp * (rows - 1)) // rows)
    n_used = padded_end[-1] // rows
    blk = jnp.arange(n_blocks, dtype=I32)
    be = jnp.sum((blk[:, None] * rows >= padded_end[None, :]).astype(I32), axis=1)
    be = jnp.minimum(be, n_exp - 1)
    be_last = jnp.sum(jnp.where(blk == jnp.maximum(n_used - 1, 0), be, 0))
    used = blk < n_used
    be = jnp.where(used, be, be_last)
    start_b = jnp.sum(jnp.where(be[:, None] == jnp.arange(n_exp)[None, :], padded_start[None, :], 0), axis=1)
    cnt_e = jnp.sum(jnp.where(be[:, None] == jnp.arange(n_exp)[None, :], counts[None, :], 0), axis=1)
    valid = jnp.where(used, jnp.clip(cnt_e - (blk * rows - start_b), 0, rows), 0).astype(I32)
    n_slots = jnp.full((1,), n_blocks * rows, I32)
    pad_lo = jnp.concatenate([padded_start + counts, padded_end[-1:]]).astype(I32)
    pad_hi = jnp.concatenate([padded_end, n_slots]).astype(I32)
    return pos.astype(I32), be.astype(I32), valid, pad_lo, pad_hi, n_blocks


def _tile_major(pos, tm):
    K, N = pos.shape
    return jnp.transpose(pos.reshape(K, N // tm, tm), (1, 0, 2)).reshape(N // tm, K * tm)


def _dispatch_kernel(lo_ref, hi_ref, pos_hbm, x_ref, xs_hbm, idx_smem, zbuf, sem_idx, sem_rows, sem_zero, *, tm):
    i = pl.program_id(0)

    @pl.when(i == 0)
    def _():
        zbuf[...] = jnp.zeros_like(zbuf)
        one = zbuf.at[pl.ds(0, 1)]

        def per_expert(e, carry):
            lo = lo_ref[e]
            hi = hi_ref[e]
            lo_al = jnp.minimum((lo + 7) // 8 * 8, hi)

            def row_start(r, c):
                pltpu.make_async_copy(one, xs_hbm.at[pl.ds(r, 1)], sem_zero).start()
                return c

            def grp_start(q, c):
                pltpu.make_async_copy(zbuf, xs_hbm.at[pl.ds(pl.multiple_of(q * 8, 8), 8)], sem_zero).start()
                return c

            def row_wait(r, c):
                pltpu.make_async_copy(one, xs_hbm.at[pl.ds(0, 1)], sem_zero).wait()
                return c

            def grp_wait(q, c):
                pltpu.make_async_copy(zbuf, xs_hbm.at[pl.ds(0, 8)], sem_zero).wait()
                return c

            lax.fori_loop(lo, lo_al, row_start, 0)
            lax.fori_loop(lo_al // 8, hi // 8, grp_start, 0)
            lax.fori_loop(lo, lo_al, row_wait, 0)
            lax.fori_loop(lo_al // 8, hi // 8, grp_wait, 0)
            return carry

        lax.fori_loop(0, lo_ref.shape[0], per_expert, 0)

    cp = pltpu.make_async_copy(pos_hbm.at[i], idx_smem, sem_idx)
    cp.start()
    cp.wait()

    def issue(jj, carry):
        for u in range(8):
            j = jj * 8 + u
            for k in range(TOP_K):
                dst = idx_smem[k * tm + j]
                pltpu.make_async_copy(x_ref.at[pl.ds(j, 1)], xs_hbm.at[pl.ds(dst, 1)], sem_rows).start()
        return carry

    lax.fori_loop(0, tm // 8, issue, 0)
    for k in range(TOP_K):
        pltpu.make_async_copy(x_ref, xs_hbm.at[pl.ds(0, tm)], sem_rows).wait()


def _dispatch(x1p, pos_tiles, pad_lo, pad_hi, n_slots):
    N, W = x1p.shape
    tm = DISPATCH_TILE
    grid_spec = pltpu.PrefetchScalarGridSpec(
        num_scalar_prefetch=2,
        grid=(N // tm,),
        in_specs=[pl.BlockSpec(memory_space=pl.ANY),
                  pl.BlockSpec((tm, W), lambda i, lo, hi: (i, 0))],
        out_specs=pl.BlockSpec(memory_space=pl.ANY),
        scratch_shapes=[pltpu.SMEM((TOP_K * tm,), I32), pltpu.VMEM((8, W), U32),
                        pltpu.SemaphoreType.DMA, pltpu.SemaphoreType.DMA, pltpu.SemaphoreType.DMA],
    )
    return pl.pallas_call(
        functools.partial(_dispatch_kernel, tm=tm),
        grid_spec=grid_spec,
        out_shape=jax.ShapeDtypeStruct((n_slots, W), U32),
        compiler_params=_params("arbitrary"),
        name="moe_dispatch",
    )(pad_lo, pad_hi, pos_tiles, x1p)


def _ffn_kernel(be_ref, cnt_ref, xs_ref, *refs, nfc):
    nu, nd = MOE_W_SPLIT, 2 * MOE_W_SPLIT
    wg_refs, wl_refs = refs[0:nu], refs[nu:2 * nu]
    bg_ref, bl_ref = refs[2 * nu], refs[2 * nu + 1]
    wd_refs = refs[2 * nu + 2:2 * nu + 2 + nd]
    bd_ref, o_ref, xb_ref, act_ref, wa_ref, wb_ref = refs[2 * nu + 2 + nd:]
    b = pl.program_id(0)
    p = pl.program_id(1)
    cnt = cnt_ref[b]
    rows = xb_ref.shape[0]
    half = xs_ref.shape[1]
    fc = o_ref.shape[1]
    n_half = rows // MOE_HALF
    up = p < nfc

    @pl.when(jnp.logical_and(cnt > 0, p == 0))
    def _():
        lo, hi = _unpack_bf16_pair(xs_ref[...])
        xb_ref[:, :half] = lo.astype(BF16)
        xb_ref[:, half:] = hi.astype(BF16)

    def cast_slabs(slab_refs, dst_ref):
        step = slab_refs[0].shape[2]
        for s, r in enumerate(slab_refs):
            dst_ref[s * step:(s + 1) * step, :] = r[0, 0].astype(BF16)
        return dst_ref[:len(slab_refs) * step, :]

    def up_phase(live):
        wg = cast_slabs(wg_refs, wa_ref)
        wl = cast_slabs(wl_refs, wb_ref)
        acts = []
        for h in range(live):
            x = xb_ref[h * MOE_HALF:(h + 1) * MOE_HALF, :]
            hg = jnp.dot(x, wg, preferred_element_type=F32) + bg_ref[0, 0]
            hl = jnp.dot(x, wl, preferred_element_type=F32) + bl_ref[0, 0]
            glu = jnp.minimum(hg, SWIGLU_LIMIT)
            lin = jnp.clip(hl, -SWIGLU_LIMIT, SWIGLU_LIMIT)
            acts.append((glu * jax.nn.sigmoid(SWIGLU_ALPHA * glu) * (lin + 1.0)).astype(BF16))
        for c in range(nfc):
            @pl.when(p == c)
            def _():
                for h in range(live):
                    act_ref[h * MOE_HALF:(h + 1) * MOE_HALF, c * fc:(c + 1) * fc] = acts[h]

    def down_phase(live):
        wd = cast_slabs(wd_refs, wa_ref) if live else None
        for h in range(live):
            rs = slice(h * MOE_HALF, (h + 1) * MOE_HALF)
            o_ref[rs, :] = jnp.dot(act_ref[rs, :], wd, preferred_element_type=F32) + bd_ref[0, 0]
        for h in range(live, n_half):
            rs = slice(h * MOE_HALF, (h + 1) * MOE_HALF)
            o_ref[rs, :] = jnp.zeros((MOE_HALF, fc), F32)

    for live in range(1, n_half + 1):
        in_range = cnt > (live - 1) * MOE_HALF
        if live < n_half:
            in_range = jnp.logical_and(in_range, cnt <= live * MOE_HALF)

        @pl.when(jnp.logical_and(in_range, up))
        def _():
            up_phase(live)

        @pl.when(jnp.logical_and(in_range, jnp.logical_not(up)))
        def _():
            down_phase(live)

    @pl.when(jnp.logical_and(cnt == 0, jnp.logical_not(up)))
    def _():
        down_phase(0)


def _expert_ffn(xs, be, valid, layer, w_gate_up, b_gate_up, w_down, b_down):
    n_slots, half = xs.shape
    D = 2 * half
    _, E, _, F2 = w_gate_up.shape
    F = F2 // 2
    rows, fc = MOE_ROWS, MOE_FF_CHUNK
    assert F % fc == 0 and D % fc == 0
    nfc = F // fc
    ndc = D // fc
    n_blocks = n_slots // rows

    def up_chunk(b, p, cnt):
        return jnp.where(cnt[b] > 0, jnp.minimum(p, nfc - 1), nfc - 1)

    def down_chunk(b, p, cnt):
        return jnp.where(cnt[b] > 0, jnp.maximum(p - nfc, 0), ndc - 1)

    nu, nd = MOE_W_SPLIT, 2 * MOE_W_SPLIT
    assert D % nu == 0 and F % nd == 0

    def slab(height, ns, s, col):
        return pl.BlockSpec((1, 1, height // ns, fc), lambda b, p, be, cnt: (layer, be[b], s, col(b, p, cnt)))

    def bias(col):
        return pl.BlockSpec((1, 1, 1, fc), lambda b, p, be, cnt: (layer, be[b], 0, col(b, p, cnt)))

    def lin_chunk(b, p, cnt):
        return nfc + up_chunk(b, p, cnt)

    grid_spec = pltpu.PrefetchScalarGridSpec(
        num_scalar_prefetch=2,
        grid=(n_blocks, nfc + ndc),
        in_specs=([pl.BlockSpec((rows, half), lambda b, p, be, cnt: (b, 0))]
                  + [slab(D, nu, s, up_chunk) for s in range(nu)] + [slab(D, nu, s, lin_chunk) for s in range(nu)]
                  + [bias(up_chunk), bias(lin_chunk)]
                  + [slab(F, nd, s, down_chunk) for s in range(nd)] + [bias(down_chunk)]),
        out_specs=pl.BlockSpec((rows, fc), lambda b, p, be, cnt: (b, jnp.maximum(p - nfc, 0))),
        scratch_shapes=[pltpu.VMEM((rows, D), BF16), pltpu.VMEM((rows, F), BF16),
                        pltpu.VMEM((max(D, F), fc), BF16), pltpu.VMEM((D, fc), BF16)],
    )
    L = w_gate_up.shape[0]
    bgu = b_gate_up.reshape(L, E, 1, F2)
    return pl.pallas_call(
        functools.partial(_ffn_kernel, nfc=nfc),
        grid_spec=grid_spec,
        out_shape=jax.ShapeDtypeStruct((n_slots, D), F32),
        compiler_params=_params("arbitrary", "arbitrary"),
        name="expert_ffn",
    )(be, valid, xs, *([w_gate_up] * (2 * nu)), bgu, bgu, *([w_down] * nd), b_down.reshape(L, E, 1, D))


def _combine_kernel(pos_hbm, ys_hbm, gate_ref, x_ref, g_ref, b_ref, x2_ref, x2b_ref,
                    idx_smem, buf, sem_idx, sem_rows, *, tm):
    i = pl.program_id(0)
    n_tiles = 2 * pl.num_programs(0)

    def idx_copy(tile, slot):
        return pltpu.make_async_copy(pos_hbm.at[tile], idx_smem.at[slot], sem_idx.at[slot])

    def gather(tile, slot):
        idx_copy(tile, slot).wait()

        def issue(jj, carry):
            for u in range(8):
                j = jj * 8 + u
                for k in range(TOP_K):
                    src = idx_smem[slot, k * tm + j]
                    pltpu.make_async_copy(ys_hbm.at[pl.ds(src, 1)], buf.at[slot, k, pl.ds(j, 1)],
                                          sem_rows.at[slot]).start()
            return carry

        lax.fori_loop(0, tm // 8, issue, 0)

        @pl.when(tile + 2 < n_tiles)
        def _():
            idx_copy(tile + 2, slot).start()

    def reduce(slot):
        rs = slice(slot * tm, (slot + 1) * tm)
        for k in range(TOP_K):
            pltpu.make_async_copy(ys_hbm.at[pl.ds(0, tm)], buf.at[slot, k], sem_rows.at[slot]).wait()
        gate = gate_ref[rs, :]
        f = buf[slot, 0] * gate[:, 0:1]
        for k in range(1, TOP_K):
            f = f + buf[slot, k] * gate[:, k:k + 1]
        xn = _layer_norm(DEEPNORM_ALPHA * x_ref[rs, :] + f, g_ref, b_ref)
        x2_ref[rs, :] = xn
        x2b_ref[rs, :] = xn.astype(BF16)

    @pl.when(i == 0)
    def _():
        idx_copy(0, 0).start()
        idx_copy(1, 1).start()
        gather(0, 0)

    gather(2 * i + 1, 1)
    reduce(0)

    @pl.when(2 * i + 2 < n_tiles)
    def _():
        gather(2 * i + 2, 0)

    reduce(1)


def _combine_ln(ys, pos_tiles, gates_t, x1, ln_g, ln_b):
    N, D = x1.shape
    tm = ROW_TILE
    row = pl.BlockSpec((2 * tm, D), lambda m: (m, 0))
    vec = pl.BlockSpec((1, D), lambda m: (0, 0))
    anyspec = pl.BlockSpec(memory_space=pl.ANY)
    return pl.pallas_call(
        functools.partial(_combine_kernel, tm=tm),
        grid=(N // (2 * tm),),
        in_specs=[anyspec, anyspec, pl.BlockSpec((2 * tm, TOP_K), lambda m: (m, 0)), row, vec, vec],
        out_specs=[row, row],
        out_shape=[jax.ShapeDtypeStruct((N, D), F32), jax.ShapeDtypeStruct((N, D), BF16)],
        scratch_shapes=[pltpu.SMEM((2, TOP_K * tm), I32), pltpu.VMEM((2, TOP_K, tm, D), F32),
                        pltpu.SemaphoreType.DMA((2,)), pltpu.SemaphoreType.DMA((2,))],
        compiler_params=_params("arbitrary"),
        name="moe_combine_ln",
    )(pos_tiles, ys, gates_t, x1, ln_g, ln_b)


def _moe_block(post, layer, w_gate_up, b_gate_up, w_down, b_down, ln_g, ln_b):
    x1, x1p, topi, gates, rank, cnt = post
    counts = cnt[:, 0].astype(I32)
    pos, be, valid, pad_lo, pad_hi, n_blocks = _slot_layout(topi, rank, counts, MOE_ROWS)
    xs = _dispatch(x1p, _tile_major(pos, DISPATCH_TILE), pad_lo, pad_hi, n_blocks * MOE_ROWS)
    ys = _expert_ffn(xs, be, valid, layer, w_gate_up, b_gate_up, w_down, b_down)
    return _combine_ln(ys, _tile_major(pos, ROW_TILE), jnp.transpose(gates), x1, ln_g, ln_b)


DN_COL_SUB = 256


def _dn_inproj_kernel(x_ref, w_ref, cw_ref, o_ref, pbuf, *, mode, tm, tn, seq_tiles, q_tiles):
    n = pl.program_id(0)
    m = pl.program_id(1)
    if mode != "z":
        @pl.when(m % seq_tiles == 0)
        def _():
            pbuf[0:8, :] = jnp.zeros((8, tn), F32)

    x = x_ref[...]
    sc = jnp.where(n < q_tiles, HEAD_DIM ** -0.5, 1.0).astype(F32)
    for cs in range(tn // DN_COL_SUB):
        cols = slice(cs * DN_COL_SUB, (cs + 1) * DN_COL_SUB)
        p = jnp.dot(x, w_ref[:, cols], preferred_element_type=F32)
        if mode == "z":
            o_ref[:, cols] = p
            continue
        pbuf[8:tm + 8, cols] = p
        cw = cw_ref[:, cols]
        y = cw[0:1] * pbuf[5:tm + 5, cols]
        y = y + cw[1:2] * pbuf[6:tm + 6, cols]
        y = y + cw[2:3] * pbuf[7:tm + 7, cols]
        y = y + cw[3:4] * p
        pbuf[0:8, cols] = pbuf[tm:tm + 8, cols]
        y = y * jax.nn.sigmoid(y)
        if mode == "v":
            o_ref[:, cols] = y
            continue
        for hh in range(DN_COL_SUB // HEAD_DIM):
            sl = slice(hh * HEAD_DIM, (hh + 1) * HEAD_DIM)
            ys = y[:, sl]
            ss = jnp.sum(ys * ys, axis=-1, keepdims=True)
            o_ref[:, cs * DN_COL_SUB + hh * HEAD_DIM:cs * DN_COL_SUB + (hh + 1) * HEAD_DIM] = (
                ys * lax.rsqrt(ss + DN_EPS) * sc)


def _dn_inproj(xb, w, conv_w, seq_len, mode, col0, width):
    N, K = xb.shape
    tm, tn = 512, 1024
    t0 = col0 // tn
    q_w = DN_QK_HEADS * HEAD_DIM
    conv_tiles = conv_w.shape[1] // tn
    kern = functools.partial(_dn_inproj_kernel, mode=mode, tm=tm, tn=tn, seq_tiles=seq_len // tm,
                             q_tiles=q_w // tn if mode == "qk" else 0)
    return pl.pallas_call(
        kern,
        grid=(width // tn, N // tm),
        in_specs=[pl.BlockSpec((tm, K), lambda n, m: (m, 0)),
                  pl.BlockSpec((K, tn), lambda n, m: (0, t0 + n)),
                  pl.BlockSpec((DN_CONV, tn), lambda n, m: (0, jnp.minimum(t0 + n, conv_tiles - 1)))],
        out_specs=pl.BlockSpec((tm, tn), lambda n, m: (m, n)),
        out_shape=jax.ShapeDtypeStruct((N, width), F32),
        scratch_shapes=[pltpu.VMEM((tm + 8, tn), F32)],
        compiler_params=_params("arbitrary", "arbitrary"),
        name=f"dn_inproj_{mode}",
    )(xb, w, conv_w)


def _dn_gates_kernel(x_ref, w_ref, alog_ref, dtb_ref, o_ref):
    r = jnp.dot(x_ref[...], w_ref[...], preferred_element_type=F32)
    lane = lax.broadcasted_iota(I32, r.shape, 1)
    row = lax.broadcasted_iota(I32, r.shape, 0)
    beta = jax.nn.sigmoid(r)
    z = r + dtb_ref[...]
    softplus = jnp.maximum(z, 0.0) + jnp.log1p(jnp.exp(-jnp.abs(z)))
    g = -jnp.exp(alog_ref[...]) * softplus
    in_chunk = row % DN_CHUNK
    s = 1
    while s < DN_CHUNK:
        g = g + jnp.where(in_chunk >= s, pltpu.roll(g, s, axis=0), 0.0)
        s *= 2
    o_ref[...] = jnp.where(lane < DN_V_HEADS, beta, g)


def _dn_gates(xb, w_ba, a_log, dt_bias):
    N, K = xb.shape
    H = DN_V_HEADS
    tm = 512
    lanes = 128
    w_pad = jnp.zeros((K, lanes), BF16).at[:, :2 * H].set(w_ba)
    alog_pad = jnp.zeros((1, lanes), F32).at[0, H:2 * H].set(a_log)
    dtb_pad = jnp.zeros((1, lanes), F32).at[0, H:2 * H].set(dt_bias)
    vec = pl.BlockSpec((1, lanes), lambda m: (0, 0))
    return pl.pallas_call(
        _dn_gates_kernel,
        grid=(N // tm,),
        in_specs=[pl.BlockSpec((tm, K), lambda m: (m, 0)), pl.BlockSpec((K, lanes), lambda m: (0, 0)), vec, vec],
        out_specs=pl.BlockSpec((tm, lanes), lambda m: (m, 0)),
        out_shape=jax.ShapeDtypeStruct((N, lanes), F32),
        compiler_params=_params("arbitrary"),
        name="dn_gates",
    )(xb, w_pad, alog_pad, dtb_pad)


DN_HEADS_PER_STEP = 8
DN_CHUNKS_PER_STEP = 4


def _delta_kernel(q_ref, k_ref, v_ref, z_ref, bg_ref, gtp_ref, ng_ref, o_ref, s_ref):
    C = DN_CHUNK
    HD = HEAD_DIM
    nh, nc = DN_HEADS_PER_STEP, DN_CHUNKS_PER_STEP
    npair = nh // 2
    ngrp = nh // 4
    nt = (((1,), (1,)), ((), ()))

    @pl.when(pl.program_id(2) == 0)
    def _():
        s_ref[...] = jnp.zeros_like(s_ref)

    ri2 = lax.broadcasted_iota(I32, (C, 2 * C), 0)
    li2 = lax.broadcasted_iota(I32, (C, 2 * C), 1)
    left = li2 < C
    ci2 = li2 % C
    incl2 = ri2 >= ci2
    strict2 = ri2 > ci2
    left_row = lax.broadcasted_iota(I32, (1, 2 * C), 1) < C
    ri4 = lax.broadcasted_iota(I32, (C, 4 * C), 0)
    li4 = lax.broadcasted_iota(I32, (C, 4 * C), 1)
    eye4 = (ri4 == li4 % C).astype(F32)
    blk4 = (lax.broadcasted_iota(I32, (4 * C, 4 * C), 0) // C) == (lax.broadcasted_iota(I32, (4 * C, 4 * C), 1) // C)
    zero_hd = jnp.zeros((C, HD), BF16)
    zero_s = jnp.zeros((HD, HD), BF16)

    def block_diag4(x):
        return jnp.where(blk4, jnp.concatenate([x, x, x, x], axis=0), 0.0).astype(BF16)

    b2, lhs2, rhs, qg, glast, bcat, qcat = {}, {}, {}, {}, {}, {}, {}
    for c in range(nc):
        rows = slice(c * C, (c + 1) * C)
        for pj in range(npair):
            csl = slice(pj * HD, (pj + 1) * HD)
            q = q_ref[0, rows, csl]
            k = k_ref[0, rows, csl]
            kk = jnp.concatenate([k, k], axis=0)
            m1 = lax.dot_general(jnp.concatenate([k, q], axis=0).astype(BF16), kk.astype(BF16), nt,
                                 preferred_element_type=F32)
            ja, jb = 2 * pj, 2 * pj + 1
            bcol = [bg_ref[0, 0, rows, j:j + 1] for j in (ja, jb)]
            gcol = [bg_ref[0, 0, rows, nh + j:nh + j + 1] for j in (ja, jb)]
            grow2 = gtp_ref[0, 0, pj:pj + 1, c * 2 * C:(c + 1) * 2 * C]
            gcol2 = jnp.where(left, gcol[0], gcol[1])
            bcol2 = jnp.where(left, bcol[0], bcol[1])
            decay2 = jnp.exp(jnp.where(incl2, gcol2 - grow2, -jnp.inf))
            b2[c, pj] = jnp.where(strict2, -(m1[:C] * bcol2 * decay2), 0.0)
            qk2 = (m1[C:] * decay2).astype(BF16)
            gl = [g[C - 1:C, :] for g in gcol]
            glast2 = jnp.where(left_row, gl[0], gl[1])
            kdec_t2 = (kk.T * jnp.exp(glast2 - grow2)).astype(BF16)
            lhs2[c, pj] = jnp.concatenate([qk2, kdec_t2], axis=0)
            for i, j in enumerate((ja, jb)):
                eg = jnp.exp(gcol[i])
                v = v_ref[0, rows, j * HD:(j + 1) * HD]
                kb = k * bcol[i]
                rhs[c, j] = jnp.concatenate([v * bcol[i], kb * eg], axis=1).astype(BF16)
                qg[c, j] = (q * eg).astype(BF16)
                glast[c, j] = gl[i]

    groups = [(c, g) for c in range(nc) for g in range(ngrp)]
    for c, g in groups:
        bcat[c, g] = jnp.concatenate([b2[c, 2 * g], b2[c, 2 * g + 1]], axis=1)
        qcat[c, g] = eye4 + bcat[c, g]
    for it in groups:
        bcat[it] = jnp.dot(bcat[it].astype(BF16), block_diag4(bcat[it]), preferred_element_type=F32)
    for _ in range(4):
        for it in groups:
            r = jnp.dot(jnp.concatenate([qcat[it], bcat[it]], axis=0).astype(BF16), block_diag4(bcat[it]),
                        preferred_element_type=F32)
            qcat[it] = qcat[it] + r[:C]
            bcat[it] = r[C:]
    uw = {}
    for c, g in groups:
        it = (c, g)
        t_inv = qcat[it] + jnp.dot(qcat[it].astype(BF16), block_diag4(bcat[it]), preferred_element_type=F32)
        rstack = jnp.concatenate([rhs[c, 4 * g + i] for i in range(4)], axis=0)
        r = jnp.dot(block_diag4(t_inv), rstack, preferred_element_type=F32)
        for i in range(4):
            uw[c, 4 * g + i] = r[i * C:(i + 1) * C]

    states = [s_ref[j] for j in range(nh)]
    ng = ng_ref[...]
    for c in range(nc):
        rows = slice(c * C, (c + 1) * C)
        r1 = {}
        for pj in range(npair):
            ja, jb = 2 * pj, 2 * pj + 1
            lhs = jnp.concatenate(
                [jnp.concatenate([uw[c, j][:, HD:].astype(BF16), qg[c, j]], axis=0) for j in (ja, jb)], axis=1)
            sa, sb = states[ja].astype(BF16), states[jb].astype(BF16)
            sbd = jnp.concatenate([jnp.concatenate([sa, zero_s], axis=1),
                                   jnp.concatenate([zero_s, sb], axis=1)], axis=0)
            r1[pj] = jnp.dot(lhs, sbd, preferred_element_type=F32)
        r2 = {}
        for pj in range(npair):
            ja, jb = 2 * pj, 2 * pj + 1
            va = (uw[c, ja][:, :HD] - r1[pj][:C, :HD]).astype(BF16)
            vb = (uw[c, jb][:, :HD] - r1[pj][:C, HD:]).astype(BF16)
            vbd = jnp.concatenate([jnp.concatenate([va, zero_hd], axis=1),
                                   jnp.concatenate([zero_hd, vb], axis=1)], axis=0)
            r2[pj] = jnp.dot(lhs2[c, pj], vbd, preferred_element_type=F32)
        for pj in range(npair):
            for i, j in enumerate((2 * pj, 2 * pj + 1)):
                hs = slice(i * HD, (i + 1) * HD)
                o = r1[pj][C:, hs] + r2[pj][:C, hs]
                states[j] = states[j] * jnp.exp(glast[c, j]) + r2[pj][C:, hs]
                o = o * lax.rsqrt(jnp.mean(o * o, axis=-1, keepdims=True) + DN_EPS) * ng
                zz = z_ref[0, rows, j * HD:(j + 1) * HD]
                o_ref[0, rows, j * HD:(j + 1) * HD] = (o * (zz * jax.nn.sigmoid(zz))).astype(o_ref.dtype)

    for j in range(nh):
        s_ref[j] = states[j]


def _delta_rule(qk, v, z, bg, norm_g, B, S):
    HD = HEAD_DIM
    C = DN_CHUNK
    hps = DN_HEADS_PER_STEP
    T = DN_CHUNKS_PER_STEP * C
    q_w = DN_QK_HEADS * HD
    v_w = DN_V_HEADS * HD
    qkw = hps // 2 * HD
    vw = hps * HD
    n_hg = DN_V_HEADS // hps
    qk3 = qk.reshape(B, S, 2 * q_w)
    v3 = v.reshape(B, S, v_w)
    z3 = z.reshape(B, S, v_w)
    beta = jnp.transpose(bg[:, :DN_V_HEADS].reshape(B, S, n_hg, hps), (0, 2, 1, 3))
    gcum = jnp.transpose(bg[:, DN_V_HEADS:2 * DN_V_HEADS].reshape(B, S, n_hg, hps), (0, 2, 1, 3))
    bg4 = jnp.concatenate([beta, gcum], axis=-1)
    gt4 = jnp.transpose(gcum.reshape(B, n_hg, S // C, C, hps // 2, 2), (0, 1, 4, 2, 5, 3)).reshape(
        B, n_hg, hps // 2, 2 * S)
    return pl.pallas_call(
        _delta_kernel,
        grid=(B, n_hg, S // T),
        in_specs=[pl.BlockSpec((1, T, qkw), lambda b, h, t: (b, t, h)),
                  pl.BlockSpec((1, T, qkw), lambda b, h, t: (b, t, q_w // qkw + h)),
                  pl.BlockSpec((1, T, vw), lambda b, h, t: (b, t, h)),
                  pl.BlockSpec((1, T, vw), lambda b, h, t: (b, t, h)),
                  pl.BlockSpec((1, 1, T, 2 * hps), lambda b, h, t: (b, h, t, 0)),
                  pl.BlockSpec((1, 1, hps // 2, 2 * T), lambda b, h, t: (b, h, 0, t)),
                  pl.BlockSpec((1, HD), lambda b, h, t: (0, 0))],
        out_specs=pl.BlockSpec((1, T, vw), lambda b, h, t: (b, t, h)),
        out_shape=jax.ShapeDtypeStruct((B, S, v_w), BF16),
        scratch_shapes=[pltpu.VMEM((hps, HD, HD), F32)],
        compiler_params=_params("arbitrary", "arbitrary", "arbitrary"),
        name="delta_rule",
    )(qk3, qk3, v3, z3, bg4, gt4, norm_g.reshape(1, HD)).reshape(B * S, v_w)


def kernel(x, rel_bias, attn_w_in, attn_w_out, dn_w_in, dn_conv_w, dn_a_log, dn_dt_bias, dn_norm_g, dn_w_out,
           ln1_g, ln1_b, router_w, router_b, w_gate_up, b_gate_up, w_down, b_down, ln2_g, ln2_b):
    B, S, D = x.shape
    N = B * S
    E = router_w.shape[2]
    x2d = x.reshape(N, D)

    def vec(p):
        return p.reshape(1, -1)

    def router(i):
        return jnp.transpose(router_w[i]).astype(BF16), router_b[i].reshape(E, 1)

    def moe(i, post):
        return _moe_block(post, i, w_gate_up, b_gate_up, w_down, b_down, vec(ln2_g[i]), vec(ln2_b[i]))

    w_in = attn_w_in[0].astype(BF16)
    os_, lses = [], []
    for g, (window, dilation) in enumerate(DILATED_GROUPS):
        assert window // dilation == ATTN_BLOCK
        qkv = _attn_inproj(x2d, w_in, g, dilation, B, S)
        bias = _attn_bias(rel_bias[:, g * ATTN_HEADS:(g + 1) * ATTN_HEADS], dilation)
        o, lse = _dilated_group(qkv, bias, g)
        os_.append(o)
        lses.append(lse)
    rwt, rb = router(0)
    post = _attn_out(os_, lses, attn_w_out[0].astype(BF16), x2d, vec(ln1_g[0]), vec(ln1_b[0]), rwt, rb, S)
    x2, x2b = moe(0, post)

    conv_cols = dn_conv_w.shape[2]
    main_cols = conv_cols + DN_V_HEADS * HEAD_DIM
    w_in = dn_w_in[0].astype(BF16)
    q_w = DN_QK_HEADS * HEAD_DIM
    v_w = DN_V_HEADS * HEAD_DIM
    qk = _dn_inproj(x2b, w_in, dn_conv_w[0], S, "qk", 0, 2 * q_w)
    v = _dn_inproj(x2b, w_in, dn_conv_w[0], S, "v", 2 * q_w, v_w)
    z = _dn_inproj(x2b, w_in, dn_conv_w[0], S, "z", conv_cols, v_w)
    bg = _dn_gates(x2b, w_in[:, main_cols:], dn_a_log[0], dn_dt_bias[0])
    o = _delta_rule(qk, v, z, bg, dn_norm_g[0], B, S)
    rwt, rb = router(1)
    post = _dn_out(o, dn_w_out[0].astype(BF16), x2, vec(ln1_g[1]), vec(ln1_b[1]), rwt, rb)
    x3, _ = moe(1, post)
    return x3.reshape(B, S, D)
```

```python
import functools
import math

import jax
import jax.numpy as jnp
from jax import lax
from jax.experimental import pallas as pl
from jax.experimental.pallas import tpu as pltpu

F32 = jnp.float32
BF16 = jnp.bfloat16
I32 = jnp.int32
U32 = jnp.uint32

DEPTH = 2
DEEPNORM_ALPHA = (2.0 * DEPTH) ** 0.25
LN_EPS = 1e-5
HEAD_DIM = 128
ATTN_HEADS = 8
ATTN_WIDTH = ATTN_HEADS * HEAD_DIM
DILATED_GROUPS = ((128, 1), (512, 4), (2048, 16))
ATTN_BLOCK = 128
MAX_DISTANCE = 2048
DN_QK_HEADS = 16
DN_V_HEADS = 32
DN_CONV = 4
DN_CHUNK = 64
DN_EPS = 1e-6
TOP_K = 4
SWIGLU_LIMIT = 7.0
SWIGLU_ALPHA = 1.702

MOE_ROWS = 1024
MOE_HALF = 512
MOE_FF_CHUNK = 512
ROW_TILE = 256
DISPATCH_TILE = 512
VMEM_LIMIT = 58 * 1024 * 1024


def _params(*semantics):
    return pltpu.CompilerParams(dimension_semantics=semantics, vmem_limit_bytes=VMEM_LIMIT)


def _pack_bf16_pair(lo, hi):
    lo_bits = lax.bitcast_convert_type(lo.astype(BF16).astype(F32), U32)
    hi_bits = lax.bitcast_convert_type(hi.astype(BF16).astype(F32), U32)
    return (hi_bits & jnp.uint32(0xFFFF0000)) | (lo_bits >> 16)


def _unpack_bf16_pair(u):
    lo = lax.bitcast_convert_type(u << 16, F32)
    hi = lax.bitcast_convert_type(u & jnp.uint32(0xFFFF0000), F32)
    return lo, hi


def _attn_inproj_kernel(a_ref, w_ref, o_ref, acc_ref, *, d, tm):
    res = jnp.dot(a_ref[...].astype(BF16), w_ref[...], preferred_element_type=F32)
    if d == 1:
        o_ref[0, 0] = res.astype(o_ref.dtype)
        return
    lanes = acc_ref.shape[2]
    for cb in range(acc_ref.shape[0]):
        acc_ref[cb] = res[:, cb * lanes:(cb + 1) * lanes]
    for r in range(d):
        for cb in range(acc_ref.shape[0]):
            o_ref[0, r, :, cb * lanes:(cb + 1) * lanes] = (
                acc_ref[cb, pl.ds(r, tm // d, stride=d), :].astype(o_ref.dtype))


def _attn_inproj(x2d, w, g, d, B, S):
    K = x2d.shape[1]
    tm, tn = 512, 1536
    gw = 3 * ATTN_WIDTH
    nt = gw // tn
    mt = S // tm
    return pl.pallas_call(
        functools.partial(_attn_inproj_kernel, d=d, tm=tm),
        grid=(nt, B, mt),
        in_specs=[pl.BlockSpec((tm, K), lambda n, b, m: (b * mt + m, 0)),
                  pl.BlockSpec((K, tn), lambda n, b, m: (0, g * nt + n))],
        out_specs=pl.BlockSpec((1, d, tm // d, tn), lambda n, b, m: (b, 0, m, n)),
        out_shape=jax.ShapeDtypeStruct((B, d, S // d, gw), BF16),
        scratch_shapes=[pltpu.VMEM((tn // HEAD_DIM, tm, HEAD_DIM), F32)],
        compiler_params=_params("arbitrary", "arbitrary", "arbitrary"),
        name=f"attn_inproj_g{g}",
    )(x2d, w)


def _t5_bucket(dist, num_buckets):
    max_exact = num_buckets // 2
    d_f = jnp.maximum(dist, 1).astype(F32)
    large = max_exact + (jnp.log(d_f / max_exact) / math.log(MAX_DISTANCE / max_exact)
                         * (num_buckets - max_exact)).astype(I32)
    large = jnp.minimum(large, num_buckets - 1)
    return jnp.where(dist < max_exact, dist, large)


def _attn_bias(rel_bias_g, dilation):
    bq = ATTN_BLOCK
    a = jnp.arange(bq, dtype=I32)[:, None]
    c = jnp.arange(2 * bq, dtype=I32)[None, :]
    steps = a + bq - c
    valid = (steps >= 0) & (steps <= bq)
    bucket = _t5_bucket(jnp.maximum(steps, 0) * dilation, rel_bias_g.shape[0])
    bias = jnp.transpose(rel_bias_g[bucket], (2, 0, 1)).astype(F32)
    return jnp.where(valid[None], bias, -jnp.inf)


def _attn_kernel(q_ref, kp_ref, kc_ref, vp_ref, vc_ref, bias_ref, o_ref, lse_ref):
    bq = ATTN_BLOCK
    first = pl.program_id(2) == 0
    scale = HEAD_DIM ** -0.5
    nt = (((1,), (1,)), ((), ()))
    lse_ref[0, 0] = jnp.zeros(lse_ref.shape[2:], F32)
    heads = range(ATTN_HEADS)
    sls = [slice(h * HEAD_DIM, (h + 1) * HEAD_DIM) for h in heads]
    s_p = [lax.dot_general(q_ref[0, 0, :, sls[h]], kp_ref[0, 0, :, sls[h]], nt, preferred_element_type=F32)
           for h in heads]
    s_c = [lax.dot_general(q_ref[0, 0, :, sls[h]], kc_ref[0, 0, :, sls[h]], nt, preferred_element_type=F32)
           for h in heads]
    p_p, p_c = [], []
    for h in heads:
        sp = jnp.where(first, -jnp.inf, s_p[h] * scale + bias_ref[h, :, :bq])
        sc = s_c[h] * scale + bias_ref[h, :, bq:]
        m = jnp.maximum(jnp.max(sp, axis=-1, keepdims=True), jnp.max(sc, axis=-1, keepdims=True))
        e_p = jnp.exp(sp - m)
        e_c = jnp.exp(sc - m)
        l = jnp.sum(e_p, axis=-1, keepdims=True) + jnp.sum(e_c, axis=-1, keepdims=True)
        inv = 1.0 / l
        p_p.append((e_p * inv).astype(BF16))
        p_c.append((e_c * inv).astype(BF16))
        lse_ref[0, 0, :, h:h + 1] = m + jnp.log(l)
    for h in heads:
        o_ref[0, 0, :, sls[h]] = (jnp.dot(p_p[h], vp_ref[0, 0, :, sls[h]], preferred_element_type=F32)
                                  + jnp.dot(p_c[h], vc_ref[0, 0, :, sls[h]], preferred_element_type=F32))


def _dilated_group(qkv, bias, g):
    B, d, L, _ = qkv.shape
    nb = L // ATTN_BLOCK
    blk = (1, 1, ATTN_BLOCK, ATTN_WIDTH)
    lse_blk = (1, 1, ATTN_BLOCK, HEAD_DIM)

    def spec(kind, prev):
        if prev:
            return pl.BlockSpec(blk, lambda b, r, i: (b, r, jnp.maximum(i - 1, 0), kind))
        return pl.BlockSpec(blk, lambda b, r, i: (b, r, i, kind))

    return pl.pallas_call(
        _attn_kernel,
        grid=(B, d, nb),
        in_specs=[spec(0, False), spec(1, True), spec(1, False), spec(2, True), spec(2, False),
                  pl.BlockSpec(bias.shape, lambda b, r, i: (0, 0, 0))],
        out_specs=[pl.BlockSpec(blk, lambda b, r, i: (b, r, i, 0)),
                   pl.BlockSpec(lse_blk, lambda b, r, i: (b, r, i, 0))],
        out_shape=[jax.ShapeDtypeStruct((B, d, L, ATTN_WIDTH), F32),
                   jax.ShapeDtypeStruct((B, d, L, HEAD_DIM), F32)],
        compiler_params=_params("arbitrary", "arbitrary", "arbitrary"),
        name=f"dilated_attn_g{g}",
    )(qkv, qkv, qkv, qkv, qkv, bias)


def _layer_norm(y, g_ref, b_ref):
    mu = jnp.mean(y, axis=-1, keepdims=True)
    yc = y - mu
    var = jnp.mean(yc * yc, axis=-1, keepdims=True)
    return yc * lax.rsqrt(var + LN_EPS) * g_ref[...] + b_ref[...]


def _residual_ln_router(h, x_ref, g_ref, b_ref, rwt_ref, rb_ref,
                        x1_ref, x1p_ref, topi_ref, gate_ref, rank_ref, cnt_ref, carry_ref, first_step):
    xn = _layer_norm(DEEPNORM_ALPHA * x_ref[...] + h, g_ref, b_ref)
    x1_ref[...] = xn
    half = xn.shape[1] // 2
    x1p_ref[...] = _pack_bf16_pair(xn[:, :half], xn[:, half:])
    logits = lax.dot_general(rwt_ref[...], xn.astype(BF16), (((1,), (1,)), ((), ())),
                             preferred_element_type=F32) + rb_ref[...]
    n_exp, tm = logits.shape
    eidx = lax.broadcasted_iota(I32, logits.shape, 0)
    vals, hots = [], []
    for k in range(TOP_K):
        mx = jnp.max(logits, axis=0, keepdims=True)
        idx = jnp.min(jnp.where(logits == mx, eidx, n_exp), axis=0, keepdims=True)
        topi_ref[k:k + 1, :] = idx
        vals.append(mx)
        hit = eidx == idx
        hots.append(hit.astype(F32))
        logits = jnp.where(hit, -jnp.inf, logits)
    exps = [jnp.exp(v - vals[0]) for v in vals]
    denom = exps[0] + exps[1] + exps[2] + exps[3]
    for k in range(TOP_K):
        gate_ref[k:k + 1, :] = exps[k] / denom

    @pl.when(first_step)
    def _():
        carry_ref[...] = jnp.zeros_like(carry_ref)

    tri = (lax.broadcasted_iota(I32, (tm, tm), 0) <= lax.broadcasted_iota(I32, (tm, tm), 1)).astype(BF16)
    cum = jnp.dot(jnp.concatenate(hots, axis=0).astype(BF16), tri, preferred_element_type=F32)
    base = carry_ref[...]
    for k in range(TOP_K):
        cum_k = cum[k * n_exp:(k + 1) * n_exp]
        before = cum_k - hots[k] + base
        rank_ref[k:k + 1, :] = jnp.sum(hots[k] * before, axis=0, keepdims=True).astype(I32)
        base = base + cum_k[:, tm - 1:tm]
    carry_ref[...] = base
    cnt_ref[...] = jnp.broadcast_to(base, cnt_ref.shape)


def _attn_out_kernel(o0_ref, o1_ref, o2_ref, l0_ref, l1_ref, l2_ref, w_ref, x_ref, g_ref, b_ref, rwt_ref, rb_ref,
                     x1_ref, x1p_ref, topi_ref, gate_ref, rank_ref, cnt_ref,
                     o_scr, l_scr, merged_ref, carry_ref, *, tm):
    o_refs = (o0_ref, o1_ref, o2_ref)
    l_refs = (l0_ref, l1_ref, l2_ref)
    H = ATTN_HEADS
    for g, (_, d) in enumerate(DILATED_GROUPS):
        for r in range(d):
            for h in range(H):
                o_scr[g * H + h, pl.ds(r, tm // d, stride=d), :] = o_refs[g][0, r, :, h * HEAD_DIM:(h + 1) * HEAD_DIM]
            l_scr[g, pl.ds(r, tm // d, stride=d), :] = l_refs[g][0, r]
    l0, l1, l2 = l_scr[0], l_scr[1], l_scr[2]
    m = jnp.maximum(jnp.maximum(l0, l1), l2)
    e = [jnp.exp(l0 - m), jnp.exp(l1 - m), jnp.exp(l2 - m)]
    den = e[0] + e[1] + e[2]
    wts = [ei / den for ei in e]
    for h in range(H):
        o = wts[0][:, h:h + 1] * o_scr[h]
        o = o + wts[1][:, h:h + 1] * o_scr[H + h]
        o = o + wts[2][:, h:h + 1] * o_scr[2 * H + h]
        merged_ref[:, h * HEAD_DIM:(h + 1) * HEAD_DIM] = o.astype(BF16)
    h_out = jnp.dot(merged_ref[...], w_ref[...], preferred_element_type=F32)
    _residual_ln_router(h_out, x_ref, g_ref, b_ref, rwt_ref, rb_ref, x1_ref, x1p_ref, topi_ref, gate_ref,
                        rank_ref, cnt_ref, carry_ref, pl.program_id(0) == 0)


def _dn_out_kernel(o_ref, w_ref, x_ref, g_ref, b_ref, rwt_ref, rb_ref,
                   x1_ref, x1p_ref, topi_ref, gate_ref, rank_ref, cnt_ref, carry_ref):
    h = jnp.dot(o_ref[...], w_ref[...], preferred_element_type=F32)
    _residual_ln_router(h, x_ref, g_ref, b_ref, rwt_ref, rb_ref, x1_ref, x1p_ref, topi_ref,
                        gate_ref, rank_ref, cnt_ref, carry_ref, pl.program_id(0) == 0)


def _post_specs(tm, D, E, idx):
    row = lambda *a: (idx(*a), 0)
    const = lambda *a: (0, 0)
    col = lambda *a: (0, idx(*a))
    ins = [pl.BlockSpec((tm, D), row),
           pl.BlockSpec((1, D), const), pl.BlockSpec((1, D), const),
           pl.BlockSpec((E, D), const), pl.BlockSpec((E, 1), const)]
    outs = [pl.BlockSpec((tm, D), row), pl.BlockSpec((tm, D // 2), row),
            pl.BlockSpec((TOP_K, tm), col), pl.BlockSpec((TOP_K, tm), col), pl.BlockSpec((TOP_K, tm), col),
            pl.BlockSpec((E, 128), const)]
    return ins, outs


def _post_shapes(N, D, E):
    return [jax.ShapeDtypeStruct((N, D), F32), jax.ShapeDtypeStruct((N, D // 2), U32),
            jax.ShapeDtypeStruct((TOP_K, N), I32), jax.ShapeDtypeStruct((TOP_K, N), F32),
            jax.ShapeDtypeStruct((TOP_K, N), I32), jax.ShapeDtypeStruct((E, 128), F32)]


def _attn_out(os_, lses, w_out, x2d, ln_g, ln_b, rwt, rb, S):
    N, D = x2d.shape
    E = rwt.shape[0]
    tm = 256
    mt = S // tm
    ins, outs = _post_specs(tm, D, E, lambda m: m)
    grp_specs = []
    for width in (ATTN_WIDTH, HEAD_DIM):
        for _, d in DILATED_GROUPS:
            grp_specs.append(pl.BlockSpec((1, d, tm // d, width), lambda m: (m // mt, 0, m % mt, 0)))
    return pl.pallas_call(
        functools.partial(_attn_out_kernel, tm=tm),
        grid=(N // tm,),
        in_specs=grp_specs + [pl.BlockSpec(w_out.shape, lambda m: (0, 0))] + ins,
        out_specs=outs,
        out_shape=_post_shapes(N, D, E),
        scratch_shapes=[pltpu.VMEM((len(DILATED_GROUPS) * ATTN_HEADS, tm, HEAD_DIM), F32),
                        pltpu.VMEM((len(DILATED_GROUPS), tm, HEAD_DIM), F32),
                        pltpu.VMEM((tm, ATTN_WIDTH), BF16),
                        pltpu.VMEM((E, 1), F32)],
        compiler_params=_params("arbitrary"),
        name="attn_out_ln_router",
    )(*os_, *lses, w_out, x2d, ln_g, ln_b, rwt, rb)


def _dn_out(o, w_out, x2d, ln_g, ln_b, rwt, rb):
    N, D = x2d.shape
    E = rwt.shape[0]
    Kin = o.shape[1]
    tm = 256
    ins, outs = _post_specs(tm, D, E, lambda m: m)
    return pl.pallas_call(
        _dn_out_kernel,
        grid=(N // tm,),
        in_specs=[pl.BlockSpec((tm, Kin), lambda m: (m, 0)),
                  pl.BlockSpec((Kin, D), lambda m: (0, 0), pipeline_mode=pl.Buffered(1))] + ins,
        out_specs=outs,
        out_shape=_post_shapes(N, D, E),
        scratch_shapes=[pltpu.VMEM((E, 1), F32)],
        compiler_params=_params("arbitrary"),
        name="dn_out_ln_router",
    )(o, w_out, x2d, ln_g, ln_b, rwt, rb)


def _slot_layout(topi, rank, counts, rows):
    K, N = topi.shape
    n_exp = counts.shape[0]
    padded = (counts + rows - 1) // rows * rows
    padded_end = jnp.cumsum(padded)
    padded_start = padded_end - padded
    sel = topi[None] == jnp.arange(n_exp, dtype=I32)[:, None, None]
    pos = jnp.sum(jnp.where(sel, padded_start[:, None, None], 0), axis=0) + rank
    n_blocks = -(-(K * N + n_exp * (rows - 1)) // rows)
    n_used = padded_end[-1] // rows
    blk = jnp.arange(n_blocks, dtype=I32)
    be = jnp.sum((blk[:, None] * rows >= padded_end[None, :]).astype(I32), axis=1)
    be = jnp.minimum(be, n_exp - 1)
    be_last = jnp.sum(jnp.where(blk == jnp.maximum(n_used - 1, 0), be, 0))
    used = blk < n_used
    be = jnp.where(used, be, be_last)
    start_b = jnp.sum(jnp.where(be[:, None] == jnp.arange(n_exp)[None, :], padded_start[None, :], 0), axis=1)
    cnt_e = jnp.sum(jnp.where(be[:, None] == jnp.arange(n_exp)[None, :], counts[None, :], 0), axis=1)
    valid = jnp.where(used, jnp.clip(cnt_e - (blk * rows - start_b), 0, rows), 0).astype(I32)
    n_slots = jnp.full((1,), n_blocks * rows, I32)
    pad_lo = jnp.concatenate([padded_start + counts, padded_end[-1:]]).astype(I32)
    pad_hi = jnp.concatenate([padded_end, n_slots]).astype(I32)
    return pos.astype(I32), be.astype(I32), valid, pad_lo, pad_hi, n_blocks


def _tile_major(pos, tm):
    K, N = pos.shape
    return jnp.transpose(pos.reshape(K, N // tm, tm), (1, 0, 2)).reshape(N // tm, K * tm)


def _dispatch_kernel(lo_ref, hi_ref, pos_hbm, x_ref, xs_hbm, idx_smem, zbuf, sem_idx, sem_rows, sem_zero, *, tm):
    i = pl.program_id(0)

    @pl.when(i == 0)
    def _():
        zbuf[...] = jnp.zeros_like(zbuf)
        one = zbuf.at[pl.ds(0, 1)]

        def per_expert(e, carry):
            lo = lo_ref[e]
            hi = hi_ref[e]
            lo_al = jnp.minimum((lo + 7) // 8 * 8, hi)

            def row_start(r, c):
                pltpu.make_async_copy(one, xs_hbm.at[pl.ds(r, 1)], sem_zero).start()
                return c

            def grp_start(q, c):
                pltpu.make_async_copy(zbuf, xs_hbm.at[pl.ds(pl.multiple_of(q * 8, 8), 8)], sem_zero).start()
                return c

            def row_wait(r, c):
                pltpu.make_async_copy(one, xs_hbm.at[pl.ds(0, 1)], sem_zero).wait()
                return c

            def grp_wait(q, c):
                pltpu.make_async_copy(zbuf, xs_hbm.at[pl.ds(0, 8)], sem_zero).wait()
                return c

            lax.fori_loop(lo, lo_al, row_start, 0)
            lax.fori_loop(lo_al // 8, hi // 8, grp_start, 0)
            lax.fori_loop(lo, lo_al, row_wait, 0)
            lax.fori_loop(lo_al // 8, hi // 8, grp_wait, 0)
            return carry

        lax.fori_loop(0, lo_ref.shape[0], per_expert, 0)

    cp = pltpu.make_async_copy(pos_hbm.at[i], idx_smem, sem_idx)
    cp.start()
    cp.wait()

    def issue(jj, carry):
        for u in range(8):
            j = jj * 8 + u
            for k in range(TOP_K):
                dst = idx_smem[k * tm + j]
                pltpu.make_async_copy(x_ref.at[pl.ds(j, 1)], xs_hbm.at[pl.ds(dst, 1)], sem_rows).start()
        return carry

    lax.fori_loop(0, tm // 8, issue, 0)
    for k in range(TOP_K):
        pltpu.make_async_copy(x_ref, xs_hbm.at[pl.ds(0, tm)], sem_rows).wait()


def _dispatch(x1p, pos_tiles, pad_lo, pad_hi, n_slots):
    N, W = x1p.shape
    tm = DISPATCH_TILE
    grid_spec = pltpu.PrefetchScalarGridSpec(
        num_scalar_prefetch=2,
        grid=(N // tm,),
        in_specs=[pl.BlockSpec(memory_space=pl.ANY),
                  pl.BlockSpec((tm, W), lambda i, lo, hi: (i, 0))],
        out_specs=pl.BlockSpec(memory_space=pl.ANY),
        scratch_shapes=[pltpu.SMEM((TOP_K * tm,), I32), pltpu.VMEM((8, W), U32),
                        pltpu.SemaphoreType.DMA, pltpu.SemaphoreType.DMA, pltpu.SemaphoreType.DMA],
    )
    return pl.pallas_call(
        functools.partial(_dispatch_kernel, tm=tm),
        grid_spec=grid_spec,
        out_shape=jax.ShapeDtypeStruct((n_slots, W), U32),
        compiler_params=_params("arbitrary"),
        name="moe_dispatch",
    )(pad_lo, pad_hi, pos_tiles, x1p)


def _ffn_kernel(be_ref, cnt_ref, xs_ref, wg_ref, wl_ref, bg_ref, bl_ref, wd_ref, bd_ref, o_ref, xb_ref, act_ref, *, nfc):
    b = pl.program_id(0)
    p = pl.program_id(1)
    cnt = cnt_ref[b]
    rows = xb_ref.shape[0]
    half = xs_ref.shape[1]
    fc = o_ref.shape[1]
    n_half = rows // MOE_HALF
    up = p < nfc

    @pl.when(jnp.logical_and(cnt > 0, p == 0))
    def _():
        lo, hi = _unpack_bf16_pair(xs_ref[...])
        xb_ref[:, :half] = lo.astype(BF16)
        xb_ref[:, half:] = hi.astype(BF16)

    def up_phase(live):
        wg = wg_ref[0, 0, 0]
        wl = wl_ref[0, 0, 0]
        acts = []
        for h in range(live):
            x = xb_ref[h * MOE_HALF:(h + 1) * MOE_HALF, :]
            hg = jnp.dot(x, wg, preferred_element_type=F32) + bg_ref[0, 0]
            hl = jnp.dot(x, wl, preferred_element_type=F32) + bl_ref[0, 0]
            glu = jnp.minimum(hg, SWIGLU_LIMIT)
            lin = jnp.clip(hl, -SWIGLU_LIMIT, SWIGLU_LIMIT)
            acts.append((glu * jax.nn.sigmoid(SWIGLU_ALPHA * glu) * (lin + 1.0)).astype(BF16))
        for c in range(nfc):
            @pl.when(p == c)
            def _():
                for h in range(live):
                    act_ref[h * MOE_HALF:(h + 1) * MOE_HALF, c * fc:(c + 1) * fc] = acts[h]

    def down_phase(live):
        wd = wd_ref[0, 0, 0] if live else None
        for h in range(live):
            rs = slice(h * MOE_HALF, (h + 1) * MOE_HALF)
            o_ref[rs, :] = jnp.dot(act_ref[rs, :], wd, preferred_element_type=F32) + bd_ref[0, 0]
        for h in range(live, n_half):
            rs = slice(h * MOE_HALF, (h + 1) * MOE_HALF)
            o_ref[rs, :] = jnp.zeros((MOE_HALF, fc), F32)

    for live in range(1, n_half + 1):
        in_range = cnt > (live - 1) * MOE_HALF
        if live < n_half:
            in_range = jnp.logical_and(in_range, cnt <= live * MOE_HALF)

        @pl.when(jnp.logical_and(in_range, up))
        def _():
            up_phase(live)

        @pl.when(jnp.logical_and(in_range, jnp.logical_not(up)))
        def _():
            down_phase(live)

    @pl.when(jnp.logical_and(cnt == 0, jnp.logical_not(up)))
    def _():
        down_phase(0)


def _chunk_major_kernel(w_ref, o_ref):
    fc = o_ref.shape[4]
    for c in range(o_ref.shape[2]):
        o_ref[0, 0, c] = w_ref[0, 0, :, c * fc:(c + 1) * fc].astype(o_ref.dtype)


def _chunk_major_bf16(w, fc):
    L, E, K, N = w.shape
    ks = 256
    return pl.pallas_call(
        _chunk_major_kernel,
        grid=(L, E, K // ks),
        in_specs=[pl.BlockSpec((1, 1, ks, N), lambda l, e, k: (l, e, k, 0))],
        out_specs=pl.BlockSpec((1, 1, N // fc, ks, fc), lambda l, e, k: (l, e, 0, k, 0)),
        out_shape=jax.ShapeDtypeStruct((L, E, N // fc, K, fc), BF16),
        compiler_params=_params("arbitrary", "arbitrary", "arbitrary"),
        name="expert_weights_chunk_major",
    )(w)


def _expert_ffn(xs, be, valid, layer, w_gate_up, b_gate_up, w_down, b_down):
    n_slots, half = xs.shape
    D = 2 * half
    L, E, nfc2, _, fc = w_gate_up.shape
    nfc = nfc2 // 2
    ndc, F = w_down.shape[2], w_down.shape[3]
    F2 = 2 * F
    rows = MOE_ROWS
    n_blocks = n_slots // rows

    def up_chunk(b, p, cnt):
        return jnp.where(cnt[b] > 0, jnp.minimum(p, nfc - 1), nfc - 1)

    def down_chunk(b, p, cnt):
        return jnp.where(cnt[b] > 0, jnp.maximum(p - nfc, 0), ndc - 1)

    def weight(height, col):
        return pl.BlockSpec((1, 1, 1, height, fc), lambda b, p, be, cnt: (layer, be[b], col(b, p, cnt), 0, 0))

    def bias(col):
        return pl.BlockSpec((1, 1, 1, fc), lambda b, p, be, cnt: (layer, be[b], 0, col(b, p, cnt)))

    def lin_chunk(b, p, cnt):
        return nfc + up_chunk(b, p, cnt)

    grid_spec = pltpu.PrefetchScalarGridSpec(
        num_scalar_prefetch=2,
        grid=(n_blocks, nfc + ndc),
        in_specs=[pl.BlockSpec((rows, half), lambda b, p, be, cnt: (b, 0)),
                  weight(D, up_chunk), weight(D, lin_chunk), bias(up_chunk), bias(lin_chunk),
                  weight(F, down_chunk), bias(down_chunk)],
        out_specs=pl.BlockSpec((rows, fc), lambda b, p, be, cnt: (b, jnp.maximum(p - nfc, 0))),
        scratch_shapes=[pltpu.VMEM((rows, D), BF16), pltpu.VMEM((rows, F), BF16)],
    )
    bgu = b_gate_up.reshape(L, E, 1, F2)
    return pl.pallas_call(
        functools.partial(_ffn_kernel, nfc=nfc),
        grid_spec=grid_spec,
        out_shape=jax.ShapeDtypeStruct((n_slots, D), F32),
        compiler_params=_params("arbitrary", "arbitrary"),
        name="expert_ffn",
    )(be, valid, xs, w_gate_up, w_gate_up, bgu, bgu, w_down, b_down.reshape(L, E, 1, D))


def _combine_kernel(pos_hbm, ys_hbm, gate_ref, x_ref, g_ref, b_ref, x2_ref, x2b_ref,
                    idx0_smem, idx1_smem, buf, sem_idx, sem_rows, *, tm):
    i = pl.program_id(0)
    n_tiles = 2 * pl.num_programs(0)
    idx_smem = (idx0_smem, idx1_smem)
    D = x_ref.shape[1]

    def idx_copy(tile, slot):
        return pltpu.make_async_copy(pos_hbm.at[tile], idx_smem[slot], sem_idx.at[slot])

    def gather(tile, slot):
        idx_copy(tile, slot).wait()

        def issue(jj, carry):
            for u in range(8):
                for k in range(TOP_K):
                    src = idx_smem[slot][k * tm + jj * 8 + u]
                    pltpu.make_async_copy(ys_hbm.at[src >> 3, pl.ds(src & 7, 1)], buf.at[slot, k, jj, pl.ds(u, 1)],
                                          sem_rows.at[slot]).start()
            return carry

        lax.fori_loop(0, tm // 8, issue, 0)

        @pl.when(tile + 2 < n_tiles)
        def _():
            idx_copy(tile + 2, slot).start()

    def reduce(slot):
        rs = slice(slot * tm, (slot + 1) * tm)
        for k in range(TOP_K):
            pltpu.make_async_copy(ys_hbm.at[pl.ds(0, tm // 8)], buf.at[slot, k], sem_rows.at[slot]).wait()
        gate = gate_ref[rs, :]
        f = buf[slot, 0].reshape(tm, D) * gate[:, 0:1]
        for k in range(1, TOP_K):
            f = f + buf[slot, k].reshape(tm, D) * gate[:, k:k + 1]
        xn = _layer_norm(DEEPNORM_ALPHA * x_ref[rs, :] + f, g_ref, b_ref)
        x2_ref[rs, :] = xn
        x2b_ref[rs, :] = xn.astype(BF16)

    @pl.when(i == 0)
    def _():
        idx_copy(0, 0).start()
        idx_copy(1, 1).start()
        gather(0, 0)

    gather(2 * i + 1, 1)
    reduce(0)

    @pl.when(2 * i + 2 < n_tiles)
    def _():
        gather(2 * i + 2, 0)

    reduce(1)


def _combine_ln(ys, pos_tiles, gates_t, x1, ln_g, ln_b):
    N, D = x1.shape
    tm = ROW_TILE
    row = pl.BlockSpec((2 * tm, D), lambda m: (m, 0))
    vec = pl.BlockSpec((1, D), lambda m: (0, 0))
    anyspec = pl.BlockSpec(memory_space=pl.ANY)
    return pl.pallas_call(
        functools.partial(_combine_kernel, tm=tm),
        grid=(N // (2 * tm),),
        in_specs=[anyspec, anyspec, pl.BlockSpec((2 * tm, TOP_K), lambda m: (m, 0)), row, vec, vec],
        out_specs=[row, row],
        out_shape=[jax.ShapeDtypeStruct((N, D), F32), jax.ShapeDtypeStruct((N, D), BF16)],
        scratch_shapes=[pltpu.SMEM((TOP_K * tm,), I32), pltpu.SMEM((TOP_K * tm,), I32),
                        pltpu.VMEM((2, TOP_K, tm // 8, 8, D), F32),
                        pltpu.SemaphoreType.DMA((2,)), pltpu.SemaphoreType.DMA((2,))],
        compiler_params=_params("arbitrary"),
        name="moe_combine_ln",
    )(pos_tiles, ys.reshape(ys.shape[0] // 8, 8, D), gates_t, x1, ln_g, ln_b)


def _moe_block(post, layer, w_gate_up, b_gate_up, w_down, b_down, ln_g, ln_b):
    x1, x1p, topi, gates, rank, cnt = post
    counts = cnt[:, 0].astype(I32)
    pos, be, valid, pad_lo, pad_hi, n_blocks = _slot_layout(topi, rank, counts, MOE_ROWS)
    xs = _dispatch(x1p, _tile_major(pos, DISPATCH_TILE), pad_lo, pad_hi, n_blocks * MOE_ROWS)
    ys = _expert_ffn(xs, be, valid, layer, w_gate_up, b_gate_up, w_down, b_down)
    return _combine_ln(ys, _tile_major(pos, ROW_TILE), jnp.transpose(gates), x1, ln_g, ln_b)


DN_COL_SUB = 256


def _dn_inproj_kernel(x_ref, w_ref, cw_ref, o_ref, pbuf, *, mode, tm, tn, seq_tiles, q_tiles):
    n = pl.program_id(0)
    m = pl.program_id(1)
    if mode != "z":
        @pl.when(m % seq_tiles == 0)
        def _():
            pbuf[0:8, :] = jnp.zeros((8, tn), F32)

    x = x_ref[...]
    sc = jnp.where(n < q_tiles, HEAD_DIM ** -0.5, 1.0).astype(F32)
    for cs in range(tn // DN_COL_SUB):
        cols = slice(cs * DN_COL_SUB, (cs + 1) * DN_COL_SUB)
        p = jnp.dot(x, w_ref[:, cols], preferred_element_type=F32)
        if mode == "z":
            o_ref[:, cols] = p
            continue
        pbuf[8:tm + 8, cols] = p
        cw = cw_ref[:, cols]
        y = cw[0:1] * pbuf[5:tm + 5, cols]
        y = y + cw[1:2] * pbuf[6:tm + 6, cols]
        y = y + cw[2:3] * pbuf[7:tm + 7, cols]
        y = y + cw[3:4] * p
        pbuf[0:8, cols] = pbuf[tm:tm + 8, cols]
        y = y * jax.nn.sigmoid(y)
        if mode == "v":
            o_ref[:, cols] = y
            continue
        for hh in range(DN_COL_SUB // HEAD_DIM):
            sl = slice(hh * HEAD_DIM, (hh + 1) * HEAD_DIM)
            ys = y[:, sl]
            ss = jnp.sum(ys * ys, axis=-1, keepdims=True)
            o_ref[:, cs * DN_COL_SUB + hh * HEAD_DIM:cs * DN_COL_SUB + (hh + 1) * HEAD_DIM] = (
                ys * lax.rsqrt(ss + DN_EPS) * sc)


def _dn_inproj(xb, w, conv_w, seq_len, mode, col0, width):
    N, K = xb.shape
    tm, tn = 512, 1024
    t0 = col0 // tn
    q_w = DN_QK_HEADS * HEAD_DIM
    conv_tiles = conv_w.shape[1] // tn
    kern = functools.partial(_dn_inproj_kernel, mode=mode, tm=tm, tn=tn, seq_tiles=seq_len // tm,
                             q_tiles=q_w // tn if mode == "qk" else 0)
    return pl.pallas_call(
        kern,
        grid=(width // tn, N // tm),
        in_specs=[pl.BlockSpec((tm, K), lambda n, m: (m, 0)),
                  pl.BlockSpec((K, tn), lambda n, m: (0, t0 + n)),
                  pl.BlockSpec((DN_CONV, tn), lambda n, m: (0, jnp.minimum(t0 + n, conv_tiles - 1)))],
        out_specs=pl.BlockSpec((tm, tn), lambda n, m: (m, n)),
        out_shape=jax.ShapeDtypeStruct((N, width), F32),
        scratch_shapes=[pltpu.VMEM((tm + 8, tn), F32)],
        compiler_params=_params("arbitrary", "arbitrary"),
        name=f"dn_inproj_{mode}",
    )(xb, w, conv_w)


def _dn_gates_kernel(x_ref, w_ref, alog_ref, dtb_ref, o_ref):
    r = jnp.dot(x_ref[...], w_ref[...], preferred_element_type=F32)
    lane = lax.broadcasted_iota(I32, r.shape, 1)
    row = lax.broadcasted_iota(I32, r.shape, 0)
    beta = jax.nn.sigmoid(r)
    z = r + dtb_ref[...]
    softplus = jnp.maximum(z, 0.0) + jnp.log1p(jnp.exp(-jnp.abs(z)))
    g = -jnp.exp(alog_ref[...]) * softplus
    in_chunk = row % DN_CHUNK
    s = 1
    while s < DN_CHUNK:
        g = g + jnp.where(in_chunk >= s, pltpu.roll(g, s, axis=0), 0.0)
        s *= 2
    o_ref[...] = jnp.where(lane < DN_V_HEADS, beta, g)


def _dn_gates(xb, w_ba, a_log, dt_bias):
    N, K = xb.shape
    H = DN_V_HEADS
    tm = 512
    lanes = 128
    w_pad = jnp.zeros((K, lanes), BF16).at[:, :2 * H].set(w_ba)
    alog_pad = jnp.zeros((1, lanes), F32).at[0, H:2 * H].set(a_log)
    dtb_pad = jnp.zeros((1, lanes), F32).at[0, H:2 * H].set(dt_bias)
    vec = pl.BlockSpec((1, lanes), lambda m: (0, 0))
    return pl.pallas_call(
        _dn_gates_kernel,
        grid=(N // tm,),
        in_specs=[pl.BlockSpec((tm, K), lambda m: (m, 0)), pl.BlockSpec((K, lanes), lambda m: (0, 0)), vec, vec],
        out_specs=pl.BlockSpec((tm, lanes), lambda m: (m, 0)),
        out_shape=jax.ShapeDtypeStruct((N, lanes), F32),
        compiler_params=_params("arbitrary"),
        name="dn_gates",
    )(xb, w_pad, alog_pad, dtb_pad)


DN_HEADS_PER_STEP = 8
DN_CHUNKS_PER_STEP = 4


def _delta_kernel(q_ref, k_ref, v_ref, z_ref, bg_ref, gtp_ref, ng_ref, o_ref, s_ref):
    C = DN_CHUNK
    HD = HEAD_DIM
    nh, nc = DN_HEADS_PER_STEP, DN_CHUNKS_PER_STEP
    npair = nh // 2
    ngrp = nh // 4
    nt = (((1,), (1,)), ((), ()))

    @pl.when(pl.program_id(2) == 0)
    def _():
        s_ref[...] = jnp.zeros_like(s_ref)

    ri2 = lax.broadcasted_iota(I32, (C, 2 * C), 0)
    li2 = lax.broadcasted_iota(I32, (C, 2 * C), 1)
    left = li2 < C
    ci2 = li2 % C
    incl2 = ri2 >= ci2
    strict2 = ri2 > ci2
    left_row = lax.broadcasted_iota(I32, (1, 2 * C), 1) < C
    ri4 = lax.broadcasted_iota(I32, (C, 4 * C), 0)
    li4 = lax.broadcasted_iota(I32, (C, 4 * C), 1)
    eye4 = (ri4 == li4 % C).astype(F32)
    blk4 = (lax.broadcasted_iota(I32, (4 * C, 4 * C), 0) // C) == (lax.broadcasted_iota(I32, (4 * C, 4 * C), 1) // C)
    zero_hd = jnp.zeros((C, HD), BF16)
    zero_s = jnp.zeros((HD, HD), BF16)

    def block_diag4(x):
        return jnp.where(blk4, jnp.concatenate([x, x, x, x], axis=0), 0.0).astype(BF16)

    b2, lhs2, rhs, qg, glast, bcat, qcat = {}, {}, {}, {}, {}, {}, {}
    for c in range(nc):
        rows = slice(c * C, (c + 1) * C)
        for pj in range(npair):
            csl = slice(pj * HD, (pj + 1) * HD)
            q = q_ref[0, rows, csl]
            k = k_ref[0, rows, csl]
            kk = jnp.concatenate([k, k], axis=0)
            m1 = lax.dot_general(jnp.concatenate([k, q], axis=0).astype(BF16), kk.astype(BF16), nt,
                                 preferred_element_type=F32)
            ja, jb = 2 * pj, 2 * pj + 1
            bcol = [bg_ref[0, 0, rows, j:j + 1] for j in (ja, jb)]
            gcol = [bg_ref[0, 0, rows, nh + j:nh + j + 1] for j in (ja, jb)]
            grow2 = gtp_ref[0, 0, pj:pj + 1, c * 2 * C:(c + 1) * 2 * C]
            gcol2 = jnp.where(left, gcol[0], gcol[1])
            bcol2 = jnp.where(left, bcol[0], bcol[1])
            decay2 = jnp.exp(jnp.where(incl2, gcol2 - grow2, -jnp.inf))
            b2[c, pj] = jnp.where(strict2, -(m1[:C] * bcol2 * decay2), 0.0)
            qk2 = (m1[C:] * decay2).astype(BF16)
            gl = [g[C - 1:C, :] for g in gcol]
            glast2 = jnp.where(left_row, gl[0], gl[1])
            kdec_t2 = (kk.T * jnp.exp(glast2 - grow2)).astype(BF16)
            lhs2[c, pj] = jnp.concatenate([qk2, kdec_t2], axis=0)
            for i, j in enumerate((ja, jb)):
                eg = jnp.exp(gcol[i])
                v = v_ref[0, rows, j * HD:(j + 1) * HD]
                kb = k * bcol[i]
                rhs[c, j] = jnp.concatenate([v * bcol[i], kb * eg], axis=1).astype(BF16)
                qg[c, j] = (q * eg).astype(BF16)
                glast[c, j] = gl[i]

    groups = [(c, g) for c in range(nc) for g in range(ngrp)]
    for c, g in groups:
        bcat[c, g] = jnp.concatenate([b2[c, 2 * g], b2[c, 2 * g + 1]], axis=1)
        qcat[c, g] = eye4 + bcat[c, g]
    for it in groups:
        bcat[it] = jnp.dot(bcat[it].astype(BF16), block_diag4(bcat[it]), preferred_element_type=F32)
    for _ in range(4):
        for it in groups:
            r = jnp.dot(jnp.concatenate([qcat[it], bcat[it]], axis=0).astype(BF16), block_diag4(bcat[it]),
                        preferred_element_type=F32)
            qcat[it] = qcat[it] + r[:C]
            bcat[it] = r[C:]
    uw = {}
    for c, g in groups:
        it = (c, g)
        t_inv = qcat[it] + jnp.dot(qcat[it].astype(BF16), block_diag4(bcat[it]), preferred_element_type=F32)
        rstack = jnp.concatenate([rhs[c, 4 * g + i] for i in range(4)], axis=0)
        r = jnp.dot(block_diag4(t_inv), rstack, preferred_element_type=F32)
        for i in range(4):
            uw[c, 4 * g + i] = r[i * C:(i + 1) * C]

    states = [s_ref[j] for j in range(nh)]
    ng = ng_ref[...]
    for c in range(nc):
        rows = slice(c * C, (c + 1) * C)
        r1 = {}
        for pj in range(npair):
            ja, jb = 2 * pj, 2 * pj + 1
            lhs = jnp.concatenate(
                [jnp.concatenate([uw[c, j][:, HD:].astype(BF16), qg[c, j]], axis=0) for j in (ja, jb)], axis=1)
            sa, sb = states[ja].astype(BF16), states[jb].astype(BF16)
            sbd = jnp.concatenate([jnp.concatenate([sa, zero_s], axis=1),
                                   jnp.concatenate([zero_s, sb], axis=1)], axis=0)
            r1[pj] = jnp.dot(lhs, sbd, preferred_element_type=F32)
        r2 = {}
        for pj in range(npair):
            ja, jb = 2 * pj, 2 * pj + 1
            va = (uw[c, ja][:, :HD] - r1[pj][:C, :HD]).astype(BF16)
            vb = (uw[c, jb][:, :HD] - r1[pj][:C, HD:]).astype(BF16)
            vbd = jnp.concatenate([jnp.concatenate([va, zero_hd], axis=1),
                                   jnp.concatenate([zero_hd, vb], axis=1)], axis=0)
            r2[pj] = jnp.dot(lhs2[c, pj], vbd, preferred_element_type=F32)
        for pj in range(npair):
            for i, j in enumerate((2 * pj, 2 * pj + 1)):
                hs = slice(i * HD, (i + 1) * HD)
                o = r1[pj][C:, hs] + r2[pj][:C, hs]
                states[j] = states[j] * jnp.exp(glast[c, j]) + r2[pj][C:, hs]
                o = o * lax.rsqrt(jnp.mean(o * o, axis=-1, keepdims=True) + DN_EPS) * ng
                zz = z_ref[0, rows, j * HD:(j + 1) * HD]
                o_ref[0, rows, j * HD:(j + 1) * HD] = (o * (zz * jax.nn.sigmoid(zz))).astype(o_ref.dtype)

    for j in range(nh):
        s_ref[j] = states[j]


def _delta_rule(qk, v, z, bg, norm_g, B, S):
    HD = HEAD_DIM
    C = DN_CHUNK
    hps = DN_HEADS_PER_STEP
    T = DN_CHUNKS_PER_STEP * C
    q_w = DN_QK_HEADS * HD
    v_w = DN_V_HEADS * HD
    qkw = hps // 2 * HD
    vw = hps * HD
    n_hg = DN_V_HEADS // hps
    qk3 = qk.reshape(B, S, 2 * q_w)
    v3 = v.reshape(B, S, v_w)
    z3 = z.reshape(B, S, v_w)
    beta = jnp.transpose(bg[:, :DN_V_HEADS].reshape(B, S, n_hg, hps), (0, 2, 1, 3))
    gcum = jnp.transpose(bg[:, DN_V_HEADS:2 * DN_V_HEADS].reshape(B, S, n_hg, hps), (0, 2, 1, 3))
    bg4 = jnp.concatenate([beta, gcum], axis=-1)
    gt4 = jnp.transpose(gcum.reshape(B, n_hg, S // C, C, hps // 2, 2), (0, 1, 4, 2, 5, 3)).reshape(
        B, n_hg, hps // 2, 2 * S)
    return pl.pallas_call(
        _delta_kernel,
        grid=(B, n_hg, S // T),
        in_specs=[pl.BlockSpec((1, T, qkw), lambda b, h, t: (b, t, h)),
                  pl.BlockSpec((1, T, qkw), lambda b, h, t: (b, t, q_w // qkw + h)),
                  pl.BlockSpec((1, T, vw), lambda b, h, t: (b, t, h)),
                  pl.BlockSpec((1, T, vw), lambda b, h, t: (b, t, h)),
                  pl.BlockSpec((1, 1, T, 2 * hps), lambda b, h, t: (b, h, t, 0)),
                  pl.BlockSpec((1, 1, hps // 2, 2 * T), lambda b, h, t: (b, h, 0, t)),
                  pl.BlockSpec((1, HD), lambda b, h, t: (0, 0))],
        out_specs=pl.BlockSpec((1, T, vw), lambda b, h, t: (b, t, h)),
        out_shape=jax.ShapeDtypeStruct((B, S, v_w), BF16),
        scratch_shapes=[pltpu.VMEM((hps, HD, HD), F32)],
        compiler_params=_params("arbitrary", "arbitrary", "arbitrary"),
        name="delta_rule",
    )(qk3, qk3, v3, z3, bg4, gt4, norm_g.reshape(1, HD)).reshape(B * S, v_w)


def kernel(x, rel_bias, attn_w_in, attn_w_out, dn_w_in, dn_conv_w, dn_a_log, dn_dt_bias, dn_norm_g, dn_w_out,
           ln1_g, ln1_b, router_w, router_b, w_gate_up, b_gate_up, w_down, b_down, ln2_g, ln2_b):
    B, S, D = x.shape
    N = B * S
    E = router_w.shape[2]
    x2d = x.reshape(N, D)

    def vec(p):
        return p.reshape(1, -1)

    def router(i):
        return jnp.transpose(router_w[i]).astype(BF16), router_b[i].reshape(E, 1)

    w_gate_up_c = _chunk_major_bf16(w_gate_up, MOE_FF_CHUNK)
    w_down_c = _chunk_major_bf16(w_down, MOE_FF_CHUNK)

    def moe(i, post):
        return _moe_block(post, i, w_gate_up_c, b_gate_up, w_down_c, b_down, vec(ln2_g[i]), vec(ln2_b[i]))

    w_in = attn_w_in[0].astype(BF16)
    os_, lses = [], []
    for g, (window, dilation) in enumerate(DILATED_GROUPS):
        assert window // dilation == ATTN_BLOCK
        qkv = _attn_inproj(x2d, w_in, g, dilation, B, S)
        bias = _attn_bias(rel_bias[:, g * ATTN_HEADS:(g + 1) * ATTN_HEADS], dilation)
        o, lse = _dilated_group(qkv, bias, g)
        os_.append(o)
        lses.append(lse)
    rwt, rb = router(0)
    post = _attn_out(os_, lses, attn_w_out[0].astype(BF16), x2d, vec(ln1_g[0]), vec(ln1_b[0]), rwt, rb, S)
    x2, x2b = moe(0, post)

    conv_cols = dn_conv_w.shape[2]
    main_cols = conv_cols + DN_V_HEADS * HEAD_DIM
    w_in = dn_w_in[0].astype(BF16)
    q_w = DN_QK_HEADS * HEAD_DIM
    v_w = DN_V_HEADS * HEAD_DIM
    qk = _dn_inproj(x2b, w_in, dn_conv_w[0], S, "qk", 0, 2 * q_w)
    v = _dn_inproj(x2b, w_in, dn_conv_w[0], S, "v", 2 * q_w, v_w)
    z = _dn_inproj(x2b, w_in, dn_conv_w[0], S, "z", conv_cols, v_w)
    bg = _dn_gates(x2b, w_in[:, main_cols:], dn_a_log[0], dn_dt_bias[0])
    o = _delta_rule(qk, v, z, bg, dn_norm_g[0], B, S)
    rwt, rb = router(1)
    post = _dn_out(o, dn_w_out[0].astype(BF16), x2, vec(ln1_g[1]), vec(ln1_b[1]), rwt, rb)
    x3, _ = moe(1, post)
    return x3.reshape(B, S, D)
```

```python
import functools
import math

import jax
import jax.numpy as jnp
from jax import lax
from jax.experimental import pallas as pl
from jax.experimental.pallas import tpu as pltpu

F32 = jnp.float32
BF16 = jnp.bfloat16
I32 = jnp.int32
U32 = jnp.uint32

DEPTH = 2
DEEPNORM_ALPHA = (2.0 * DEPTH) ** 0.25
LN_EPS = 1e-5
HEAD_DIM = 128
ATTN_HEADS = 8
ATTN_WIDTH = ATTN_HEADS * HEAD_DIM
DILATED_GROUPS = ((128, 1), (512, 4), (2048, 16))
ATTN_BLOCK = 128
MAX_DISTANCE = 2048
DN_QK_HEADS = 16
DN_V_HEADS = 32
DN_CONV = 4
DN_CHUNK = 64
DN_EPS = 1e-6
TOP_K = 4
SWIGLU_LIMIT = 7.0
SWIGLU_ALPHA = 1.702

MOE_ROWS = 1024
MOE_HALF = 512
MOE_FF_CHUNK = 512
ROW_TILE = 256
DISPATCH_TILE = 512
VMEM_LIMIT = 58 * 1024 * 1024


def _params(*semantics):
    return pltpu.CompilerParams(dimension_semantics=semantics, vmem_limit_bytes=VMEM_LIMIT)


def _pack_bf16_pair(lo, hi):
    lo_bits = lax.bitcast_convert_type(lo.astype(BF16).astype(F32), U32)
    hi_bits = lax.bitcast_convert_type(hi.astype(BF16).astype(F32), U32)
    return (hi_bits & jnp.uint32(0xFFFF0000)) | (lo_bits >> 16)


def _unpack_bf16_pair(u):
    lo = lax.bitcast_convert_type(u << 16, F32)
    hi = lax.bitcast_convert_type(u & jnp.uint32(0xFFFF0000), F32)
    return lo, hi


def _attn_inproj_kernel(a_ref, w_ref, o_ref, acc_ref, *, d, tm):
    res = jnp.dot(a_ref[...].astype(BF16), w_ref[...], preferred_element_type=F32)
    if d == 1:
        o_ref[0, 0] = res.astype(o_ref.dtype)
        return
    lanes = acc_ref.shape[2]
    for cb in range(acc_ref.shape[0]):
        acc_ref[cb] = res[:, cb * lanes:(cb + 1) * lanes]
    for r in range(d):
        for cb in range(acc_ref.shape[0]):
            o_ref[0, r, :, cb * lanes:(cb + 1) * lanes] = (
                acc_ref[cb, pl.ds(r, tm // d, stride=d), :].astype(o_ref.dtype))


def _attn_inproj(x2d, w, g, d, B, S):
    K = x2d.shape[1]
    tm, tn = 512, 1536
    gw = 3 * ATTN_WIDTH
    nt = gw // tn
    mt = S // tm
    return pl.pallas_call(
        functools.partial(_attn_inproj_kernel, d=d, tm=tm),
        grid=(nt, B, mt),
        in_specs=[pl.BlockSpec((tm, K), lambda n, b, m: (b * mt + m, 0)),
                  pl.BlockSpec((K, tn), lambda n, b, m: (0, g * nt + n))],
        out_specs=pl.BlockSpec((1, d, tm // d, tn), lambda n, b, m: (b, 0, m, n)),
        out_shape=jax.ShapeDtypeStruct((B, d, S // d, gw), BF16),
        scratch_shapes=[pltpu.VMEM((tn // HEAD_DIM, tm, HEAD_DIM), F32)],
        compiler_params=_params("arbitrary", "arbitrary", "arbitrary"),
        name=f"attn_inproj_g{g}",
    )(x2d, w)


def _t5_bucket(dist, num_buckets):
    max_exact = num_buckets // 2
    d_f = jnp.maximum(dist, 1).astype(F32)
    large = max_exact + (jnp.log(d_f / max_exact) / math.log(MAX_DISTANCE / max_exact)
                         * (num_buckets - max_exact)).astype(I32)
    large = jnp.minimum(large, num_buckets - 1)
    return jnp.where(dist < max_exact, dist, large)


def _attn_bias(rel_bias_g, dilation):
    bq = ATTN_BLOCK
    a = jnp.arange(bq, dtype=I32)[:, None]
    c = jnp.arange(2 * bq, dtype=I32)[None, :]
    steps = a + bq - c
    valid = (steps >= 0) & (steps <= bq)
    bucket = _t5_bucket(jnp.maximum(steps, 0) * dilation, rel_bias_g.shape[0])
    bias = jnp.transpose(rel_bias_g[bucket], (2, 0, 1)).astype(F32)
    return jnp.where(valid[None], bias, -jnp.inf)


def _attn_kernel(q_ref, kp_ref, kc_ref, vp_ref, vc_ref, bias_ref, o_ref, lse_ref):
    bq = ATTN_BLOCK
    first = pl.program_id(2) == 0
    scale = HEAD_DIM ** -0.5
    nt = (((1,), (1,)), ((), ()))
    lse_ref[0, 0] = jnp.zeros(lse_ref.shape[2:], F32)
    heads = range(ATTN_HEADS)
    sls = [slice(h * HEAD_DIM, (h + 1) * HEAD_DIM) for h in heads]
    s_p = [lax.dot_general(q_ref[0, 0, :, sls[h]], kp_ref[0, 0, :, sls[h]], nt, preferred_element_type=F32)
           for h in heads]
    s_c = [lax.dot_general(q_ref[0, 0, :, sls[h]], kc_ref[0, 0, :, sls[h]], nt, preferred_element_type=F32)
           for h in heads]
    p_p, p_c = [], []
    for h in heads:
        sp = jnp.where(first, -jnp.inf, s_p[h] * scale + bias_ref[h, :, :bq])
        sc = s_c[h] * scale + bias_ref[h, :, bq:]
        m = jnp.maximum(jnp.max(sp, axis=-1, keepdims=True), jnp.max(sc, axis=-1, keepdims=True))
        e_p = jnp.exp(sp - m)
        e_c = jnp.exp(sc - m)
        l = jnp.sum(e_p, axis=-1, keepdims=True) + jnp.sum(e_c, axis=-1, keepdims=True)
        inv = 1.0 / l
        p_p.append((e_p * inv).astype(BF16))
        p_c.append((e_c * inv).astype(BF16))
        lse_ref[0, 0, :, h:h + 1] = m + jnp.log(l)
    for h in heads:
        o_ref[0, 0, :, sls[h]] = (jnp.dot(p_p[h], vp_ref[0, 0, :, sls[h]], preferred_element_type=F32)
                                  + jnp.dot(p_c[h], vc_ref[0, 0, :, sls[h]], preferred_element_type=F32))


def _dilated_group(qkv, bias, g):
    B, d, L, _ = qkv.shape
    nb = L // ATTN_BLOCK
    blk = (1, 1, ATTN_BLOCK, ATTN_WIDTH)
    lse_blk = (1, 1, ATTN_BLOCK, HEAD_DIM)

    def spec(kind, prev):
        if prev:
            return pl.BlockSpec(blk, lambda b, r, i: (b, r, jnp.maximum(i - 1, 0), kind))
        return pl.BlockSpec(blk, lambda b, r, i: (b, r, i, kind))

    return pl.pallas_call(
        _attn_kernel,
        grid=(B, d, nb),
        in_specs=[spec(0, False), spec(1, True), spec(1, False), spec(2, True), spec(2, False),
                  pl.BlockSpec(bias.shape, lambda b, r, i: (0, 0, 0))],
        out_specs=[pl.BlockSpec(blk, lambda b, r, i: (b, r, i, 0)),
                   pl.BlockSpec(lse_blk, lambda b, r, i: (b, r, i, 0))],
        out_shape=[jax.ShapeDtypeStruct((B, d, L, ATTN_WIDTH), F32),
                   jax.ShapeDtypeStruct((B, d, L, HEAD_DIM), F32)],
        compiler_params=_params("arbitrary", "arbitrary", "arbitrary"),
        name=f"dilated_attn_g{g}",
    )(qkv, qkv, qkv, qkv, qkv, bias)


def _layer_norm(y, g_ref, b_ref):
    mu = jnp.mean(y, axis=-1, keepdims=True)
    yc = y - mu
    var = jnp.mean(yc * yc, axis=-1, keepdims=True)
    return yc * lax.rsqrt(var + LN_EPS) * g_ref[...] + b_ref[...]


def _residual_ln_router(h, x_ref, g_ref, b_ref, rwt_ref, rb_ref,
                        x1_ref, x1p_ref, topi_ref, gate_ref, rank_ref, cnt_ref, carry_ref, first_step):
    xn = _layer_norm(DEEPNORM_ALPHA * x_ref[...] + h, g_ref, b_ref)
    x1_ref[...] = xn
    half = xn.shape[1] // 2
    x1p_ref[...] = _pack_bf16_pair(xn[:, :half], xn[:, half:])
    logits = lax.dot_general(rwt_ref[...], xn.astype(BF16), (((1,), (1,)), ((), ())),
                             preferred_element_type=F32) + rb_ref[...]
    n_exp, tm = logits.shape
    eidx = lax.broadcasted_iota(I32, logits.shape, 0)
    vals, hots = [], []
    for k in range(TOP_K):
        mx = jnp.max(logits, axis=0, keepdims=True)
        idx = jnp.min(jnp.where(logits == mx, eidx, n_exp), axis=0, keepdims=True)
        topi_ref[k:k + 1, :] = idx
        vals.append(mx)
        hit = eidx == idx
        hots.append(hit.astype(F32))
        logits = jnp.where(hit, -jnp.inf, logits)
    exps = [jnp.exp(v - vals[0]) for v in vals]
    denom = exps[0] + exps[1] + exps[2] + exps[3]
    for k in range(TOP_K):
        gate_ref[k:k + 1, :] = exps[k] / denom

    @pl.when(first_step)
    def _():
        carry_ref[...] = jnp.zeros_like(carry_ref)

    tri = (lax.broadcasted_iota(I32, (tm, tm), 0) <= lax.broadcasted_iota(I32, (tm, tm), 1)).astype(BF16)
    cum = jnp.dot(jnp.concatenate(hots, axis=0).astype(BF16), tri, preferred_element_type=F32)
    base = carry_ref[...]
    for k in range(TOP_K):
        cum_k = cum[k * n_exp:(k + 1) * n_exp]
        before = cum_k - hots[k] + base
        rank_ref[k:k + 1, :] = jnp.sum(hots[k] * before, axis=0, keepdims=True).astype(I32)
        base = base + cum_k[:, tm - 1:tm]
    carry_ref[...] = base
    cnt_ref[...] = jnp.broadcast_to(base, cnt_ref.shape)


def _attn_out_kernel(o0_ref, o1_ref, o2_ref, l0_ref, l1_ref, l2_ref, w_ref, x_ref, g_ref, b_ref, rwt_ref, rb_ref,
                     x1_ref, x1p_ref, topi_ref, gate_ref, rank_ref, cnt_ref,
                     o_scr, l_scr, merged_ref, carry_ref, *, tm):
    o_refs = (o0_ref, o1_ref, o2_ref)
    l_refs = (l0_ref, l1_ref, l2_ref)
    H = ATTN_HEADS
    for g, (_, d) in enumerate(DILATED_GROUPS):
        for r in range(d):
            for h in range(H):
                o_scr[g * H + h, pl.ds(r, tm // d, stride=d), :] = o_refs[g][0, r, :, h * HEAD_DIM:(h + 1) * HEAD_DIM]
            l_scr[g, pl.ds(r, tm // d, stride=d), :] = l_refs[g][0, r]
    l0, l1, l2 = l_scr[0], l_scr[1], l_scr[2]
    m = jnp.maximum(jnp.maximum(l0, l1), l2)
    e = [jnp.exp(l0 - m), jnp.exp(l1 - m), jnp.exp(l2 - m)]
    den = e[0] + e[1] + e[2]
    wts = [ei / den for ei in e]
    for h in range(H):
        o = wts[0][:, h:h + 1] * o_scr[h]
        o = o + wts[1][:, h:h + 1] * o_scr[H + h]
        o = o + wts[2][:, h:h + 1] * o_scr[2 * H + h]
        merged_ref[:, h * HEAD_DIM:(h + 1) * HEAD_DIM] = o.astype(BF16)
    h_out = jnp.dot(merged_ref[...], w_ref[...], preferred_element_type=F32)
    _residual_ln_router(h_out, x_ref, g_ref, b_ref, rwt_ref, rb_ref, x1_ref, x1p_ref, topi_ref, gate_ref,
                        rank_ref, cnt_ref, carry_ref, pl.program_id(0) == 0)


def _dn_out_kernel(o_ref, w_ref, x_ref, g_ref, b_ref, rwt_ref, rb_ref,
                   x1_ref, x1p_ref, topi_ref, gate_ref, rank_ref, cnt_ref, carry_ref):
    h = jnp.dot(o_ref[...], w_ref[...], preferred_element_type=F32)
    _residual_ln_router(h, x_ref, g_ref, b_ref, rwt_ref, rb_ref, x1_ref, x1p_ref, topi_ref,
                        gate_ref, rank_ref, cnt_ref, carry_ref, pl.program_id(0) == 0)


def _post_specs(tm, D, E, idx):
    row = lambda *a: (idx(*a), 0)
    const = lambda *a: (0, 0)
    col = lambda *a: (0, idx(*a))
    ins = [pl.BlockSpec((tm, D), row),
           pl.BlockSpec((1, D), const), pl.BlockSpec((1, D), const),
           pl.BlockSpec((E, D), const), pl.BlockSpec((E, 1), const)]
    outs = [pl.BlockSpec((tm, D), row), pl.BlockSpec((tm, D // 2), row),
            pl.BlockSpec((TOP_K, tm), col), pl.BlockSpec((TOP_K, tm), col), pl.BlockSpec((TOP_K, tm), col),
            pl.BlockSpec((E, 128), const)]
    return ins, outs


def _post_shapes(N, D, E):
    return [jax.ShapeDtypeStruct((N, D), F32), jax.ShapeDtypeStruct((N, D // 2), U32),
            jax.ShapeDtypeStruct((TOP_K, N), I32), jax.ShapeDtypeStruct((TOP_K, N), F32),
            jax.ShapeDtypeStruct((TOP_K, N), I32), jax.ShapeDtypeStruct((E, 128), F32)]


def _attn_out(os_, lses, w_out, x2d, ln_g, ln_b, rwt, rb, S):
    N, D = x2d.shape
    E = rwt.shape[0]
    tm = 256
    mt = S // tm
    ins, outs = _post_specs(tm, D, E, lambda m: m)
    grp_specs = []
    for width in (ATTN_WIDTH, HEAD_DIM):
        for _, d in DILATED_GROUPS:
            grp_specs.append(pl.BlockSpec((1, d, tm // d, width), lambda m: (m // mt, 0, m % mt, 0)))
    return pl.pallas_call(
        functools.partial(_attn_out_kernel, tm=tm),
        grid=(N // tm,),
        in_specs=grp_specs + [pl.BlockSpec(w_out.shape, lambda m: (0, 0))] + ins,
        out_specs=outs,
        out_shape=_post_shapes(N, D, E),
        scratch_shapes=[pltpu.VMEM((len(DILATED_GROUPS) * ATTN_HEADS, tm, HEAD_DIM), F32),
                        pltpu.VMEM((len(DILATED_GROUPS), tm, HEAD_DIM), F32),
                        pltpu.VMEM((tm, ATTN_WIDTH), BF16),
                        pltpu.VMEM((E, 1), F32)],
        compiler_params=_params("arbitrary"),
        name="attn_out_ln_router",
    )(*os_, *lses, w_out, x2d, ln_g, ln_b, rwt, rb)


def _dn_out(o, w_out, x2d, ln_g, ln_b, rwt, rb):
    N, D = x2d.shape
    E = rwt.shape[0]
    Kin = o.shape[1]
    tm = 256
    ins, outs = _post_specs(tm, D, E, lambda m: m)
    return pl.pallas_call(
        _dn_out_kernel,
        grid=(N // tm,),
        in_specs=[pl.BlockSpec((tm, Kin), lambda m: (m, 0)),
                  pl.BlockSpec((Kin, D), lambda m: (0, 0), pipeline_mode=pl.Buffered(1))] + ins,
        out_specs=outs,
        out_shape=_post_shapes(N, D, E),
        scratch_shapes=[pltpu.VMEM((E, 1), F32)],
        compiler_params=_params("arbitrary"),
        name="dn_out_ln_router",
    )(o, w_out, x2d, ln_g, ln_b, rwt, rb)


def _slot_layout(topi, rank, counts, rows):
    K, N = topi.shape
    n_exp = counts.shape[0]
    padded = (counts + rows - 1) // rows * rows
    padded_end = jnp.cumsum(padded)
    padded_start = padded_end - padded
    sel = topi[None] == jnp.arange(n_exp, dtype=I32)[:, None, None]
    pos = jnp.sum(jnp.where(sel, padded_start[:, None, None], 0), axis=0) + rank
    n_blocks = -(-(K * N + n_exp * (rows - 1)) // rows)
    n_used = padded_end[-1] // rows
    blk = jnp.arange(n_blocks, dtype=I32)
    be = jnp.sum((blk[:, None] * rows >= padded_end[None, :]).astype(I32), axis=1)
    be = jnp.minimum(be, n_exp - 1)
    be_last = jnp.sum(jnp.where(blk == jnp.maximum(n_used - 1, 0), be, 0))
    used = blk < n_used
    be = jnp.where(used, be, be_last)
    start_b = jnp.sum(jnp.where(be[:, None] == jnp.arange(n_exp)[None, :], padded_start[None, :], 0), axis=1)
    cnt_e = jnp.sum(jnp.where(be[:, None] == jnp.arange(n_exp)[None, :], counts[None, :], 0), axis=1)
    valid = jnp.where(used, jnp.clip(cnt_e - (blk * rows - start_b), 0, rows), 0).astype(I32)
    n_slots = jnp.full((1,), n_blocks * rows, I32)
    pad_lo = jnp.concatenate([padded_start + counts, padded_end[-1:]]).astype(I32)
    pad_hi = jnp.concatenate([padded_end, n_slots]).astype(I32)
    return pos.astype(I32), be.astype(I32), valid, pad_lo, pad_hi, n_blocks


def _tile_major(pos, tm):
    K, N = pos.shape
    return jnp.transpose(pos.reshape(K, N // tm, tm), (1, 0, 2)).reshape(N // tm, K * tm)


def _dispatch_kernel(lo_ref, hi_ref, pos_hbm, x_ref, xs_hbm, idx_smem, zbuf, sem_idx, sem_rows, sem_zero, *, tm):
    i = pl.program_id(0)

    @pl.when(i == 0)
    def _():
        zbuf[...] = jnp.zeros_like(zbuf)
        one = zbuf.at[pl.ds(0, 1)]

        def per_expert(e, carry):
            lo = lo_ref[e]
            hi = hi_ref[e]
            lo_al = jnp.minimum((lo + 7) // 8 * 8, hi)

            def row_start(r, c):
                pltpu.make_async_copy(one, xs_hbm.at[pl.ds(r, 1)], sem_zero).start()
                return c

            def grp_start(q, c):
                pltpu.make_async_copy(zbuf, xs_hbm.at[pl.ds(pl.multiple_of(q * 8, 8), 8)], sem_zero).start()
                return c

            def row_wait(r, c):
                pltpu.make_async_copy(one, xs_hbm.at[pl.ds(0, 1)], sem_zero).wait()
                return c

            def grp_wait(q, c):
                pltpu.make_async_copy(zbuf, xs_hbm.at[pl.ds(0, 8)], sem_zero).wait()
                return c

            lax.fori_loop(lo, lo_al, row_start, 0)
            lax.fori_loop(lo_al // 8, hi // 8, grp_start, 0)
            lax.fori_loop(lo, lo_al, row_wait, 0)
            lax.fori_loop(lo_al // 8, hi // 8, grp_wait, 0)
            return carry

        lax.fori_loop(0, lo_ref.shape[0], per_expert, 0)

    cp = pltpu.make_async_copy(pos_hbm.at[i], idx_smem, sem_idx)
    cp.start()
    cp.wait()

    def issue(jj, carry):
        for u in range(8):
            j = jj * 8 + u
            for k in range(TOP_K):
                dst = idx_smem[k * tm + j]
                pltpu.make_async_copy(x_ref.at[pl.ds(j, 1)], xs_hbm.at[pl.ds(dst, 1)], sem_rows).start()
        return carry

    lax.fori_loop(0, tm // 8, issue, 0)
    for k in range(TOP_K):
        pltpu.make_async_copy(x_ref, xs_hbm.at[pl.ds(0, tm)], sem_rows).wait()


def _dispatch(x1p, pos_tiles, pad_lo, pad_hi, n_slots):
    N, W = x1p.shape
    tm = DISPATCH_TILE
    grid_spec = pltpu.PrefetchScalarGridSpec(
        num_scalar_prefetch=2,
        grid=(N // tm,),
        in_specs=[pl.BlockSpec(memory_space=pl.ANY),
                  pl.BlockSpec((tm, W), lambda i, lo, hi: (i, 0))],
        out_specs=pl.BlockSpec(memory_space=pl.ANY),
        scratch_shapes=[pltpu.SMEM((TOP_K * tm,), I32), pltpu.VMEM((8, W), U32),
                        pltpu.SemaphoreType.DMA, pltpu.SemaphoreType.DMA, pltpu.SemaphoreType.DMA],
    )
    return pl.pallas_call(
        functools.partial(_dispatch_kernel, tm=tm),
        grid_spec=grid_spec,
        out_shape=jax.ShapeDtypeStruct((n_slots, W), U32),
        compiler_params=_params("arbitrary"),
        name="moe_dispatch",
    )(pad_lo, pad_hi, pos_tiles, x1p)


def _ffn_kernel(be_ref, cnt_ref, xs_ref, wg_ref, wl_ref, bgu_ref, wd_ref, bd_ref, o_ref, xb_ref, act_ref,
                *, nfc, ndc, layer, n_exp):
    b = pl.program_id(0)
    p = pl.program_id(1)
    cnt = cnt_ref[b]
    rows = xb_ref.shape[0]
    half = xs_ref.shape[1]
    fc = o_ref.shape[1]
    n_half = rows // MOE_HALF
    up = p < nfc
    expert = layer * n_exp + be_ref[b]
    bg_row = expert * (2 * nfc) + jnp.minimum(p, nfc - 1)
    bd_row = expert * ndc + jnp.maximum(p - nfc, 0)

    @pl.when(jnp.logical_and(cnt > 0, p == 0))
    def _():
        lo, hi = _unpack_bf16_pair(xs_ref[...])
        xb_ref[:, :half] = lo.astype(BF16)
        xb_ref[:, half:] = hi.astype(BF16)

    def up_phase(live):
        wg = wg_ref[0, 0].astype(BF16)
        wl = wl_ref[0, 0].astype(BF16)
        bg = bgu_ref[pl.ds(bg_row, 1), :]
        bl = bgu_ref[pl.ds(bg_row + nfc, 1), :]
        acts = []
        for h in range(live):
            x = xb_ref[h * MOE_HALF:(h + 1) * MOE_HALF, :]
            hg = jnp.dot(x, wg, preferred_element_type=F32) + bg
            hl = jnp.dot(x, wl, preferred_element_type=F32) + bl
            glu = jnp.minimum(hg, SWIGLU_LIMIT)
            lin = jnp.clip(hl, -SWIGLU_LIMIT, SWIGLU_LIMIT)
            acts.append((glu * jax.nn.sigmoid(SWIGLU_ALPHA * glu) * (lin + 1.0)).astype(BF16))
        for c in range(nfc):
            @pl.when(p == c)
            def _():
                for h in range(live):
                    act_ref[h * MOE_HALF:(h + 1) * MOE_HALF, c * fc:(c + 1) * fc] = acts[h]

    def down_phase(live):
        wd = wd_ref[0, 0].astype(BF16) if live else None
        bd = bd_ref[pl.ds(bd_row, 1), :]
        for h in range(live):
            rs = slice(h * MOE_HALF, (h + 1) * MOE_HALF)
            o_ref[rs, :] = jnp.dot(act_ref[rs, :], wd, preferred_element_type=F32) + bd
        for h in range(live, n_half):
            rs = slice(h * MOE_HALF, (h + 1) * MOE_HALF)
            o_ref[rs, :] = jnp.zeros((MOE_HALF, fc), F32)

    for live in range(1, n_half + 1):
        in_range = cnt > (live - 1) * MOE_HALF
        if live < n_half:
            in_range = jnp.logical_and(in_range, cnt <= live * MOE_HALF)

        @pl.when(jnp.logical_and(in_range, up))
        def _():
            up_phase(live)

        @pl.when(jnp.logical_and(in_range, jnp.logical_not(up)))
        def _():
            down_phase(live)

    @pl.when(jnp.logical_and(cnt == 0, jnp.logical_not(up)))
    def _():
        down_phase(0)


def _expert_ffn(xs, be, valid, layer, w_gate_up, b_gate_up, w_down, b_down):
    n_slots, half = xs.shape
    D = 2 * half
    L, E, _, F2 = w_gate_up.shape
    F = F2 // 2
    rows, fc = MOE_ROWS, MOE_FF_CHUNK
    assert F % fc == 0 and D % fc == 0
    nfc = F // fc
    ndc = D // fc
    n_blocks = n_slots // rows

    def up_chunk(b, p, cnt):
        return jnp.where(cnt[b] > 0, jnp.minimum(p, nfc - 1), nfc - 1)

    def down_chunk(b, p, cnt):
        return jnp.where(cnt[b] > 0, jnp.maximum(p - nfc, 0), ndc - 1)

    def weight(height, col):
        return pl.BlockSpec((1, 1, height, fc), lambda b, p, be, cnt: (layer, be[b], 0, col(b, p, cnt)))

    def table(n_rows):
        return pl.BlockSpec((n_rows, fc), lambda b, p, be, cnt: (0, 0))

    def lin_chunk(b, p, cnt):
        return nfc + up_chunk(b, p, cnt)

    grid_spec = pltpu.PrefetchScalarGridSpec(
        num_scalar_prefetch=2,
        grid=(n_blocks, nfc + ndc),
        in_specs=[pl.BlockSpec((rows, half), lambda b, p, be, cnt: (b, 0)),
                  weight(D, up_chunk), weight(D, lin_chunk), table(L * E * 2 * nfc),
                  weight(F, down_chunk), table(L * E * ndc)],
        out_specs=pl.BlockSpec((rows, fc), lambda b, p, be, cnt: (b, jnp.maximum(p - nfc, 0))),
        scratch_shapes=[pltpu.VMEM((rows, D), BF16), pltpu.VMEM((rows, F), BF16)],
    )
    return pl.pallas_call(
        functools.partial(_ffn_kernel, nfc=nfc, ndc=ndc, layer=layer, n_exp=E),
        grid_spec=grid_spec,
        out_shape=jax.ShapeDtypeStruct((n_slots, D), F32),
        compiler_params=_params("arbitrary", "arbitrary"),
        name="expert_ffn",
    )(be, valid, xs, w_gate_up, w_gate_up, b_gate_up.reshape(L * E * 2 * nfc, fc),
      w_down, b_down.reshape(L * E * ndc, fc))


def _combine_kernel(pos_hbm, ys_hbm, gate_ref, x_ref, g_ref, b_ref, x2_ref, x2b_ref,
                    idx0_smem, idx1_smem, buf, sem_idx, sem_rows, *, tm):
    i = pl.program_id(0)
    n_tiles = 2 * pl.num_programs(0)
    idx_smem = (idx0_smem, idx1_smem)
    D = x_ref.shape[1]

    def idx_copy(tile, slot):
        return pltpu.make_async_copy(pos_hbm.at[tile], idx_smem[slot], sem_idx.at[slot])

    def gather(tile, slot):
        idx_copy(tile, slot).wait()

        def issue(jj, carry):
            for u in range(8):
                for k in range(TOP_K):
                    src = idx_smem[slot][k * tm + jj * 8 + u]
                    pltpu.make_async_copy(ys_hbm.at[src >> 3, pl.ds(src & 7, 1)], buf.at[slot, k, jj, pl.ds(u, 1)],
                                          sem_rows.at[slot]).start()
            return carry

        lax.fori_loop(0, tm // 8, issue, 0)

        @pl.when(tile + 2 < n_tiles)
        def _():
            idx_copy(tile + 2, slot).start()

    def reduce(slot):
        rs = slice(slot * tm, (slot + 1) * tm)
        for k in range(TOP_K):
            pltpu.make_async_copy(ys_hbm.at[pl.ds(0, tm // 8)], buf.at[slot, k], sem_rows.at[slot]).wait()
        gate = gate_ref[rs, :]
        f = buf[slot, 0].reshape(tm, D) * gate[:, 0:1]
        for k in range(1, TOP_K):
            f = f + buf[slot, k].reshape(tm, D) * gate[:, k:k + 1]
        xn = _layer_norm(DEEPNORM_ALPHA * x_ref[rs, :] + f, g_ref, b_ref)
        x2_ref[rs, :] = xn
        x2b_ref[rs, :] = xn.astype(BF16)

    @pl.when(i == 0)
    def _():
        idx_copy(0, 0).start()
        idx_copy(1, 1).start()
        gather(0, 0)

    gather(2 * i + 1, 1)
    reduce(0)

    @pl.when(2 * i + 2 < n_tiles)
    def _():
        gather(2 * i + 2, 0)

    reduce(1)


def _combine_ln(ys, pos_tiles, gates_t, x1, ln_g, ln_b):
    N, D = x1.shape
    tm = ROW_TILE
    row = pl.BlockSpec((2 * tm, D), lambda m: (m, 0))
    vec = pl.BlockSpec((1, D), lambda m: (0, 0))
    anyspec = pl.BlockSpec(memory_space=pl.ANY)
    return pl.pallas_call(
        functools.partial(_combine_kernel, tm=tm),
        grid=(N // (2 * tm),),
        in_specs=[anyspec, anyspec, pl.BlockSpec((2 * tm, TOP_K), lambda m: (m, 0)), row, vec, vec],
        out_specs=[row, row],
        out_shape=[jax.ShapeDtypeStruct((N, D), F32), jax.ShapeDtypeStruct((N, D), BF16)],
        scratch_shapes=[pltpu.SMEM((TOP_K * tm,), I32), pltpu.SMEM((TOP_K * tm,), I32),
                        pltpu.VMEM((2, TOP_K, tm // 8, 8, D), F32),
                        pltpu.SemaphoreType.DMA((2,)), pltpu.SemaphoreType.DMA((2,))],
        compiler_params=_params("arbitrary"),
        name="moe_combine_ln",
    )(pos_tiles, ys.reshape(ys.shape[0] // 8, 8, D), gates_t, x1, ln_g, ln_b)


def _moe_block(post, layer, w_gate_up, b_gate_up, w_down, b_down, ln_g, ln_b):
    x1, x1p, topi, gates, rank, cnt = post
    counts = cnt[:, 0].astype(I32)
    pos, be, valid, pad_lo, pad_hi, n_blocks = _slot_layout(topi, rank, counts, MOE_ROWS)
    xs = _dispatch(x1p, _tile_major(pos, DISPATCH_TILE), pad_lo, pad_hi, n_blocks * MOE_ROWS)
    ys = _expert_ffn(xs, be, valid, layer, w_gate_up, b_gate_up, w_down, b_down)
    return _combine_ln(ys, _tile_major(pos, ROW_TILE), jnp.transpose(gates), x1, ln_g, ln_b)


DN_COL_SUB = 256


def _dn_inproj_kernel(x_ref, w_ref, cw_ref, o_ref, pbuf, *, mode, tm, tn, seq_tiles, q_tiles):
    n = pl.program_id(0)
    m = pl.program_id(1)
    if mode != "z":
        @pl.when(m % seq_tiles == 0)
        def _():
            pbuf[0:8, :] = jnp.zeros((8, tn), F32)

    x = x_ref[...]
    sc = jnp.where(n < q_tiles, HEAD_DIM ** -0.5, 1.0).astype(F32)
    for cs in range(tn // DN_COL_SUB):
        cols = slice(cs * DN_COL_SUB, (cs + 1) * DN_COL_SUB)
        p = jnp.dot(x, w_ref[:, cols], preferred_element_type=F32)
        if mode == "z":
            o_ref[:, cols] = p
            continue
        pbuf[8:tm + 8, cols] = p
        cw = cw_ref[:, cols]
        y = cw[0:1] * pbuf[5:tm + 5, cols]
        y = y + cw[1:2] * pbuf[6:tm + 6, cols]
        y = y + cw[2:3] * pbuf[7:tm + 7, cols]
        y = y + cw[3:4] * p
        pbuf[0:8, cols] = pbuf[tm:tm + 8, cols]
        y = y * jax.nn.sigmoid(y)
        if mode == "v":
            o_ref[:, cols] = y
            continue
        for hh in range(DN_COL_SUB // HEAD_DIM):
            sl = slice(hh * HEAD_DIM, (hh + 1) * HEAD_DIM)
            ys = y[:, sl]
            ss = jnp.sum(ys * ys, axis=-1, keepdims=True)
            o_ref[:, cs * DN_COL_SUB + hh * HEAD_DIM:cs * DN_COL_SUB + (hh + 1) * HEAD_DIM] = (
                ys * lax.rsqrt(ss + DN_EPS) * sc)


def _dn_inproj(xb, w, conv_w, seq_len, mode, col0, width):
    N, K = xb.shape
    tm, tn = 512, 1024
    t0 = col0 // tn
    q_w = DN_QK_HEADS * HEAD_DIM
    conv_tiles = conv_w.shape[1] // tn
    kern = functools.partial(_dn_inproj_kernel, mode=mode, tm=tm, tn=tn, seq_tiles=seq_len // tm,
                             q_tiles=q_w // tn if mode == "qk" else 0)
    return pl.pallas_call(
        kern,
        grid=(width // tn, N // tm),
        in_specs=[pl.BlockSpec((tm, K), lambda n, m: (m, 0)),
                  pl.BlockSpec((K, tn), lambda n, m: (0, t0 + n)),
                  pl.BlockSpec((DN_CONV, tn), lambda n, m: (0, jnp.minimum(t0 + n, conv_tiles - 1)))],
        out_specs=pl.BlockSpec((tm, tn), lambda n, m: (m, n)),
        out_shape=jax.ShapeDtypeStruct((N, width), F32),
        scratch_shapes=[pltpu.VMEM((tm + 8, tn), F32)],
        compiler_params=_params("arbitrary", "arbitrary"),
        name=f"dn_inproj_{mode}",
    )(xb, w, conv_w)


def _dn_gates_kernel(x_ref, w_ref, alog_ref, dtb_ref, o_ref):
    r = jnp.dot(x_ref[...], w_ref[...], preferred_element_type=F32)
    lane = lax.broadcasted_iota(I32, r.shape, 1)
    row = lax.broadcasted_iota(I32, r.shape, 0)
    beta = jax.nn.sigmoid(r)
    z = r + dtb_ref[...]
    softplus = jnp.maximum(z, 0.0) + jnp.log1p(jnp.exp(-jnp.abs(z)))
    g = -jnp.exp(alog_ref[...]) * softplus
    in_chunk = row % DN_CHUNK
    s = 1
    while s < DN_CHUNK:
        g = g + jnp.where(in_chunk >= s, pltpu.roll(g, s, axis=0), 0.0)
        s *= 2
    o_ref[...] = jnp.where(lane < DN_V_HEADS, beta, g)


def _dn_gates(xb, w_ba, a_log, dt_bias):
    N, K = xb.shape
    H = DN_V_HEADS
    tm = 512
    lanes = 128
    w_pad = jnp.zeros((K, lanes), BF16).at[:, :2 * H].set(w_ba)
    alog_pad = jnp.zeros((1, lanes), F32).at[0, H:2 * H].set(a_log)
    dtb_pad = jnp.zeros((1, lanes), F32).at[0, H:2 * H].set(dt_bias)
    vec = pl.BlockSpec((1, lanes), lambda m: (0, 0))
    return pl.pallas_call(
        _dn_gates_kernel,
        grid=(N // tm,),
        in_specs=[pl.BlockSpec((tm, K), lambda m: (m, 0)), pl.BlockSpec((K, lanes), lambda m: (0, 0)), vec, vec],
        out_specs=pl.BlockSpec((tm, lanes), lambda m: (m, 0)),
        out_shape=jax.ShapeDtypeStruct((N, lanes), F32),
        compiler_params=_params("arbitrary"),
        name="dn_gates",
    )(xb, w_pad, alog_pad, dtb_pad)


DN_HEADS_PER_STEP = 8
DN_CHUNKS_PER_STEP = 4


def _delta_kernel(q_ref, k_ref, v_ref, z_ref, bg_ref, gtp_ref, ng_ref, o_ref, s_ref):
    C = DN_CHUNK
    HD = HEAD_DIM
    nh, nc = DN_HEADS_PER_STEP, DN_CHUNKS_PER_STEP
    npair = nh // 2
    ngrp = nh // 4
    nt = (((1,), (1,)), ((), ()))

    @pl.when(pl.program_id(2) == 0)
    def _():
        s_ref[...] = jnp.zeros_like(s_ref)

    ri2 = lax.broadcasted_iota(I32, (C, 2 * C), 0)
    li2 = lax.broadcasted_iota(I32, (C, 2 * C), 1)
    left = li2 < C
    ci2 = li2 % C
    incl2 = ri2 >= ci2
    strict2 = ri2 > ci2
    left_row = lax.broadcasted_iota(I32, (1, 2 * C), 1) < C
    ri4 = lax.broadcasted_iota(I32, (C, 4 * C), 0)
    li4 = lax.broadcasted_iota(I32, (C, 4 * C), 1)
    eye4 = (ri4 == li4 % C).astype(F32)
    blk4 = (lax.broadcasted_iota(I32, (4 * C, 4 * C), 0) // C) == (lax.broadcasted_iota(I32, (4 * C, 4 * C), 1) // C)
    zero_hd = jnp.zeros((C, HD), BF16)
    zero_s = jnp.zeros((HD, HD), BF16)

    def block_diag4(x):
        return jnp.where(blk4, jnp.concatenate([x, x, x, x], axis=0), 0.0).astype(BF16)

    b2, lhs2, rhs, qg, glast, bcat, qcat = {}, {}, {}, {}, {}, {}, {}
    for c in range(nc):
        rows = slice(c * C, (c + 1) * C)
        for pj in range(npair):
            csl = slice(pj * HD, (pj + 1) * HD)
            q = q_ref[0, rows, csl]
            k = k_ref[0, rows, csl]
            kk = jnp.concatenate([k, k], axis=0)
            m1 = lax.dot_general(jnp.concatenate([k, q], axis=0).astype(BF16), kk.astype(BF16), nt,
                                 preferred_element_type=F32)
            ja, jb = 2 * pj, 2 * pj + 1
            bcol = [bg_ref[0, 0, rows, j:j + 1] for j in (ja, jb)]
            gcol = [bg_ref[0, 0, rows, nh + j:nh + j + 1] for j in (ja, jb)]
            grow2 = gtp_ref[0, 0, pj:pj + 1, c * 2 * C:(c + 1) * 2 * C]
            gcol2 = jnp.where(left, gcol[0], gcol[1])
            bcol2 = jnp.where(left, bcol[0], bcol[1])
            decay2 = jnp.exp(jnp.where(incl2, gcol2 - grow2, -jnp.inf))
            b2[c, pj] = jnp.where(strict2, -(m1[:C] * bcol2 * decay2), 0.0)
            qk2 = (m1[C:] * decay2).astype(BF16)
            gl = [g[C - 1:C, :] for g in gcol]
            glast2 = jnp.where(left_row, gl[0], gl[1])
            kdec_t2 = (kk.T * jnp.exp(glast2 - grow2)).astype(BF16)
            lhs2[c, pj] = jnp.concatenate([qk2, kdec_t2], axis=0)
            for i, j in enumerate((ja, jb)):
                eg = jnp.exp(gcol[i])
                v = v_ref[0, rows, j * HD:(j + 1) * HD]
                kb = k * bcol[i]
                rhs[c, j] = jnp.concatenate([v * bcol[i], kb * eg], axis=1).astype(BF16)
                qg[c, j] = (q * eg).astype(BF16)
                glast[c, j] = gl[i]

    groups = [(c, g) for c in range(nc) for g in range(ngrp)]
    for c, g in groups:
        bcat[c, g] = jnp.concatenate([b2[c, 2 * g], b2[c, 2 * g + 1]], axis=1)
        qcat[c, g] = eye4 + bcat[c, g]
    for it in groups:
        bcat[it] = jnp.dot(bcat[it].astype(BF16), block_diag4(bcat[it]), preferred_element_type=F32)
    for _ in range(4):
        for it in groups:
            r = jnp.dot(jnp.concatenate([qcat[it], bcat[it]], axis=0).astype(BF16), block_diag4(bcat[it]),
                        preferred_element_type=F32)
            qcat[it] = qcat[it] + r[:C]
            bcat[it] = r[C:]
    uw = {}
    for c, g in groups:
        it = (c, g)
        t_inv = qcat[it] + jnp.dot(qcat[it].astype(BF16), block_diag4(bcat[it]), preferred_element_type=F32)
        rstack = jnp.concatenate([rhs[c, 4 * g + i] for i in range(4)], axis=0)
        r = jnp.dot(block_diag4(t_inv), rstack, preferred_element_type=F32)
        for i in range(4):
            uw[c, 4 * g + i] = r[i * C:(i + 1) * C]

    states = [s_ref[j] for j in range(nh)]
    ng = ng_ref[...]
    for c in range(nc):
        rows = slice(c * C, (c + 1) * C)
        r1 = {}
        for pj in range(npair):
            ja, jb = 2 * pj, 2 * pj + 1
            lhs = jnp.concatenate(
                [jnp.concatenate([uw[c, j][:, HD:].astype(BF16), qg[c, j]], axis=0) for j in (ja, jb)], axis=1)
            sa, sb = states[ja].astype(BF16), states[jb].astype(BF16)
            sbd = jnp.concatenate([jnp.concatenate([sa, zero_s], axis=1),
                                   jnp.concatenate([zero_s, sb], axis=1)], axis=0)
            r1[pj] = jnp.dot(lhs, sbd, preferred_element_type=F32)
        r2 = {}
        for pj in range(npair):
            ja, jb = 2 * pj, 2 * pj + 1
            va = (uw[c, ja][:, :HD] - r1[pj][:C, :HD]).astype(BF16)
            vb = (uw[c, jb][:, :HD] - r1[pj][:C, HD:]).astype(BF16)
            vbd = jnp.concatenate([jnp.concatenate([va, zero_hd], axis=1),
                                   jnp.concatenate([zero_hd, vb], axis=1)], axis=0)
            r2[pj] = jnp.dot(lhs2[c, pj], vbd, preferred_element_type=F32)
        for pj in range(npair):
            for i, j in enumerate((2 * pj, 2 * pj + 1)):
                hs = slice(i * HD, (i + 1) * HD)
                o = r1[pj][C:, hs] + r2[pj][:C, hs]
                states[j] = states[j] * jnp.exp(glast[c, j]) + r2[pj][C:, hs]
                o = o * lax.rsqrt(jnp.mean(o * o, axis=-1, keepdims=True) + DN_EPS) * ng
                zz = z_ref[0, rows, j * HD:(j + 1) * HD]
                o_ref[0, rows, j * HD:(j + 1) * HD] = (o * (zz * jax.nn.sigmoid(zz))).astype(o_ref.dtype)

    for j in range(nh):
        s_ref[j] = states[j]


def _delta_rule(qk, v, z, bg, norm_g, B, S):
    HD = HEAD_DIM
    C = DN_CHUNK
    hps = DN_HEADS_PER_STEP
    T = DN_CHUNKS_PER_STEP * C
    q_w = DN_QK_HEADS * HD
    v_w = DN_V_HEADS * HD
    qkw = hps // 2 * HD
    vw = hps * HD
    n_hg = DN_V_HEADS // hps
    qk3 = qk.reshape(B, S, 2 * q_w)
    v3 = v.reshape(B, S, v_w)
    z3 = z.reshape(B, S, v_w)
    beta = jnp.transpose(bg[:, :DN_V_HEADS].reshape(B, S, n_hg, hps), (0, 2, 1, 3))
    gcum = jnp.transpose(bg[:, DN_V_HEADS:2 * DN_V_HEADS].reshape(B, S, n_hg, hps), (0, 2, 1, 3))
    bg4 = jnp.concatenate([beta, gcum], axis=-1)
    gt4 = jnp.transpose(gcum.reshape(B, n_hg, S // C, C, hps // 2, 2), (0, 1, 4, 2, 5, 3)).reshape(
        B, n_hg, hps // 2, 2 * S)
    return pl.pallas_call(
        _delta_kernel,
        grid=(B, n_hg, S // T),
        in_specs=[pl.BlockSpec((1, T, qkw), lambda b, h, t: (b, t, h)),
                  pl.BlockSpec((1, T, qkw), lambda b, h, t: (b, t, q_w // qkw + h)),
                  pl.BlockSpec((1, T, vw), lambda b, h, t: (b, t, h)),
                  pl.BlockSpec((1, T, vw), lambda b, h, t: (b, t, h)),
                  pl.BlockSpec((1, 1, T, 2 * hps), lambda b, h, t: (b, h, t, 0)),
                  pl.BlockSpec((1, 1, hps // 2, 2 * T), lambda b, h, t: (b, h, 0, t)),
                  pl.BlockSpec((1, HD), lambda b, h, t: (0, 0))],
        out_specs=pl.BlockSpec((1, T, vw), lambda b, h, t: (b, t, h)),
        out_shape=jax.ShapeDtypeStruct((B, S, v_w), BF16),
        scratch_shapes=[pltpu.VMEM((hps, HD, HD), F32)],
        compiler_params=_params("arbitrary", "arbitrary", "arbitrary"),
        name="delta_rule",
    )(qk3, qk3, v3, z3, bg4, gt4, norm_g.reshape(1, HD)).reshape(B * S, v_w)


def kernel(x, rel_bias, attn_w_in, attn_w_out, dn_w_in, dn_conv_w, dn_a_log, dn_dt_bias, dn_norm_g, dn_w_out,
           ln1_g, ln1_b, router_w, router_b, w_gate_up, b_gate_up, w_down, b_down, ln2_g, ln2_b):
    B, S, D = x.shape
    N = B * S
    E = router_w.shape[2]
    x2d = x.reshape(N, D)

    def vec(p):
        return p.reshape(1, -1)

    def router(i):
        return jnp.transpose(router_w[i]).astype(BF16), router_b[i].reshape(E, 1)

    def moe(i, post):
        return _moe_block(post, i, w_gate_up, b_gate_up, w_down, b_down, vec(ln2_g[i]), vec(ln2_b[i]))

    w_in = attn_w_in[0].astype(BF16)
    os_, lses = [], []
    for g, (window, dilation) in enumerate(DILATED_GROUPS):
        assert window // dilation == ATTN_BLOCK
        qkv = _attn_inproj(x2d, w_in, g, dilation, B, S)
        bias = _attn_bias(rel_bias[:, g * ATTN_HEADS:(g + 1) * ATTN_HEADS], dilation)
        o, lse = _dilated_group(qkv, bias, g)
        os_.append(o)
        lses.append(lse)
    rwt, rb = router(0)
    post = _attn_out(os_, lses, attn_w_out[0].astype(BF16), x2d, vec(ln1_g[0]), vec(ln1_b[0]), rwt, rb, S)
    x2, x2b = moe(0, post)

    conv_cols = dn_conv_w.shape[2]
    main_cols = conv_cols + DN_V_HEADS * HEAD_DIM
    w_in = dn_w_in[0].astype(BF16)
    q_w = DN_QK_HEADS * HEAD_DIM
    v_w = DN_V_HEADS * HEAD_DIM
    qk = _dn_inproj(x2b, w_in, dn_conv_w[0], S, "qk", 0, 2 * q_w)
    v = _dn_inproj(x2b, w_in, dn_conv_w[0], S, "v", 2 * q_w, v_w)
    z = _dn_inproj(x2b, w_in, dn_conv_w[0], S, "z", conv_cols, v_w)
    bg = _dn_gates(x2b, w_in[:, main_cols:], dn_a_log[0], dn_dt_bias[0])
    o = _delta_rule(qk, v, z, bg, dn_norm_g[0], B, S)
    rwt, rb = router(1)
    post = _dn_out(o, dn_w_out[0].astype(BF16), x2, vec(ln1_g[1]), vec(ln1_b[1]), rwt, rb)
    x3, _ = moe(1, post)
    return x3.reshape(B, S, D)
```

```python
import functools
import math

import jax
import jax.numpy as jnp
from jax import lax
from jax.experimental import pallas as pl
from jax.experimental.pallas import tpu as pltpu

F32 = jnp.float32
BF16 = jnp.bfloat16
I32 = jnp.int32
U32 = jnp.uint32

DEPTH = 2
DEEPNORM_ALPHA = (2.0 * DEPTH) ** 0.25
LN_EPS = 1e-5
HEAD_DIM = 128
ATTN_HEADS = 8
ATTN_WIDTH = ATTN_HEADS * HEAD_DIM
DILATED_GROUPS = ((128, 1), (512, 4), (2048, 16))
ATTN_BLOCK = 128
MAX_DISTANCE = 2048
DN_QK_HEADS = 16
DN_V_HEADS = 32
DN_CONV = 4
DN_CHUNK = 64
DN_EPS = 1e-6
TOP_K = 4
SWIGLU_LIMIT = 7.0
SWIGLU_ALPHA = 1.702

MOE_ROWS = 1024
MOE_HALF = 512
MOE_FF_CHUNK = 512
ROW_TILE = 256
DISPATCH_TILE = 512
VMEM_LIMIT = 58 * 1024 * 1024


def _params(*semantics):
    return pltpu.CompilerParams(dimension_semantics=semantics, vmem_limit_bytes=VMEM_LIMIT)


def _pack_bf16_pair(lo, hi):
    lo_bits = lax.bitcast_convert_type(lo.astype(BF16).astype(F32), U32)
    hi_bits = lax.bitcast_convert_type(hi.astype(BF16).astype(F32), U32)
    return (hi_bits & jnp.uint32(0xFFFF0000)) | (lo_bits >> 16)


def _unpack_bf16_pair(u):
    lo = lax.bitcast_convert_type(u << 16, F32)
    hi = lax.bitcast_convert_type(u & jnp.uint32(0xFFFF0000), F32)
    return lo, hi


def _attn_inproj_kernel(a_ref, w_ref, o_ref, acc_ref, *, d, tm):
    res = jnp.dot(a_ref[...].astype(BF16), w_ref[...], preferred_element_type=F32)
    if d == 1:
        o_ref[0, 0] = res.astype(o_ref.dtype)
        return
    lanes = acc_ref.shape[2]
    for cb in range(acc_ref.shape[0]):
        acc_ref[cb] = res[:, cb * lanes:(cb + 1) * lanes]
    for r in range(d):
        for cb in range(acc_ref.shape[0]):
            o_ref[0, r, :, cb * lanes:(cb + 1) * lanes] = (
                acc_ref[cb, pl.ds(r, tm // d, stride=d), :].astype(o_ref.dtype))


def _attn_inproj(x2d, w, g, d, B, S):
    K = x2d.shape[1]
    tm, tn = 512, 1536
    gw = 3 * ATTN_WIDTH
    nt = gw // tn
    mt = S // tm
    return pl.pallas_call(
        functools.partial(_attn_inproj_kernel, d=d, tm=tm),
        grid=(nt, B, mt),
        in_specs=[pl.BlockSpec((tm, K), lambda n, b, m: (b * mt + m, 0)),
                  pl.BlockSpec((K, tn), lambda n, b, m: (0, g * nt + n))],
        out_specs=pl.BlockSpec((1, d, tm // d, tn), lambda n, b, m: (b, 0, m, n)),
        out_shape=jax.ShapeDtypeStruct((B, d, S // d, gw), BF16),
        scratch_shapes=[pltpu.VMEM((tn // HEAD_DIM, tm, HEAD_DIM), F32)],
        compiler_params=_params("arbitrary", "arbitrary", "arbitrary"),
        name=f"attn_inproj_g{g}",
    )(x2d, w)


def _t5_bucket(dist, num_buckets):
    max_exact = num_buckets // 2
    d_f = jnp.maximum(dist, 1).astype(F32)
    large = max_exact + (jnp.log(d_f / max_exact) / math.log(MAX_DISTANCE / max_exact)
                         * (num_buckets - max_exact)).astype(I32)
    large = jnp.minimum(large, num_buckets - 1)
    return jnp.where(dist < max_exact, dist, large)


def _attn_bias(rel_bias_g, dilation):
    bq = ATTN_BLOCK
    a = jnp.arange(bq, dtype=I32)[:, None]
    c = jnp.arange(2 * bq, dtype=I32)[None, :]
    steps = a + bq - c
    valid = (steps >= 0) & (steps <= bq)
    bucket = _t5_bucket(jnp.maximum(steps, 0) * dilation, rel_bias_g.shape[0])
    bias = jnp.transpose(rel_bias_g[bucket], (2, 0, 1)).astype(F32)
    return jnp.where(valid[None], bias, -jnp.inf)


ATTN_BLOCKS_PER_STEP = 2


def _attn_kernel(q_ref, kp_ref, kc_ref, vp_ref, vc_ref, bias_ref, o_ref, lse_ref):
    bq = ATTN_BLOCK
    first = pl.program_id(2) == 0
    scale = HEAD_DIM ** -0.5
    nt = (((1,), (1,)), ((), ()))
    lse_ref[0, 0] = jnp.zeros(lse_ref.shape[2:], F32)
    items = [(j, h) for j in range(ATTN_BLOCKS_PER_STEP) for h in range(ATTN_HEADS)]

    def rows(j):
        return slice(j * bq, (j + 1) * bq)

    def cols(h):
        return slice(h * HEAD_DIM, (h + 1) * HEAD_DIM)

    def prev(ref_p, ref_c, j, h):
        return ref_p[0, 0, :, cols(h)] if j == 0 else ref_c[0, 0, rows(j - 1), cols(h)]

    s_p = {(j, h): lax.dot_general(q_ref[0, 0, rows(j), cols(h)], prev(kp_ref, kc_ref, j, h), nt,
                                   preferred_element_type=F32) for j, h in items}
    s_c = {(j, h): lax.dot_general(q_ref[0, 0, rows(j), cols(h)], kc_ref[0, 0, rows(j), cols(h)], nt,
                                   preferred_element_type=F32) for j, h in items}
    p_p, p_c = {}, {}
    for j, h in items:
        sp = s_p[j, h] * scale + bias_ref[h, :, :bq]
        if j == 0:
            sp = jnp.where(first, -jnp.inf, sp)
        sc = s_c[j, h] * scale + bias_ref[h, :, bq:]
        m = jnp.maximum(jnp.max(sp, axis=-1, keepdims=True), jnp.max(sc, axis=-1, keepdims=True))
        e_p = jnp.exp(sp - m)
        e_c = jnp.exp(sc - m)
        l = jnp.sum(e_p, axis=-1, keepdims=True) + jnp.sum(e_c, axis=-1, keepdims=True)
        inv = 1.0 / l
        p_p[j, h] = (e_p * inv).astype(BF16)
        p_c[j, h] = (e_c * inv).astype(BF16)
        lse_ref[0, 0, rows(j), h:h + 1] = m + jnp.log(l)
    for j, h in items:
        o_ref[0, 0, rows(j), cols(h)] = (
            jnp.dot(p_p[j, h], prev(vp_ref, vc_ref, j, h), preferred_element_type=F32)
            + jnp.dot(p_c[j, h], vc_ref[0, 0, rows(j), cols(h)], preferred_element_type=F32))


def _dilated_group(qkv, bias, g):
    B, d, L, _ = qkv.shape
    nbs = ATTN_BLOCKS_PER_STEP
    nb = L // (nbs * ATTN_BLOCK)
    blk = (1, 1, nbs * ATTN_BLOCK, ATTN_WIDTH)
    lse_blk = (1, 1, nbs * ATTN_BLOCK, HEAD_DIM)

    def spec(kind, prev):
        if prev:
            return pl.BlockSpec((1, 1, ATTN_BLOCK, ATTN_WIDTH),
                                lambda b, r, i: (b, r, jnp.maximum(nbs * i - 1, 0), kind))
        return pl.BlockSpec(blk, lambda b, r, i: (b, r, i, kind))

    return pl.pallas_call(
        _attn_kernel,
        grid=(B, d, nb),
        in_specs=[spec(0, False), spec(1, True), spec(1, False), spec(2, True), spec(2, False),
                  pl.BlockSpec(bias.shape, lambda b, r, i: (0, 0, 0))],
        out_specs=[pl.BlockSpec(blk, lambda b, r, i: (b, r, i, 0)),
                   pl.BlockSpec(lse_blk, lambda b, r, i: (b, r, i, 0))],
        out_shape=[jax.ShapeDtypeStruct((B, d, L, ATTN_WIDTH), F32),
                   jax.ShapeDtypeStruct((B, d, L, HEAD_DIM), F32)],
        compiler_params=_params("arbitrary", "arbitrary", "arbitrary"),
        name=f"dilated_attn_g{g}",
    )(qkv, qkv, qkv, qkv, qkv, bias)


def _layer_norm(y, g_ref, b_ref):
    mu = jnp.mean(y, axis=-1, keepdims=True)
    yc = y - mu
    var = jnp.mean(yc * yc, axis=-1, keepdims=True)
    return yc * lax.rsqrt(var + LN_EPS) * g_ref[...] + b_ref[...]


def _residual_ln_router(h, x_ref, g_ref, b_ref, rwt_ref, rb_ref,
                        x1_ref, x1p_ref, topi_ref, gate_ref, rank_ref, cnt_ref, carry_ref, first_step):
    xn = _layer_norm(DEEPNORM_ALPHA * x_ref[...] + h, g_ref, b_ref)
    x1_ref[...] = xn
    half = xn.shape[1] // 2
    x1p_ref[...] = _pack_bf16_pair(xn[:, :half], xn[:, half:])
    logits = lax.dot_general(rwt_ref[...], xn.astype(BF16), (((1,), (1,)), ((), ())),
                             preferred_element_type=F32) + rb_ref[...]
    n_exp, tm = logits.shape
    eidx = lax.broadcasted_iota(I32, logits.shape, 0)
    vals, hots = [], []
    for k in range(TOP_K):
        mx = jnp.max(logits, axis=0, keepdims=True)
        idx = jnp.min(jnp.where(logits == mx, eidx, n_exp), axis=0, keepdims=True)
        topi_ref[k:k + 1, :] = idx
        vals.append(mx)
        hit = eidx == idx
        hots.append(hit.astype(F32))
        logits = jnp.where(hit, -jnp.inf, logits)
    exps = [jnp.exp(v - vals[0]) for v in vals]
    denom = exps[0] + exps[1] + exps[2] + exps[3]
    for k in range(TOP_K):
        gate_ref[k:k + 1, :] = exps[k] / denom

    @pl.when(first_step)
    def _():
        carry_ref[...] = jnp.zeros_like(carry_ref)

    tri = (lax.broadcasted_iota(I32, (tm, tm), 0) <= lax.broadcasted_iota(I32, (tm, tm), 1)).astype(BF16)
    cum = jnp.dot(jnp.concatenate(hots, axis=0).astype(BF16), tri, preferred_element_type=F32)
    base = carry_ref[...]
    for k in range(TOP_K):
        cum_k = cum[k * n_exp:(k + 1) * n_exp]
        before = cum_k - hots[k] + base
        rank_ref[k:k + 1, :] = jnp.sum(hots[k] * before, axis=0, keepdims=True).astype(I32)
        base = base + cum_k[:, tm - 1:tm]
    carry_ref[...] = base
    cnt_ref[...] = jnp.broadcast_to(base, cnt_ref.shape)


def _attn_out_kernel(o0_ref, o1_ref, o2_ref, l0_ref, l1_ref, l2_ref, w_ref, x_ref, g_ref, b_ref, rwt_ref, rb_ref,
                     x1_ref, x1p_ref, topi_ref, gate_ref, rank_ref, cnt_ref,
                     o_scr, l_scr, merged_ref, carry_ref, *, tm):
    o_refs = (o0_ref, o1_ref, o2_ref)
    l_refs = (l0_ref, l1_ref, l2_ref)
    H = ATTN_HEADS
    for g, (_, d) in enumerate(DILATED_GROUPS):
        for r in range(d):
            for h in range(H):
                o_scr[g * H + h, pl.ds(r, tm // d, stride=d), :] = o_refs[g][0, r, :, h * HEAD_DIM:(h + 1) * HEAD_DIM]
            l_scr[g, pl.ds(r, tm // d, stride=d), :] = l_refs[g][0, r]
    l0, l1, l2 = l_scr[0], l_scr[1], l_scr[2]
    m = jnp.maximum(jnp.maximum(l0, l1), l2)
    e = [jnp.exp(l0 - m), jnp.exp(l1 - m), jnp.exp(l2 - m)]
    den = e[0] + e[1] + e[2]
    wts = [ei / den for ei in e]
    for h in range(H):
        o = wts[0][:, h:h + 1] * o_scr[h]
        o = o + wts[1][:, h:h + 1] * o_scr[H + h]
        o = o + wts[2][:, h:h + 1] * o_scr[2 * H + h]
        merged_ref[:, h * HEAD_DIM:(h + 1) * HEAD_DIM] = o.astype(BF16)
    h_out = jnp.dot(merged_ref[...], w_ref[...], preferred_element_type=F32)
    _residual_ln_router(h_out, x_ref, g_ref, b_ref, rwt_ref, rb_ref, x1_ref, x1p_ref, topi_ref, gate_ref,
                        rank_ref, cnt_ref, carry_ref, pl.program_id(0) == 0)


def _dn_out_kernel(o_ref, w_ref, x_ref, g_ref, b_ref, rwt_ref, rb_ref,
                   x1_ref, x1p_ref, topi_ref, gate_ref, rank_ref, cnt_ref, carry_ref):
    h = jnp.dot(o_ref[...], w_ref[...], preferred_element_type=F32)
    _residual_ln_router(h, x_ref, g_ref, b_ref, rwt_ref, rb_ref, x1_ref, x1p_ref, topi_ref,
                        gate_ref, rank_ref, cnt_ref, carry_ref, pl.program_id(0) == 0)


def _post_specs(tm, D, E, idx):
    row = lambda *a: (idx(*a), 0)
    const = lambda *a: (0, 0)
    col = lambda *a: (0, idx(*a))
    ins = [pl.BlockSpec((tm, D), row),
           pl.BlockSpec((1, D), const), pl.BlockSpec((1, D), const),
           pl.BlockSpec((E, D), const), pl.BlockSpec((E, 1), const)]
    outs = [pl.BlockSpec((tm, D), row), pl.BlockSpec((tm, D // 2), row),
            pl.BlockSpec((TOP_K, tm), col), pl.BlockSpec((TOP_K, tm), col), pl.BlockSpec((TOP_K, tm), col),
            pl.BlockSpec((E, 128), const)]
    return ins, outs


def _post_shapes(N, D, E):
    return [jax.ShapeDtypeStruct((N, D), F32), jax.ShapeDtypeStruct((N, D // 2), U32),
            jax.ShapeDtypeStruct((TOP_K, N), I32), jax.ShapeDtypeStruct((TOP_K, N), F32),
            jax.ShapeDtypeStruct((TOP_K, N), I32), jax.ShapeDtypeStruct((E, 128), F32)]


def _attn_out(os_, lses, w_out, x2d, ln_g, ln_b, rwt, rb, S):
    N, D = x2d.shape
    E = rwt.shape[0]
    tm = 256
    mt = S // tm
    ins, outs = _post_specs(tm, D, E, lambda m: m)
    grp_specs = []
    for width in (ATTN_WIDTH, HEAD_DIM):
        for _, d in DILATED_GROUPS:
            grp_specs.append(pl.BlockSpec((1, d, tm // d, width), lambda m: (m // mt, 0, m % mt, 0)))
    return pl.pallas_call(
        functools.partial(_attn_out_kernel, tm=tm),
        grid=(N // tm,),
        in_specs=grp_specs + [pl.BlockSpec(w_out.shape, lambda m: (0, 0))] + ins,
        out_specs=outs,
        out_shape=_post_shapes(N, D, E),
        scratch_shapes=[pltpu.VMEM((len(DILATED_GROUPS) * ATTN_HEADS, tm, HEAD_DIM), F32),
                        pltpu.VMEM((len(DILATED_GROUPS), tm, HEAD_DIM), F32),
                        pltpu.VMEM((tm, ATTN_WIDTH), BF16),
                        pltpu.VMEM((E, 1), F32)],
        compiler_params=_params("arbitrary"),
        name="attn_out_ln_router",
    )(*os_, *lses, w_out, x2d, ln_g, ln_b, rwt, rb)


def _dn_out(o, w_out, x2d, ln_g, ln_b, rwt, rb):
    N, D = x2d.shape
    E = rwt.shape[0]
    Kin = o.shape[1]
    tm = 256
    ins, outs = _post_specs(tm, D, E, lambda m: m)
    return pl.pallas_call(
        _dn_out_kernel,
        grid=(N // tm,),
        in_specs=[pl.BlockSpec((tm, Kin), lambda m: (m, 0)),
                  pl.BlockSpec((Kin, D), lambda m: (0, 0), pipeline_mode=pl.Buffered(1))] + ins,
        out_specs=outs,
        out_shape=_post_shapes(N, D, E),
        scratch_shapes=[pltpu.VMEM((E, 1), F32)],
        compiler_params=_params("arbitrary"),
        name="dn_out_ln_router",
    )(o, w_out, x2d, ln_g, ln_b, rwt, rb)


def _slot_layout(topi, rank, counts, rows):
    K, N = topi.shape
    n_exp = counts.shape[0]
    padded = (counts + rows - 1) // rows * rows
    padded_end = jnp.cumsum(padded)
    padded_start = padded_end - padded
    sel = topi[None] == jnp.arange(n_exp, dtype=I32)[:, None, None]
    pos = jnp.sum(jnp.where(sel, padded_start[:, None, None], 0), axis=0) + rank
    n_blocks = -(-(K * N + n_exp * (rows - 1)) // rows)
    n_used = padded_end[-1] // rows
    blk = jnp.arange(n_blocks, dtype=I32)
    be = jnp.sum((blk[:, None] * rows >= padded_end[None, :]).astype(I32), axis=1)
    be = jnp.minimum(be, n_exp - 1)
    be_last = jnp.sum(jnp.where(blk == jnp.maximum(n_used - 1, 0), be, 0))
    used = blk < n_used
    be = jnp.where(used, be, be_last)
    start_b = jnp.sum(jnp.where(be[:, None] == jnp.arange(n_exp)[None, :], padded_start[None, :], 0), axis=1)
    cnt_e = jnp.sum(jnp.where(be[:, None] == jnp.arange(n_exp)[None, :], counts[None, :], 0), axis=1)
    valid = jnp.where(used, jnp.clip(cnt_e - (blk * rows - start_b), 0, rows), 0).astype(I32)
    n_slots = jnp.full((1,), n_blocks * rows, I32)
    pad_lo = jnp.concatenate([padded_start + counts, padded_end[-1:]]).astype(I32)
    pad_hi = jnp.concatenate([padded_end, n_slots]).astype(I32)
    return pos.astype(I32), be.astype(I32), valid, pad_lo, pad_hi, n_blocks


def _tile_major(pos, tm):
    K, N = pos.shape
    return jnp.transpose(pos.reshape(K, N // tm, tm), (1, 0, 2)).reshape(N // tm, K * tm)


def _dispatch_kernel(lo_ref, hi_ref, pos_hbm, x_ref, xs_hbm, idx_smem, zbuf, sem_idx, sem_rows, sem_zero, *, tm):
    i = pl.program_id(0)

    @pl.when(i == 0)
    def _():
        zbuf[...] = jnp.zeros_like(zbuf)
        one = zbuf.at[pl.ds(0, 1)]

        def per_expert(e, carry):
            lo = lo_ref[e]
            hi = hi_ref[e]
            lo_al = jnp.minimum((lo + 7) // 8 * 8, hi)

            def row_start(r, c):
                pltpu.make_async_copy(one, xs_hbm.at[pl.ds(r, 1)], sem_zero).start()
                return c

            def grp_start(q, c):
                pltpu.make_async_copy(zbuf, xs_hbm.at[pl.ds(pl.multiple_of(q * 8, 8), 8)], sem_zero).start()
                return c

            def row_wait(r, c):
                pltpu.make_async_copy(one, xs_hbm.at[pl.ds(0, 1)], sem_zero).wait()
                return c

            def grp_wait(q, c):
                pltpu.make_async_copy(zbuf, xs_hbm.at[pl.ds(0, 8)], sem_zero).wait()
                return c

            lax.fori_loop(lo, lo_al, row_start, 0)
            lax.fori_loop(lo_al // 8, hi // 8, grp_start, 0)
            lax.fori_loop(lo, lo_al, row_wait, 0)
            lax.fori_loop(lo_al // 8, hi // 8, grp_wait, 0)
            return carry

        lax.fori_loop(0, lo_ref.shape[0], per_expert, 0)

    cp = pltpu.make_async_copy(pos_hbm.at[i], idx_smem, sem_idx)
    cp.start()
    cp.wait()

    def issue(jj, carry):
        for u in range(8):
            j = jj * 8 + u
            for k in range(TOP_K):
                dst = idx_smem[k * tm + j]
                pltpu.make_async_copy(x_ref.at[pl.ds(j, 1)], xs_hbm.at[pl.ds(dst, 1)], sem_rows).start()
        return carry

    lax.fori_loop(0, tm // 8, issue, 0)
    for k in range(TOP_K):
        pltpu.make_async_copy(x_ref, xs_hbm.at[pl.ds(0, tm)], sem_rows).wait()


def _dispatch(x1p, pos_tiles, pad_lo, pad_hi, n_slots):
    N, W = x1p.shape
    tm = DISPATCH_TILE
    grid_spec = pltpu.PrefetchScalarGridSpec(
        num_scalar_prefetch=2,
        grid=(N // tm,),
        in_specs=[pl.BlockSpec(memory_space=pl.ANY),
                  pl.BlockSpec((tm, W), lambda i, lo, hi: (i, 0))],
        out_specs=pl.BlockSpec(memory_space=pl.ANY),
        scratch_shapes=[pltpu.SMEM((TOP_K * tm,), I32), pltpu.VMEM((8, W), U32),
                        pltpu.SemaphoreType.DMA, pltpu.SemaphoreType.DMA, pltpu.SemaphoreType.DMA],
    )
    return pl.pallas_call(
        functools.partial(_dispatch_kernel, tm=tm),
        grid_spec=grid_spec,
        out_shape=jax.ShapeDtypeStruct((n_slots, W), U32),
        compiler_params=_params("arbitrary"),
        name="moe_dispatch",
    )(pad_lo, pad_hi, pos_tiles, x1p)


def _ffn_kernel(be_ref, cnt_ref, xs_ref, wg_ref, wl_ref, bgu_ref, wd_ref, bd_ref, o_ref, xb_ref, act_ref,
                *, nfc, ndc, layer, n_exp):
    b = pl.program_id(0)
    p = pl.program_id(1)
    cnt = cnt_ref[b]
    rows = xb_ref.shape[0]
    half = xs_ref.shape[1]
    fc = o_ref.shape[1]
    n_half = rows // MOE_HALF
    up = p < nfc
    expert = layer * n_exp + be_ref[b]
    bg_row = expert * (2 * nfc) + jnp.minimum(p, nfc - 1)
    bd_row = expert * ndc + jnp.maximum(p - nfc, 0)

    @pl.when(jnp.logical_and(cnt > 0, p == 0))
    def _():
        lo, hi = _unpack_bf16_pair(xs_ref[...])
        xb_ref[:, :half] = lo.astype(BF16)
        xb_ref[:, half:] = hi.astype(BF16)

    def up_phase(live):
        wg = wg_ref[0, 0].astype(BF16)
        wl = wl_ref[0, 0].astype(BF16)
        bg = bgu_ref[pl.ds(bg_row, 1), :]
        bl = bgu_ref[pl.ds(bg_row + nfc, 1), :]
        acts = []
        for h in range(live):
            x = xb_ref[h * MOE_HALF:(h + 1) * MOE_HALF, :]
            hg = jnp.dot(x, wg, preferred_element_type=F32) + bg
            hl = jnp.dot(x, wl, preferred_element_type=F32) + bl
            glu = jnp.minimum(hg, SWIGLU_LIMIT)
            lin = jnp.clip(hl, -SWIGLU_LIMIT, SWIGLU_LIMIT)
            acts.append((glu * jax.nn.sigmoid(SWIGLU_ALPHA * glu) * (lin + 1.0)).astype(BF16))
        for c in range(nfc):
            @pl.when(p == c)
            def _():
                for h in range(live):
                    act_ref[h * MOE_HALF:(h + 1) * MOE_HALF, c * fc:(c + 1) * fc] = acts[h]

    def down_phase(live):
        wd = wd_ref[0, 0].astype(BF16) if live else None
        bd = bd_ref[pl.ds(bd_row, 1), :]
        for h in range(live):
            rs = slice(h * MOE_HALF, (h + 1) * MOE_HALF)
            o_ref[rs, :] = jnp.dot(act_ref[rs, :], wd, preferred_element_type=F32) + bd
        for h in range(live, n_half):
            rs = slice(h * MOE_HALF, (h + 1) * MOE_HALF)
            o_ref[rs, :] = jnp.zeros((MOE_HALF, fc), F32)

    for live in range(1, n_half + 1):
        in_range = cnt > (live - 1) * MOE_HALF
        if live < n_half:
            in_range = jnp.logical_and(in_range, cnt <= live * MOE_HALF)

        @pl.when(jnp.logical_and(in_range, up))
        def _():
            up_phase(live)

        @pl.when(jnp.logical_and(in_range, jnp.logical_not(up)))
        def _():
            down_phase(live)

    @pl.when(jnp.logical_and(cnt == 0, jnp.logical_not(up)))
    def _():
        down_phase(0)


def _expert_ffn(xs, be, valid, layer, w_gate_up, b_gate_up, w_down, b_down):
    n_slots, half = xs.shape
    D = 2 * half
    L, E, _, F2 = w_gate_up.shape
    F = F2 // 2
    rows, fc = MOE_ROWS, MOE_FF_CHUNK
    assert F % fc == 0 and D % fc == 0
    nfc = F // fc
    ndc = D // fc
    n_blocks = n_slots // rows

    def up_chunk(b, p, cnt):
        return jnp.where(cnt[b] > 0, jnp.minimum(p, nfc - 1), nfc - 1)

    def down_chunk(b, p, cnt):
        return jnp.where(cnt[b] > 0, jnp.maximum(p - nfc, 0), ndc - 1)

    def weight(height, col):
        return pl.BlockSpec((1, 1, height, fc), lambda b, p, be, cnt: (layer, be[b], 0, col(b, p, cnt)))

    def table(n_rows):
        return pl.BlockSpec((n_rows, fc), lambda b, p, be, cnt: (0, 0))

    def lin_chunk(b, p, cnt):
        return nfc + up_chunk(b, p, cnt)

    grid_spec = pltpu.PrefetchScalarGridSpec(
        num_scalar_prefetch=2,
        grid=(n_blocks, nfc + ndc),
        in_specs=[pl.BlockSpec((rows, half), lambda b, p, be, cnt: (b, 0)),
                  weight(D, up_chunk), weight(D, lin_chunk), table(L * E * 2 * nfc),
                  weight(F, down_chunk), table(L * E * ndc)],
        out_specs=pl.BlockSpec((rows, fc), lambda b, p, be, cnt: (b, jnp.maximum(p - nfc, 0))),
        scratch_shapes=[pltpu.VMEM((rows, D), BF16), pltpu.VMEM((rows, F), BF16)],
    )
    return pl.pallas_call(
        functools.partial(_ffn_kernel, nfc=nfc, ndc=ndc, layer=layer, n_exp=E),
        grid_spec=grid_spec,
        out_shape=jax.ShapeDtypeStruct((n_slots, D), F32),
        compiler_params=_params("arbitrary", "arbitrary"),
        name="expert_ffn",
    )(be, valid, xs, w_gate_up, w_gate_up, b_gate_up.reshape(L * E * 2 * nfc, fc),
      w_down, b_down.reshape(L * E * ndc, fc))


def _combine_kernel(pos_hbm, ys_hbm, gate_ref, x_ref, g_ref, b_ref, x2_ref, x2b_ref,
                    idx0_smem, idx1_smem, buf, sem_idx, sem_rows, *, tm):
    i = pl.program_id(0)
    n_tiles = 2 * pl.num_programs(0)
    idx_smem = (idx0_smem, idx1_smem)
    D = x_ref.shape[1]

    def idx_copy(tile, slot):
        return pltpu.make_async_copy(pos_hbm.at[tile], idx_smem[slot], sem_idx.at[slot])

    def gather(tile, slot):
        idx_copy(tile, slot).wait()

        def issue(jj, carry):
            for u in range(8):
                for k in range(TOP_K):
                    src = idx_smem[slot][k * tm + jj * 8 + u]
                    pltpu.make_async_copy(ys_hbm.at[src >> 3, pl.ds(src & 7, 1)], buf.at[slot, k, jj, pl.ds(u, 1)],
                                          sem_rows.at[slot]).start()
            return carry

        lax.fori_loop(0, tm // 8, issue, 0)

        @pl.when(tile + 2 < n_tiles)
        def _():
            idx_copy(tile + 2, slot).start()

    def reduce(slot):
        rs = slice(slot * tm, (slot + 1) * tm)
        for k in range(TOP_K):
            pltpu.make_async_copy(ys_hbm.at[pl.ds(0, tm // 8)], buf.at[slot, k], sem_rows.at[slot]).wait()
        gate = gate_ref[rs, :]
        f = buf[slot, 0].reshape(tm, D) * gate[:, 0:1]
        for k in range(1, TOP_K):
            f = f + buf[slot, k].reshape(tm, D) * gate[:, k:k + 1]
        xn = _layer_norm(DEEPNORM_ALPHA * x_ref[rs, :] + f, g_ref, b_ref)
        x2_ref[rs, :] = xn
        x2b_ref[rs, :] = xn.astype(BF16)

    @pl.when(i == 0)
    def _():
        idx_copy(0, 0).start()
        idx_copy(1, 1).start()
        gather(0, 0)

    gather(2 * i + 1, 1)
    reduce(0)

    @pl.when(2 * i + 2 < n_tiles)
    def _():
        gather(2 * i + 2, 0)

    reduce(1)


def _combine_ln(ys, pos_tiles, gates_t, x1, ln_g, ln_b):
    N, D = x1.shape
    tm = ROW_TILE
    row = pl.BlockSpec((2 * tm, D), lambda m: (m, 0))
    vec = pl.BlockSpec((1, D), lambda m: (0, 0))
    anyspec = pl.BlockSpec(memory_space=pl.ANY)
    return pl.pallas_call(
        functools.partial(_combine_kernel, tm=tm),
        grid=(N // (2 * tm),),
        in_specs=[anyspec, anyspec, pl.BlockSpec((2 * tm, TOP_K), lambda m: (m, 0)), row, vec, vec],
        out_specs=[row, row],
        out_shape=[jax.ShapeDtypeStruct((N, D), F32), jax.ShapeDtypeStruct((N, D), BF16)],
        scratch_shapes=[pltpu.SMEM((TOP_K * tm,), I32), pltpu.SMEM((TOP_K * tm,), I32),
                        pltpu.VMEM((2, TOP_K, tm // 8, 8, D), F32),
                        pltpu.SemaphoreType.DMA((2,)), pltpu.SemaphoreType.DMA((2,))],
        compiler_params=_params("arbitrary"),
        name="moe_combine_ln",
    )(pos_tiles, ys.reshape(ys.shape[0] // 8, 8, D), gates_t, x1, ln_g, ln_b)


def _moe_block(post, layer, w_gate_up, b_gate_up, w_down, b_down, ln_g, ln_b):
    x1, x1p, topi, gates, rank, cnt = post
    counts = cnt[:, 0].astype(I32)
    pos, be, valid, pad_lo, pad_hi, n_blocks = _slot_layout(topi, rank, counts, MOE_ROWS)
    xs = _dispatch(x1p, _tile_major(pos, DISPATCH_TILE), pad_lo, pad_hi, n_blocks * MOE_ROWS)
    ys = _expert_ffn(xs, be, valid, layer, w_gate_up, b_gate_up, w_down, b_down)
    return _combine_ln(ys, _tile_major(pos, ROW_TILE), jnp.transpose(gates), x1, ln_g, ln_b)


DN_COL_SUB = 256


def _dn_inproj_kernel(x_ref, w_ref, cw_ref, o_ref, pbuf, *, mode, tm, tn, seq_tiles, q_tiles):
    n = pl.program_id(0)
    m = pl.program_id(1)
    if mode != "z":
        @pl.when(m % seq_tiles == 0)
        def _():
            pbuf[0:8, :] = jnp.zeros((8, tn), F32)

    x = x_ref[...]
    sc = jnp.where(n < q_tiles, HEAD_DIM ** -0.5, 1.0).astype(F32)
    for cs in range(tn // DN_COL_SUB):
        cols = slice(cs * DN_COL_SUB, (cs + 1) * DN_COL_SUB)
        p = jnp.dot(x, w_ref[:, cols], preferred_element_type=F32)
        if mode == "z":
            o_ref[:, cols] = p
            continue
        pbuf[8:tm + 8, cols] = p
        cw = cw_ref[:, cols]
        y = cw[0:1] * pbuf[5:tm + 5, cols]
        y = y + cw[1:2] * pbuf[6:tm + 6, cols]
        y = y + cw[2:3] * pbuf[7:tm + 7, cols]
        y = y + cw[3:4] * p
        pbuf[0:8, cols] = pbuf[tm:tm + 8, cols]
        y = y * jax.nn.sigmoid(y)
        if mode == "v":
            o_ref[:, cols] = y
            continue
        for hh in range(DN_COL_SUB // HEAD_DIM):
            sl = slice(hh * HEAD_DIM, (hh + 1) * HEAD_DIM)
            ys = y[:, sl]
            ss = jnp.sum(ys * ys, axis=-1, keepdims=True)
            o_ref[:, cs * DN_COL_SUB + hh * HEAD_DIM:cs * DN_COL_SUB + (hh + 1) * HEAD_DIM] = (
                ys * lax.rsqrt(ss + DN_EPS) * sc)


def _dn_inproj(xb, w, conv_w, seq_len, mode, col0, width):
    N, K = xb.shape
    tm, tn = 512, 2048
    t0 = col0 // tn
    q_w = DN_QK_HEADS * HEAD_DIM
    conv_tiles = conv_w.shape[1] // tn
    kern = functools.partial(_dn_inproj_kernel, mode=mode, tm=tm, tn=tn, seq_tiles=seq_len // tm,
                             q_tiles=q_w // tn if mode == "qk" else 0)
    return pl.pallas_call(
        kern,
        grid=(width // tn, N // tm),
        in_specs=[pl.BlockSpec((tm, K), lambda n, m: (m, 0)),
                  pl.BlockSpec((K, tn), lambda n, m: (0, t0 + n)),
                  pl.BlockSpec((DN_CONV, tn), lambda n, m: (0, jnp.minimum(t0 + n, conv_tiles - 1)))],
        out_specs=pl.BlockSpec((tm, tn), lambda n, m: (m, n)),
        out_shape=jax.ShapeDtypeStruct((N, width), F32),
        scratch_shapes=[pltpu.VMEM((tm + 8, tn), F32)],
        compiler_params=_params("arbitrary", "arbitrary"),
        name=f"dn_inproj_{mode}",
    )(xb, w, conv_w)


def _dn_gates_kernel(x_ref, w_ref, alog_ref, dtb_ref, o_ref):
    r = jnp.dot(x_ref[...], w_ref[...], preferred_element_type=F32)
    lane = lax.broadcasted_iota(I32, r.shape, 1)
    row = lax.broadcasted_iota(I32, r.shape, 0)
    beta = jax.nn.sigmoid(r)
    z = r + dtb_ref[...]
    softplus = jnp.maximum(z, 0.0) + jnp.log1p(jnp.exp(-jnp.abs(z)))
    g = -jnp.exp(alog_ref[...]) * softplus
    in_chunk = row % DN_CHUNK
    s = 1
    while s < DN_CHUNK:
        g = g + jnp.where(in_chunk >= s, pltpu.roll(g, s, axis=0), 0.0)
        s *= 2
    o_ref[...] = jnp.where(lane < DN_V_HEADS, beta, g)


def _dn_gates(xb, w_ba, a_log, dt_bias):
    N, K = xb.shape
    H = DN_V_HEADS
    tm = 512
    lanes = 128
    w_pad = jnp.zeros((K, lanes), BF16).at[:, :2 * H].set(w_ba)
    alog_pad = jnp.zeros((1, lanes), F32).at[0, H:2 * H].set(a_log)
    dtb_pad = jnp.zeros((1, lanes), F32).at[0, H:2 * H].set(dt_bias)
    vec = pl.BlockSpec((1, lanes), lambda m: (0, 0))
    return pl.pallas_call(
        _dn_gates_kernel,
        grid=(N // tm,),
        in_specs=[pl.BlockSpec((tm, K), lambda m: (m, 0)), pl.BlockSpec((K, lanes), lambda m: (0, 0)), vec, vec],
        out_specs=pl.BlockSpec((tm, lanes), lambda m: (m, 0)),
        out_shape=jax.ShapeDtypeStruct((N, lanes), F32),
        compiler_params=_params("arbitrary"),
        name="dn_gates",
    )(xb, w_pad, alog_pad, dtb_pad)


DN_HEADS_PER_STEP = 8
DN_CHUNKS_PER_STEP = 4


def _delta_kernel(q_ref, k_ref, v_ref, z_ref, bg_ref, gtp_ref, ng_ref, o_ref, s_ref):
    C = DN_CHUNK
    HD = HEAD_DIM
    nh, nc = DN_HEADS_PER_STEP, DN_CHUNKS_PER_STEP
    npair = nh // 2
    ngrp = nh // 4
    nt = (((1,), (1,)), ((), ()))

    @pl.when(pl.program_id(2) == 0)
    def _():
        s_ref[...] = jnp.zeros_like(s_ref)

    ri2 = lax.broadcasted_iota(I32, (C, 2 * C), 0)
    li2 = lax.broadcasted_iota(I32, (C, 2 * C), 1)
    left = li2 < C
    ci2 = li2 % C
    incl2 = ri2 >= ci2
    strict2 = ri2 > ci2
    left_row = lax.broadcasted_iota(I32, (1, 2 * C), 1) < C
    ri4 = lax.broadcasted_iota(I32, (C, 4 * C), 0)
    li4 = lax.broadcasted_iota(I32, (C, 4 * C), 1)
    eye4 = (ri4 == li4 % C).astype(F32)
    blk4 = (lax.broadcasted_iota(I32, (4 * C, 4 * C), 0) // C) == (lax.broadcasted_iota(I32, (4 * C, 4 * C), 1) // C)
    zero_hd = jnp.zeros((C, HD), BF16)
    zero_s = jnp.zeros((HD, HD), BF16)

    def block_diag4(x):
        return jnp.where(blk4, jnp.concatenate([x, x, x, x], axis=0), 0.0).astype(BF16)

    b2, lhs2, rhs, qg, glast, bcat, qcat = {}, {}, {}, {}, {}, {}, {}
    for c in range(nc):
        rows = slice(c * C, (c + 1) * C)
        for pj in range(npair):
            csl = slice(pj * HD, (pj + 1) * HD)
            q = q_ref[0, rows, csl]
            k = k_ref[0, rows, csl]
            kk = jnp.concatenate([k, k], axis=0)
            m1 = lax.dot_general(jnp.concatenate([k, q], axis=0).astype(BF16), kk.astype(BF16), nt,
                                 preferred_element_type=F32)
            ja, jb = 2 * pj, 2 * pj + 1
            bcol = [bg_ref[0, 0, rows, j:j + 1] for j in (ja, jb)]
            gcol = [bg_ref[0, 0, rows, nh + j:nh + j + 1] for j in (ja, jb)]
            grow2 = gtp_ref[0, 0, pj:pj + 1, c * 2 * C:(c + 1) * 2 * C]
            gcol2 = jnp.where(left, gcol[0], gcol[1])
            bcol2 = jnp.where(left, bcol[0], bcol[1])
            decay2 = jnp.exp(jnp.where(incl2, gcol2 - grow2, -jnp.inf))
            b2[c, pj] = jnp.where(strict2, -(m1[:C] * bcol2 * decay2), 0.0)
            qk2 = (m1[C:] * decay2).astype(BF16)
            gl = [g[C - 1:C, :] for g in gcol]
            glast2 = jnp.where(left_row, gl[0], gl[1])
            kdec_t2 = (kk.T * jnp.exp(glast2 - grow2)).astype(BF16)
            lhs2[c, pj] = jnp.concatenate([qk2, kdec_t2], axis=0)
            for i, j in enumerate((ja, jb)):
                eg = jnp.exp(gcol[i])
                v = v_ref[0, rows, j * HD:(j + 1) * HD]
                kb = k * bcol[i]
                rhs[c, j] = jnp.concatenate([v * bcol[i], kb * eg], axis=1).astype(BF16)
                qg[c, j] = (q * eg).astype(BF16)
                glast[c, j] = gl[i]

    groups = [(c, g) for c in range(nc) for g in range(ngrp)]
    for c, g in groups:
        bcat[c, g] = jnp.concatenate([b2[c, 2 * g], b2[c, 2 * g + 1]], axis=1)
        qcat[c, g] = eye4 + bcat[c, g]
    for it in groups:
        bcat[it] = jnp.dot(bcat[it].astype(BF16), block_diag4(bcat[it]), preferred_element_type=F32)
    for _ in range(4):
        for it in groups:
            r = jnp.dot(jnp.concatenate([qcat[it], bcat[it]], axis=0).astype(BF16), block_diag4(bcat[it]),
                        preferred_element_type=F32)
            qcat[it] = qcat[it] + r[:C]
            bcat[it] = r[C:]
    uw = {}
    for c, g in groups:
        it = (c, g)
        t_inv = qcat[it] + jnp.dot(qcat[it].astype(BF16), block_diag4(bcat[it]), preferred_element_type=F32)
        rstack = jnp.concatenate([rhs[c, 4 * g + i] for i in range(4)], axis=0)
        r = jnp.dot(block_diag4(t_inv), rstack, preferred_element_type=F32)
        for i in range(4):
            uw[c, 4 * g + i] = r[i * C:(i + 1) * C]

    states = [s_ref[j] for j in range(nh)]
    ng = ng_ref[...]
    for c in range(nc):
        rows = slice(c * C, (c + 1) * C)
        r1 = {}
        for pj in range(npair):
            ja, jb = 2 * pj, 2 * pj + 1
            lhs = jnp.concatenate(
                [jnp.concatenate([uw[c, j][:, HD:].astype(BF16), qg[c, j]], axis=0) for j in (ja, jb)], axis=1)
            sa, sb = states[ja].astype(BF16), states[jb].astype(BF16)
            sbd = jnp.concatenate([jnp.concatenate([sa, zero_s], axis=1),
                                   jnp.concatenate([zero_s, sb], axis=1)], axis=0)
            r1[pj] = jnp.dot(lhs, sbd, preferred_element_type=F32)
        r2 = {}
        for pj in range(npair):
            ja, jb = 2 * pj, 2 * pj + 1
            va = (uw[c, ja][:, :HD] - r1[pj][:C, :HD]).astype(BF16)
            vb = (uw[c, jb][:, :HD] - r1[pj][:C, HD:]).astype(BF16)
            vbd = jnp.concatenate([jnp.concatenate([va, zero_hd], axis=1),
                                   jnp.concatenate([zero_hd, vb], axis=1)], axis=0)
            r2[pj] = jnp.dot(lhs2[c, pj], vbd, preferred_element_type=F32)
        for pj in range(npair):
            for i, j in enumerate((2 * pj, 2 * pj + 1)):
                hs = slice(i * HD, (i + 1) * HD)
                o = r1[pj][C:, hs] + r2[pj][:C, hs]
                states[j] = states[j] * jnp.exp(glast[c, j]) + r2[pj][C:, hs]
                o = o * lax.rsqrt(jnp.mean(o * o, axis=-1, keepdims=True) + DN_EPS) * ng
                zz = z_ref[0, rows, j * HD:(j + 1) * HD]
                o_ref[0, rows, j * HD:(j + 1) * HD] = (o * (zz * jax.nn.sigmoid(zz))).astype(o_ref.dtype)

    for j in range(nh):
        s_ref[j] = states[j]


def _delta_rule(qk, v, z, bg, norm_g, B, S):
    HD = HEAD_DIM
    C = DN_CHUNK
    hps = DN_HEADS_PER_STEP
    T = DN_CHUNKS_PER_STEP * C
    q_w = DN_QK_HEADS * HD
    v_w = DN_V_HEADS * HD
    qkw = hps // 2 * HD
    vw = hps * HD
    n_hg = DN_V_HEADS // hps
    qk3 = qk.reshape(B, S, 2 * q_w)
    v3 = v.reshape(B, S, v_w)
    z3 = z.reshape(B, S, v_w)
    beta = jnp.transpose(bg[:, :DN_V_HEADS].reshape(B, S, n_hg, hps), (0, 2, 1, 3))
    gcum = jnp.transpose(bg[:, DN_V_HEADS:2 * DN_V_HEADS].reshape(B, S, n_hg, hps), (0, 2, 1, 3))
    bg4 = jnp.concatenate([beta, gcum], axis=-1)
    gt4 = jnp.transpose(gcum.reshape(B, n_hg, S // C, C, hps // 2, 2), (0, 1, 4, 2, 5, 3)).reshape(
        B, n_hg, hps // 2, 2 * S)
    return pl.pallas_call(
        _delta_kernel,
        grid=(B, n_hg, S // T),
        in_specs=[pl.BlockSpec((1, T, qkw), lambda b, h, t: (b, t, h)),
                  pl.BlockSpec((1, T, qkw), lambda b, h, t: (b, t, q_w // qkw + h)),
                  pl.BlockSpec((1, T, vw), lambda b, h, t: (b, t, h)),
                  pl.BlockSpec((1, T, vw), lambda b, h, t: (b, t, h)),
                  pl.BlockSpec((1, 1, T, 2 * hps), lambda b, h, t: (b, h, t, 0)),
                  pl.BlockSpec((1, 1, hps // 2, 2 * T), lambda b, h, t: (b, h, 0, t)),
                  pl.BlockSpec((1, HD), lambda b, h, t: (0, 0))],
        out_specs=pl.BlockSpec((1, T, vw), lambda b, h, t: (b, t, h)),
        out_shape=jax.ShapeDtypeStruct((B, S, v_w), BF16),
        scratch_shapes=[pltpu.VMEM((hps, HD, HD), F32)],
        compiler_params=_params("arbitrary", "arbitrary", "arbitrary"),
        name="delta_rule",
    )(qk3, qk3, v3, z3, bg4, gt4, norm_g.reshape(1, HD)).reshape(B * S, v_w)


def kernel(x, rel_bias, attn_w_in, attn_w_out, dn_w_in, dn_conv_w, dn_a_log, dn_dt_bias, dn_norm_g, dn_w_out,
           ln1_g, ln1_b, router_w, router_b, w_gate_up, b_gate_up, w_down, b_down, ln2_g, ln2_b):
    B, S, D = x.shape
    N = B * S
    E = router_w.shape[2]
    x2d = x.reshape(N, D)

    def vec(p):
        return p.reshape(1, -1)

    def router(i):
        return jnp.transpose(router_w[i]).astype(BF16), router_b[i].reshape(E, 1)

    def moe(i, post):
        return _moe_block(post, i, w_gate_up, b_gate_up, w_down, b_down, vec(ln2_g[i]), vec(ln2_b[i]))

    w_in = attn_w_in[0].astype(BF16)
    os_, lses = [], []
    for g, (window, dilation) in enumerate(DILATED_GROUPS):
        assert window // dilation == ATTN_BLOCK
        qkv = _attn_inproj(x2d, w_in, g, dilation, B, S)
        bias = _attn_bias(rel_bias[:, g * ATTN_HEADS:(g + 1) * ATTN_HEADS], dilation)
        o, lse = _dilated_group(qkv, bias, g)
        os_.append(o)
        lses.append(lse)
    rwt, rb = router(0)
    post = _attn_out(os_, lses, attn_w_out[0].astype(BF16), x2d, vec(ln1_g[0]), vec(ln1_b[0]), rwt, rb, S)
    x2, x2b = moe(0, post)

    conv_cols = dn_conv_w.shape[2]
    main_cols = conv_cols + DN_V_HEADS * HEAD_DIM
    w_in = dn_w_in[0].astype(BF16)
    q_w = DN_QK_HEADS * HEAD_DIM
    v_w = DN_V_HEADS * HEAD_DIM
    qk = _dn_inproj(x2b, w_in, dn_conv_w[0], S, "qk", 0, 2 * q_w)
    v = _dn_inproj(x2b, w_in, dn_conv_w[0], S, "v", 2 * q_w, v_w)
    z = _dn_inproj(x2b, w_in, dn_conv_w[0], S, "z", conv_cols, v_w)
    bg = _dn_gates(x2b, w_in[:, main_cols:], dn_a_log[0], dn_dt_bias[0])
    o = _delta_rule(qk, v, z, bg, dn_norm_g[0], B, S)
    rwt, rb = router(1)
    post = _dn_out(o, dn_w_out[0].astype(BF16), x2, vec(ln1_g[1]), vec(ln1_b[1]), rwt, rb)
    x3, _ = moe(1, post)
    return x3.reshape(B, S, D)
```

```python
import functools
import math

import jax
import jax.numpy as jnp
from jax import lax
from jax.experimental import pallas as pl
from jax.experimental.pallas import tpu as pltpu

F32 = jnp.float32
BF16 = jnp.bfloat16
I32 = jnp.int32
U32 = jnp.uint32

DEPTH = 2
DEEPNORM_ALPHA = (2.0 * DEPTH) ** 0.25
LN_EPS = 1e-5
HEAD_DIM = 128
ATTN_HEADS = 8
ATTN_WIDTH = ATTN_HEADS * HEAD_DIM
DILATED_GROUPS = ((128, 1), (512, 4), (2048, 16))
ATTN_BLOCK = 128
MAX_DISTANCE = 2048
DN_QK_HEADS = 16
DN_V_HEADS = 32
DN_CONV = 4
DN_CHUNK = 64
DN_EPS = 1e-6
TOP_K = 4
SWIGLU_LIMIT = 7.0
SWIGLU_ALPHA = 1.702

MOE_ROWS = 1024
MOE_HALF = 512
MOE_FF_CHUNK = 512
ROW_TILE = 256
DISPATCH_TILE = 512
VMEM_LIMIT = 58 * 1024 * 1024


def _params(*semantics):
    return pltpu.CompilerParams(dimension_semantics=semantics, vmem_limit_bytes=VMEM_LIMIT)


def _pack_bf16_pair(lo, hi):
    lo_bits = lax.bitcast_convert_type(lo.astype(BF16).astype(F32), U32)
    hi_bits = lax.bitcast_convert_type(hi.astype(BF16).astype(F32), U32)
    return (hi_bits & jnp.uint32(0xFFFF0000)) | (lo_bits >> 16)


def _unpack_bf16_pair(u):
    lo = lax.bitcast_convert_type(u << 16, F32)
    hi = lax.bitcast_convert_type(u & jnp.uint32(0xFFFF0000), F32)
    return lo, hi


def _attn_inproj_kernel(a_ref, w_ref, o_ref, acc_ref, *, d, tm):
    res = jnp.dot(a_ref[...].astype(BF16), w_ref[...], preferred_element_type=F32)
    if d == 1:
        o_ref[0, 0] = res.astype(o_ref.dtype)
        return
    lanes = acc_ref.shape[2]
    for cb in range(acc_ref.shape[0]):
        acc_ref[cb] = res[:, cb * lanes:(cb + 1) * lanes]
    for r in range(d):
        for cb in range(acc_ref.shape[0]):
            o_ref[0, r, :, cb * lanes:(cb + 1) * lanes] = (
                acc_ref[cb, pl.ds(r, tm // d, stride=d), :].astype(o_ref.dtype))


def _attn_inproj(x2d, w, g, d, B, S):
    K = x2d.shape[1]
    tm, tn = 512, 1536
    gw = 3 * ATTN_WIDTH
    nt = gw // tn
    mt = S // tm
    return pl.pallas_call(
        functools.partial(_attn_inproj_kernel, d=d, tm=tm),
        grid=(nt, B, mt),
        in_specs=[pl.BlockSpec((tm, K), lambda n, b, m: (b * mt + m, 0)),
                  pl.BlockSpec((K, tn), lambda n, b, m: (0, g * nt + n))],
        out_specs=pl.BlockSpec((1, d, tm // d, tn), lambda n, b, m: (b, 0, m, n)),
        out_shape=jax.ShapeDtypeStruct((B, d, S // d, gw), BF16),
        scratch_shapes=[pltpu.VMEM((tn // HEAD_DIM, tm, HEAD_DIM), F32)],
        compiler_params=_params("arbitrary", "arbitrary", "arbitrary"),
        name=f"attn_inproj_g{g}",
    )(x2d, w)


def _t5_bucket(dist, num_buckets):
    max_exact = num_buckets // 2
    d_f = jnp.maximum(dist, 1).astype(F32)
    large = max_exact + (jnp.log(d_f / max_exact) / math.log(MAX_DISTANCE / max_exact)
                         * (num_buckets - max_exact)).astype(I32)
    large = jnp.minimum(large, num_buckets - 1)
    return jnp.where(dist < max_exact, dist, large)


def _attn_bias(rel_bias_g, dilation):
    bq = ATTN_BLOCK
    a = jnp.arange(bq, dtype=I32)[:, None]
    c = jnp.arange(2 * bq, dtype=I32)[None, :]
    steps = a + bq - c
    valid = (steps >= 0) & (steps <= bq)
    n_buckets = rel_bias_g.shape[0]
    bucket = _t5_bucket(jnp.maximum(steps, 0) * dilation, n_buckets)
    table = jnp.transpose(rel_bias_g).astype(F32)
    bias = jnp.zeros((table.shape[0],) + bucket.shape, F32)
    for b in range(n_buckets):
        bias = jnp.where(bucket[None] == b, table[:, b][:, None, None], bias)
    return jnp.where(valid[None], bias, -jnp.inf)


ATTN_BLOCKS_PER_STEP = 2


def _attn_kernel(q_ref, kp_ref, kc_ref, vp_ref, vc_ref, bias_ref, o_ref, lse_ref):
    bq = ATTN_BLOCK
    nbs = q_ref.shape[2] // bq
    first = pl.program_id(2) == 0
    scale = HEAD_DIM ** -0.5
    nt = (((1,), (1,)), ((), ()))
    lse_ref[0, 0] = jnp.zeros(lse_ref.shape[2:], F32)
    items = [(j, h) for j in range(nbs) for h in range(ATTN_HEADS)]

    def rows(j):
        return slice(j * bq, (j + 1) * bq)

    def cols(h):
        return slice(h * HEAD_DIM, (h + 1) * HEAD_DIM)

    def prev(ref_p, ref_c, j, h):
        return ref_p[0, 0, :, cols(h)] if j == 0 else ref_c[0, 0, rows(j - 1), cols(h)]

    s_p = {(j, h): lax.dot_general(q_ref[0, 0, rows(j), cols(h)], prev(kp_ref, kc_ref, j, h), nt,
                                   preferred_element_type=F32) for j, h in items}
    s_c = {(j, h): lax.dot_general(q_ref[0, 0, rows(j), cols(h)], kc_ref[0, 0, rows(j), cols(h)], nt,
                                   preferred_element_type=F32) for j, h in items}
    p_p, p_c = {}, {}
    for j, h in items:
        sp = s_p[j, h] * scale + bias_ref[h, :, :bq]
        if j == 0:
            sp = jnp.where(first, -jnp.inf, sp)
        sc = s_c[j, h] * scale + bias_ref[h, :, bq:]
        m = jnp.maximum(jnp.max(sp, axis=-1, keepdims=True), jnp.max(sc, axis=-1, keepdims=True))
        e_p = jnp.exp(sp - m)
        e_c = jnp.exp(sc - m)
        l = jnp.sum(e_p, axis=-1, keepdims=True) + jnp.sum(e_c, axis=-1, keepdims=True)
        inv = 1.0 / l
        p_p[j, h] = (e_p * inv).astype(BF16)
        p_c[j, h] = (e_c * inv).astype(BF16)
        lse_ref[0, 0, rows(j), h:h + 1] = m + jnp.log(l)
    for j, h in items:
        o_ref[0, 0, rows(j), cols(h)] = (
            jnp.dot(p_p[j, h], prev(vp_ref, vc_ref, j, h), preferred_element_type=F32)
            + jnp.dot(p_c[j, h], vc_ref[0, 0, rows(j), cols(h)], preferred_element_type=F32))


def _dilated_group(qkv, bias, g):
    B, d, L, _ = qkv.shape
    nbs = min(ATTN_BLOCKS_PER_STEP, L // ATTN_BLOCK)
    assert L % (nbs * ATTN_BLOCK) == 0
    nb = L // (nbs * ATTN_BLOCK)
    blk = (1, 1, nbs * ATTN_BLOCK, ATTN_WIDTH)
    lse_blk = (1, 1, nbs * ATTN_BLOCK, HEAD_DIM)

    def spec(kind, prev):
        if prev:
            return pl.BlockSpec((1, 1, ATTN_BLOCK, ATTN_WIDTH),
                                lambda b, r, i: (b, r, jnp.maximum(nbs * i - 1, 0), kind))
        return pl.BlockSpec(blk, lambda b, r, i: (b, r, i, kind))

    return pl.pallas_call(
        _attn_kernel,
        grid=(B, d, nb),
        in_specs=[spec(0, False), spec(1, True), spec(1, False), spec(2, True), spec(2, False),
                  pl.BlockSpec(bias.shape, lambda b, r, i: (0, 0, 0))],
        out_specs=[pl.BlockSpec(blk, lambda b, r, i: (b, r, i, 0)),
                   pl.BlockSpec(lse_blk, lambda b, r, i: (b, r, i, 0))],
        out_shape=[jax.ShapeDtypeStruct((B, d, L, ATTN_WIDTH), F32),
                   jax.ShapeDtypeStruct((B, d, L, HEAD_DIM), F32)],
        compiler_params=_params("arbitrary", "arbitrary", "arbitrary"),
        name=f"dilated_attn_g{g}",
    )(qkv, qkv, qkv, qkv, qkv, bias)


def _layer_norm(y, g_ref, b_ref):
    mu = jnp.mean(y, axis=-1, keepdims=True)
    yc = y - mu
    var = jnp.mean(yc * yc, axis=-1, keepdims=True)
    return yc * lax.rsqrt(var + LN_EPS) * g_ref[...] + b_ref[...]


def _residual_ln_router(h, x_ref, g_ref, b_ref, rwt_ref, rb_ref,
                        x1_ref, x1p_ref, topi_ref, gate_ref, rank_ref, cnt_ref, carry_ref, first_step):
    xn = _layer_norm(DEEPNORM_ALPHA * x_ref[...] + h, g_ref, b_ref)
    x1_ref[...] = xn
    half = xn.shape[1] // 2
    x1p_ref[...] = _pack_bf16_pair(xn[:, :half], xn[:, half:])
    logits = lax.dot_general(rwt_ref[...], xn.astype(BF16), (((1,), (1,)), ((), ())),
                             preferred_element_type=F32) + rb_ref[...]
    n_exp, tm = logits.shape
    eidx = lax.broadcasted_iota(I32, logits.shape, 0)
    vals, hots = [], []
    for k in range(TOP_K):
        mx = jnp.max(logits, axis=0, keepdims=True)
        idx = jnp.min(jnp.where(logits == mx, eidx, n_exp), axis=0, keepdims=True)
        topi_ref[k:k + 1, :] = idx
        vals.append(mx)
        hit = eidx == idx
        hots.append(hit.astype(F32))
        logits = jnp.where(hit, -jnp.inf, logits)
    exps = [jnp.exp(v - vals[0]) for v in vals]
    denom = exps[0] + exps[1] + exps[2] + exps[3]
    for k in range(TOP_K):
        gate_ref[k:k + 1, :] = exps[k] / denom

    @pl.when(first_step)
    def _():
        carry_ref[...] = jnp.zeros_like(carry_ref)

    tri = (lax.broadcasted_iota(I32, (tm, tm), 0) <= lax.broadcasted_iota(I32, (tm, tm), 1)).astype(BF16)
    cum = jnp.dot(jnp.concatenate(hots, axis=0).astype(BF16), tri, preferred_element_type=F32)
    base = carry_ref[...]
    for k in range(TOP_K):
        cum_k = cum[k * n_exp:(k + 1) * n_exp]
        before = cum_k - hots[k] + base
        rank_ref[k:k + 1, :] = jnp.sum(hots[k] * before, axis=0, keepdims=True).astype(I32)
        base = base + cum_k[:, tm - 1:tm]
    carry_ref[...] = base
    cnt_ref[...] = jnp.broadcast_to(base, cnt_ref.shape)


def _attn_out_kernel(o0_ref, o1_ref, o2_ref, l0_ref, l1_ref, l2_ref, w_ref, x_ref, g_ref, b_ref, rwt_ref, rb_ref,
                     x1_ref, x1p_ref, topi_ref, gate_ref, rank_ref, cnt_ref,
                     o_scr, l_scr, merged_ref, carry_ref, *, tm):
    o_refs = (o0_ref, o1_ref, o2_ref)
    l_refs = (l0_ref, l1_ref, l2_ref)
    H = ATTN_HEADS
    for g, (_, d) in enumerate(DILATED_GROUPS):
        for r in range(d):
            for h in range(H):
                o_scr[g * H + h, pl.ds(r, tm // d, stride=d), :] = o_refs[g][0, r, :, h * HEAD_DIM:(h + 1) * HEAD_DIM]
            l_scr[g, pl.ds(r, tm // d, stride=d), :] = l_refs[g][0, r]
    l0, l1, l2 = l_scr[0], l_scr[1], l_scr[2]
    m = jnp.maximum(jnp.maximum(l0, l1), l2)
    e = [jnp.exp(l0 - m), jnp.exp(l1 - m), jnp.exp(l2 - m)]
    den = e[0] + e[1] + e[2]
    wts = [ei / den for ei in e]
    for h in range(H):
        o = wts[0][:, h:h + 1] * o_scr[h]
        o = o + wts[1][:, h:h + 1] * o_scr[H + h]
        o = o + wts[2][:, h:h + 1] * o_scr[2 * H + h]
        merged_ref[:, h * HEAD_DIM:(h + 1) * HEAD_DIM] = o.astype(BF16)
    h_out = jnp.dot(merged_ref[...], w_ref[...], preferred_element_type=F32)
    _residual_ln_router(h_out, x_ref, g_ref, b_ref, rwt_ref, rb_ref, x1_ref, x1p_ref, topi_ref, gate_ref,
                        rank_ref, cnt_ref, carry_ref, pl.program_id(0) == 0)


def _dn_out_kernel(o_ref, w_ref, x_ref, g_ref, b_ref, rwt_ref, rb_ref,
                   x1_ref, x1p_ref, topi_ref, gate_ref, rank_ref, cnt_ref, carry_ref):
    h = jnp.dot(o_ref[...], w_ref[...], preferred_element_type=F32)
    _residual_ln_router(h, x_ref, g_ref, b_ref, rwt_ref, rb_ref, x1_ref, x1p_ref, topi_ref,
                        gate_ref, rank_ref, cnt_ref, carry_ref, pl.program_id(0) == 0)


def _post_specs(tm, D, E, idx):
    row = lambda *a: (idx(*a), 0)
    const = lambda *a: (0, 0)
    col = lambda *a: (0, idx(*a))
    ins = [pl.BlockSpec((tm, D), row),
           pl.BlockSpec((1, D), const), pl.BlockSpec((1, D), const),
           pl.BlockSpec((E, D), const), pl.BlockSpec((E, 1), const)]
    outs = [pl.BlockSpec((tm, D), row), pl.BlockSpec((tm, D // 2), row),
            pl.BlockSpec((TOP_K, tm), col), pl.BlockSpec((TOP_K, tm), col), pl.BlockSpec((TOP_K, tm), col),
            pl.BlockSpec((E, 128), const)]
    return ins, outs


def _post_shapes(N, D, E):
    return [jax.ShapeDtypeStruct((N, D), F32), jax.ShapeDtypeStruct((N, D // 2), U32),
            jax.ShapeDtypeStruct((TOP_K, N), I32), jax.ShapeDtypeStruct((TOP_K, N), F32),
            jax.ShapeDtypeStruct((TOP_K, N), I32), jax.ShapeDtypeStruct((E, 128), F32)]


def _attn_out(os_, lses, w_out, x2d, ln_g, ln_b, rwt, rb, S):
    N, D = x2d.shape
    E = rwt.shape[0]
    tm = 256
    mt = S // tm
    ins, outs = _post_specs(tm, D, E, lambda m: m)
    grp_specs = []
    for width in (ATTN_WIDTH, HEAD_DIM):
        for _, d in DILATED_GROUPS:
            grp_specs.append(pl.BlockSpec((1, d, tm // d, width), lambda m: (m // mt, 0, m % mt, 0)))
    return pl.pallas_call(
        functools.partial(_attn_out_kernel, tm=tm),
        grid=(N // tm,),
        in_specs=grp_specs + [pl.BlockSpec(w_out.shape, lambda m: (0, 0))] + ins,
        out_specs=outs,
        out_shape=_post_shapes(N, D, E),
        scratch_shapes=[pltpu.VMEM((len(DILATED_GROUPS) * ATTN_HEADS, tm, HEAD_DIM), F32),
                        pltpu.VMEM((len(DILATED_GROUPS), tm, HEAD_DIM), F32),
                        pltpu.VMEM((tm, ATTN_WIDTH), BF16),
                        pltpu.VMEM((E, 1), F32)],
        compiler_params=_params("arbitrary"),
        name="attn_out_ln_router",
    )(*os_, *lses, w_out, x2d, ln_g, ln_b, rwt, rb)


def _dn_out(o, w_out, x2d, ln_g, ln_b, rwt, rb):
    N, D = x2d.shape
    E = rwt.shape[0]
    Kin = o.shape[1]
    tm = 256
    ins, outs = _post_specs(tm, D, E, lambda m: m)
    return pl.pallas_call(
        _dn_out_kernel,
        grid=(N // tm,),
        in_specs=[pl.BlockSpec((tm, Kin), lambda m: (m, 0)),
                  pl.BlockSpec((Kin, D), lambda m: (0, 0), pipeline_mode=pl.Buffered(1))] + ins,
        out_specs=outs,
        out_shape=_post_shapes(N, D, E),
        scratch_shapes=[pltpu.VMEM((E, 1), F32)],
        compiler_params=_params("arbitrary"),
        name="dn_out_ln_router",
    )(o, w_out, x2d, ln_g, ln_b, rwt, rb)


def _slot_layout(topi, rank, counts, rows):
    K, N = topi.shape
    n_exp = counts.shape[0]
    padded = (counts + rows - 1) // rows * rows
    padded_end = jnp.cumsum(padded)
    padded_start = padded_end - padded
    sel = topi[None] == jnp.arange(n_exp, dtype=I32)[:, None, None]
    pos = jnp.sum(jnp.where(sel, padded_start[:, None, None], 0), axis=0) + rank
    n_blocks = -(-(K * N + n_exp * (rows - 1)) // rows)
    n_used = padded_end[-1] // rows
    blk = jnp.arange(n_blocks, dtype=I32)
    be = jnp.sum((blk[:, None] * rows >= padded_end[None, :]).astype(I32), axis=1)
    be = jnp.minimum(be, n_exp - 1)
    be_last = jnp.sum(jnp.where(blk == jnp.maximum(n_used - 1, 0), be, 0))
    used = blk < n_used
    be = jnp.where(used, be, be_last)
    start_b = jnp.sum(jnp.where(be[:, None] == jnp.arange(n_exp)[None, :], padded_start[None, :], 0), axis=1)
    cnt_e = jnp.sum(jnp.where(be[:, None] == jnp.arange(n_exp)[None, :], counts[None, :], 0), axis=1)
    valid = jnp.where(used, jnp.clip(cnt_e - (blk * rows - start_b), 0, rows), 0).astype(I32)
    n_slots = jnp.full((1,), n_blocks * rows, I32)
    pad_lo = jnp.concatenate([padded_start + counts, padded_end[-1:]]).astype(I32)
    pad_hi = jnp.concatenate([padded_end, n_slots]).astype(I32)
    return pos.astype(I32), be.astype(I32), valid, pad_lo, pad_hi, n_blocks


def _tile_major(pos, tm):
    K, N = pos.shape
    return jnp.transpose(pos.reshape(K, N // tm, tm), (1, 0, 2)).reshape(N // tm, K * tm)


def _dispatch_kernel(lo_ref, hi_ref, pos_hbm, x_ref, xs_hbm, idx_smem, zbuf, sem_idx, sem_rows, sem_zero, *, tm):
    i = pl.program_id(0)

    @pl.when(i == 0)
    def _():
        zbuf[...] = jnp.zeros_like(zbuf)
        one = zbuf.at[pl.ds(0, 1)]

        def per_expert(e, carry):
            lo = lo_ref[e]
            hi = hi_ref[e]
            lo_al = jnp.minimum((lo + 7) // 8 * 8, hi)

            def row_start(r, c):
                pltpu.make_async_copy(one, xs_hbm.at[pl.ds(r, 1)], sem_zero).start()
                return c

            def grp_start(q, c):
                pltpu.make_async_copy(zbuf, xs_hbm.at[pl.ds(pl.multiple_of(q * 8, 8), 8)], sem_zero).start()
                return c

            def row_wait(r, c):
                pltpu.make_async_copy(one, xs_hbm.at[pl.ds(0, 1)], sem_zero).wait()
                return c

            def grp_wait(q, c):
                pltpu.make_async_copy(zbuf, xs_hbm.at[pl.ds(0, 8)], sem_zero).wait()
                return c

            lax.fori_loop(lo, lo_al, row_start, 0)
            lax.fori_loop(lo_al // 8, hi // 8, grp_start, 0)
            lax.fori_loop(lo, lo_al, row_wait, 0)
            lax.fori_loop(lo_al // 8, hi // 8, grp_wait, 0)
            return carry

        lax.fori_loop(0, lo_ref.shape[0], per_expert, 0)

    cp = pltpu.make_async_copy(pos_hbm.at[i], idx_smem, sem_idx)
    cp.start()
    cp.wait()

    def issue(jj, carry):
        for u in range(8):
            j = jj * 8 + u
            for k in range(TOP_K):
                dst = idx_smem[k * tm + j]
                pltpu.make_async_copy(x_ref.at[pl.ds(j, 1)], xs_hbm.at[pl.ds(dst, 1)], sem_rows).start()
        return carry

    lax.fori_loop(0, tm // 8, issue, 0)
    for k in range(TOP_K):
        pltpu.make_async_copy(x_ref, xs_hbm.at[pl.ds(0, tm)], sem_rows).wait()


def _dispatch(x1p, pos_tiles, pad_lo, pad_hi, n_slots):
    N, W = x1p.shape
    tm = DISPATCH_TILE
    grid_spec = pltpu.PrefetchScalarGridSpec(
        num_scalar_prefetch=2,
        grid=(N // tm,),
        in_specs=[pl.BlockSpec(memory_space=pl.ANY),
                  pl.BlockSpec((tm, W), lambda i, lo, hi: (i, 0))],
        out_specs=pl.BlockSpec(memory_space=pl.ANY),
        scratch_shapes=[pltpu.SMEM((TOP_K * tm,), I32), pltpu.VMEM((8, W), U32),
                        pltpu.SemaphoreType.DMA, pltpu.SemaphoreType.DMA, pltpu.SemaphoreType.DMA],
    )
    return pl.pallas_call(
        functools.partial(_dispatch_kernel, tm=tm),
        grid_spec=grid_spec,
        out_shape=jax.ShapeDtypeStruct((n_slots, W), U32),
        compiler_params=_params("arbitrary"),
        name="moe_dispatch",
    )(pad_lo, pad_hi, pos_tiles, x1p)


def _ffn_kernel(be_ref, cnt_ref, xs_ref, wg_ref, wl_ref, bgu_ref, wd_ref, bd_ref, o_ref, xb_ref, act_ref,
                *, nfc, ndc, layer, n_exp):
    b = pl.program_id(0)
    p = pl.program_id(1)
    cnt = cnt_ref[b]
    rows = xb_ref.shape[0]
    half = xs_ref.shape[1]
    fc = o_ref.shape[1]
    n_half = rows // MOE_HALF
    up = p < nfc
    expert = layer * n_exp + be_ref[b]
    bg_row = expert * (2 * nfc) + jnp.minimum(p, nfc - 1)
    bd_row = expert * ndc + jnp.maximum(p - nfc, 0)

    @pl.when(jnp.logical_and(cnt > 0, p == 0))
    def _():
        lo, hi = _unpack_bf16_pair(xs_ref[...])
        xb_ref[:, :half] = lo.astype(BF16)
        xb_ref[:, half:] = hi.astype(BF16)

    def up_phase(live):
        wg = wg_ref[0, 0].astype(BF16)
        wl = wl_ref[0, 0].astype(BF16)
        bg = bgu_ref[pl.ds(bg_row, 1), :]
        bl = bgu_ref[pl.ds(bg_row + nfc, 1), :]
        acts = []
        for h in range(live):
            x = xb_ref[h * MOE_HALF:(h + 1) * MOE_HALF, :]
            hg = jnp.dot(x, wg, preferred_element_type=F32) + bg
            hl = jnp.dot(x, wl, preferred_element_type=F32) + bl
            glu = jnp.minimum(hg, SWIGLU_LIMIT)
            lin = jnp.clip(hl, -SWIGLU_LIMIT, SWIGLU_LIMIT)
            acts.append((glu * jax.nn.sigmoid(SWIGLU_ALPHA * glu) * (lin + 1.0)).astype(BF16))
        for c in range(nfc):
            @pl.when(p == c)
            def _():
                for h in range(live):
                    act_ref[h * MOE_HALF:(h + 1) * MOE_HALF, c * fc:(c + 1) * fc] = acts[h]

    def down_phase(live):
        wd = wd_ref[0, 0].astype(BF16) if live else None
        bd = bd_ref[pl.ds(bd_row, 1), :]
        for h in range(live):
            rs = slice(h * MOE_HALF, (h + 1) * MOE_HALF)
            o_ref[rs, :] = jnp.dot(act_ref[rs, :], wd, preferred_element_type=F32) + bd
        for h in range(live, n_half):
            rs = slice(h * MOE_HALF, (h + 1) * MOE_HALF)
            o_ref[rs, :] = jnp.zeros((MOE_HALF, fc), F32)

    for live in range(1, n_half + 1):
        in_range = cnt > (live - 1) * MOE_HALF
        if live < n_half:
            in_range = jnp.logical_and(in_range, cnt <= live * MOE_HALF)

        @pl.when(jnp.logical_and(in_range, up))
        def _():
            up_phase(live)

        @pl.when(jnp.logical_and(in_range, jnp.logical_not(up)))
        def _():
            down_phase(live)

    @pl.when(jnp.logical_and(cnt == 0, jnp.logical_not(up)))
    def _():
        down_phase(0)


def _expert_ffn(xs, be, valid, layer, w_gate_up, b_gate_up, w_down, b_down):
    n_slots, half = xs.shape
    D = 2 * half
    L, E, _, F2 = w_gate_up.shape
    F = F2 // 2
    rows, fc = MOE_ROWS, MOE_FF_CHUNK
    assert F % fc == 0 and D % fc == 0
    nfc = F // fc
    ndc = D // fc
    n_blocks = n_slots // rows

    def up_chunk(b, p, cnt):
        return jnp.where(cnt[b] > 0, jnp.minimum(p, nfc - 1), nfc - 1)

    def down_chunk(b, p, cnt):
        return jnp.where(cnt[b] > 0, jnp.maximum(p - nfc, 0), ndc - 1)

    def weight(height, col):
        return pl.BlockSpec((1, 1, height, fc), lambda b, p, be, cnt: (layer, be[b], 0, col(b, p, cnt)))

    def table(n_rows):
        return pl.BlockSpec((n_rows, fc), lambda b, p, be, cnt: (0, 0))

    def lin_chunk(b, p, cnt):
        return nfc + up_chunk(b, p, cnt)

    grid_spec = pltpu.PrefetchScalarGridSpec(
        num_scalar_prefetch=2,
        grid=(n_blocks, nfc + ndc),
        in_specs=[pl.BlockSpec((rows, half), lambda b, p, be, cnt: (b, 0)),
                  weight(D, up_chunk), weight(D, lin_chunk), table(L * E * 2 * nfc),
                  weight(F, down_chunk), table(L * E * ndc)],
        out_specs=pl.BlockSpec((rows, fc), lambda b, p, be, cnt: (b, jnp.maximum(p - nfc, 0))),
        scratch_shapes=[pltpu.VMEM((rows, D), BF16), pltpu.VMEM((rows, F), BF16)],
    )
    return pl.pallas_call(
        functools.partial(_ffn_kernel, nfc=nfc, ndc=ndc, layer=layer, n_exp=E),
        grid_spec=grid_spec,
        out_shape=jax.ShapeDtypeStruct((n_slots, D), F32),
        compiler_params=_params("arbitrary", "arbitrary"),
        name="expert_ffn",
    )(be, valid, xs, w_gate_up, w_gate_up, b_gate_up.reshape(L * E * 2 * nfc, fc),
      w_down, b_down.reshape(L * E * ndc, fc))


def _combine_kernel(pos_hbm, ys_hbm, gate_ref, x_ref, g_ref, b_ref, x2_ref, x2b_ref,
                    idx0_smem, idx1_smem, buf, sem_idx, sem_rows, *, tm):
    i = pl.program_id(0)
    n_tiles = 2 * pl.num_programs(0)
    idx_smem = (idx0_smem, idx1_smem)
    D = x_ref.shape[1]

    def idx_copy(tile, slot):
        return pltpu.make_async_copy(pos_hbm.at[tile], idx_smem[slot], sem_idx.at[slot])

    def gather(tile, slot):
        idx_copy(tile, slot).wait()

        def issue(jj, carry):
            for u in range(8):
                for k in range(TOP_K):
                    src = idx_smem[slot][k * tm + jj * 8 + u]
                    pltpu.make_async_copy(ys_hbm.at[src >> 3, pl.ds(src & 7, 1)], buf.at[slot, k, jj, pl.ds(u, 1)],
                                          sem_rows.at[slot]).start()
            return carry

        lax.fori_loop(0, tm // 8, issue, 0)

        @pl.when(tile + 2 < n_tiles)
        def _():
            idx_copy(tile + 2, slot).start()

    def reduce(slot):
        rs = slice(slot * tm, (slot + 1) * tm)
        for k in range(TOP_K):
            pltpu.make_async_copy(ys_hbm.at[pl.ds(0, tm // 8)], buf.at[slot, k], sem_rows.at[slot]).wait()
        gate = gate_ref[rs, :]
        f = buf[slot, 0].reshape(tm, D) * gate[:, 0:1]
        for k in range(1, TOP_K):
            f = f + buf[slot, k].reshape(tm, D) * gate[:, k:k + 1]
        xn = _layer_norm(DEEPNORM_ALPHA * x_ref[rs, :] + f, g_ref, b_ref)
        x2_ref[rs, :] = xn
        x2b_ref[rs, :] = xn.astype(BF16)

    @pl.when(i == 0)
    def _():
        idx_copy(0, 0).start()
        idx_copy(1, 1).start()
        gather(0, 0)

    gather(2 * i + 1, 1)
    reduce(0)

    @pl.when(2 * i + 2 < n_tiles)
    def _():
        gather(2 * i + 2, 0)

    reduce(1)


def _combine_ln(ys, pos_tiles, gates_t, x1, ln_g, ln_b):
    N, D = x1.shape
    tm = ROW_TILE
    row = pl.BlockSpec((2 * tm, D), lambda m: (m, 0))
    vec = pl.BlockSpec((1, D), lambda m: (0, 0))
    anyspec = pl.BlockSpec(memory_space=pl.ANY)
    return pl.pallas_call(
        functools.partial(_combine_kernel, tm=tm),
        grid=(N // (2 * tm),),
        in_specs=[anyspec, anyspec, pl.BlockSpec((2 * tm, TOP_K), lambda m: (m, 0)), row, vec, vec],
        out_specs=[row, row],
        out_shape=[jax.ShapeDtypeStruct((N, D), F32), jax.ShapeDtypeStruct((N, D), BF16)],
        scratch_shapes=[pltpu.SMEM((TOP_K * tm,), I32), pltpu.SMEM((TOP_K * tm,), I32),
                        pltpu.VMEM((2, TOP_K, tm // 8, 8, D), F32),
                        pltpu.SemaphoreType.DMA((2,)), pltpu.SemaphoreType.DMA((2,))],
        compiler_params=_params("arbitrary"),
        name="moe_combine_ln",
    )(pos_tiles, ys.reshape(ys.shape[0] // 8, 8, D), gates_t, x1, ln_g, ln_b)


def _moe_block(post, layer, w_gate_up, b_gate_up, w_down, b_down, ln_g, ln_b):
    x1, x1p, topi, gates, rank, cnt = post
    counts = cnt[:, 0].astype(I32)
    pos, be, valid, pad_lo, pad_hi, n_blocks = _slot_layout(topi, rank, counts, MOE_ROWS)
    xs = _dispatch(x1p, _tile_major(pos, DISPATCH_TILE), pad_lo, pad_hi, n_blocks * MOE_ROWS)
    ys = _expert_ffn(xs, be, valid, layer, w_gate_up, b_gate_up, w_down, b_down)
    return _combine_ln(ys, _tile_major(pos, ROW_TILE), jnp.transpose(gates), x1, ln_g, ln_b)


DN_COL_SUB = 256


def _dn_inproj_kernel(x_ref, w_ref, cw_ref, o_ref, pbuf, *, mode, tm, tn, seq_tiles, q_tiles):
    n = pl.program_id(0)
    m = pl.program_id(1)
    if mode != "z":
        @pl.when(m % seq_tiles == 0)
        def _():
            pbuf[0:8, :] = jnp.zeros((8, tn), F32)

    x = x_ref[...]
    sc = jnp.where(n < q_tiles, HEAD_DIM ** -0.5, 1.0).astype(F32)
    for cs in range(tn // DN_COL_SUB):
        cols = slice(cs * DN_COL_SUB, (cs + 1) * DN_COL_SUB)
        p = jnp.dot(x, w_ref[:, cols], preferred_element_type=F32)
        if mode == "z":
            o_ref[:, cols] = p
            continue
        pbuf[8:tm + 8, cols] = p
        cw = cw_ref[:, cols]
        y = cw[0:1] * pbuf[5:tm + 5, cols]
        y = y + cw[1:2] * pbuf[6:tm + 6, cols]
        y = y + cw[2:3] * pbuf[7:tm + 7, cols]
        y = y + cw[3:4] * p
        pbuf[0:8, cols] = pbuf[tm:tm + 8, cols]
        y = y * jax.nn.sigmoid(y)
        if mode == "v":
            o_ref[:, cols] = y
            continue
        for hh in range(DN_COL_SUB // HEAD_DIM):
            sl = slice(hh * HEAD_DIM, (hh + 1) * HEAD_DIM)
            ys = y[:, sl]
            ss = jnp.sum(ys * ys, axis=-1, keepdims=True)
            o_ref[:, cs * DN_COL_SUB + hh * HEAD_DIM:cs * DN_COL_SUB + (hh + 1) * HEAD_DIM] = (
                ys * lax.rsqrt(ss + DN_EPS) * sc)


def _dn_inproj(xb, w, conv_w, seq_len, mode, col0, width):
    N, K = xb.shape
    tm, tn = 512, 2048
    t0 = col0 // tn
    q_w = DN_QK_HEADS * HEAD_DIM
    conv_tiles = conv_w.shape[1] // tn
    kern = functools.partial(_dn_inproj_kernel, mode=mode, tm=tm, tn=tn, seq_tiles=seq_len // tm,
                             q_tiles=q_w // tn if mode == "qk" else 0)
    return pl.pallas_call(
        kern,
        grid=(width // tn, N // tm),
        in_specs=[pl.BlockSpec((tm, K), lambda n, m: (m, 0)),
                  pl.BlockSpec((K, tn), lambda n, m: (0, t0 + n)),
                  pl.BlockSpec((DN_CONV, tn), lambda n, m: (0, jnp.minimum(t0 + n, conv_tiles - 1)))],
        out_specs=pl.BlockSpec((tm, tn), lambda n, m: (m, n)),
        out_shape=jax.ShapeDtypeStruct((N, width), F32),
        scratch_shapes=[pltpu.VMEM((tm + 8, tn), F32)],
        compiler_params=_params("arbitrary", "arbitrary"),
        name=f"dn_inproj_{mode}",
    )(xb, w, conv_w)


def _dn_gates_kernel(x_ref, w_ref, alog_ref, dtb_ref, o_ref):
    r = jnp.dot(x_ref[...], w_ref[...], preferred_element_type=F32)
    lane = lax.broadcasted_iota(I32, r.shape, 1)
    row = lax.broadcasted_iota(I32, r.shape, 0)
    beta = jax.nn.sigmoid(r)
    z = r + dtb_ref[...]
    softplus = jnp.maximum(z, 0.0) + jnp.log1p(jnp.exp(-jnp.abs(z)))
    g = -jnp.exp(alog_ref[...]) * softplus
    in_chunk = row % DN_CHUNK
    s = 1
    while s < DN_CHUNK:
        g = g + jnp.where(in_chunk >= s, pltpu.roll(g, s, axis=0), 0.0)
        s *= 2
    o_ref[...] = jnp.where(lane < DN_V_HEADS, beta, g)


def _dn_gates(xb, w_ba, a_log, dt_bias):
    N, K = xb.shape
    H = DN_V_HEADS
    tm = 512
    lanes = 128
    w_pad = jnp.zeros((K, lanes), BF16).at[:, :2 * H].set(w_ba)
    alog_pad = jnp.zeros((1, lanes), F32).at[0, H:2 * H].set(a_log)
    dtb_pad = jnp.zeros((1, lanes), F32).at[0, H:2 * H].set(dt_bias)
    vec = pl.BlockSpec((1, lanes), lambda m: (0, 0))
    return pl.pallas_call(
        _dn_gates_kernel,
        grid=(N // tm,),
        in_specs=[pl.BlockSpec((tm, K), lambda m: (m, 0)), pl.BlockSpec((K, lanes), lambda m: (0, 0)), vec, vec],
        out_specs=pl.BlockSpec((tm, lanes), lambda m: (m, 0)),
        out_shape=jax.ShapeDtypeStruct((N, lanes), F32),
        compiler_params=_params("arbitrary"),
        name="dn_gates",
    )(xb, w_pad, alog_pad, dtb_pad)


DN_HEADS_PER_STEP = 8
DN_CHUNKS_PER_STEP = 4


def _delta_kernel(q_ref, k_ref, v_ref, z_ref, bg_ref, gtp_ref, ng_ref, o_ref, s_ref):
    C = DN_CHUNK
    HD = HEAD_DIM
    nh, nc = DN_HEADS_PER_STEP, DN_CHUNKS_PER_STEP
    npair = nh // 2
    ngrp = nh // 4
    nt = (((1,), (1,)), ((), ()))

    @pl.when(pl.program_id(2) == 0)
    def _():
        s_ref[...] = jnp.zeros_like(s_ref)

    ri2 = lax.broadcasted_iota(I32, (C, 2 * C), 0)
    li2 = lax.broadcasted_iota(I32, (C, 2 * C), 1)
    left = li2 < C
    ci2 = li2 % C
    incl2 = ri2 >= ci2
    strict2 = ri2 > ci2
    left_row = lax.broadcasted_iota(I32, (1, 2 * C), 1) < C
    ri4 = lax.broadcasted_iota(I32, (C, 4 * C), 0)
    li4 = lax.broadcasted_iota(I32, (C, 4 * C), 1)
    eye4 = (ri4 == li4 % C).astype(F32)
    blk4 = (lax.broadcasted_iota(I32, (4 * C, 4 * C), 0) // C) == (lax.broadcasted_iota(I32, (4 * C, 4 * C), 1) // C)
    zero_hd = jnp.zeros((C, HD), BF16)
    zero_s = jnp.zeros((HD, HD), BF16)

    def block_diag4(x):
        return jnp.where(blk4, jnp.concatenate([x, x, x, x], axis=0), 0.0).astype(BF16)

    b2, lhs2, rhs, qg, glast, bcat, qcat = {}, {}, {}, {}, {}, {}, {}
    for c in range(nc):
        rows = slice(c * C, (c + 1) * C)
        for pj in range(npair):
            csl = slice(pj * HD, (pj + 1) * HD)
            q = q_ref[0, rows, csl]
            k = k_ref[0, rows, csl]
            kk = jnp.concatenate([k, k], axis=0)
            m1 = lax.dot_general(jnp.concatenate([k, q], axis=0).astype(BF16), kk.astype(BF16), nt,
                                 preferred_element_type=F32)
            ja, jb = 2 * pj, 2 * pj + 1
            bcol = [bg_ref[0, 0, rows, j:j + 1] for j in (ja, jb)]
            gcol = [bg_ref[0, 0, rows, nh + j:nh + j + 1] for j in (ja, jb)]
            grow2 = gtp_ref[0, 0, pj:pj + 1, c * 2 * C:(c + 1) * 2 * C]
            gcol2 = jnp.where(left, gcol[0], gcol[1])
            bcol2 = jnp.where(left, bcol[0], bcol[1])
            decay2 = jnp.exp(jnp.where(incl2, gcol2 - grow2, -jnp.inf))
            b2[c, pj] = jnp.where(strict2, -(m1[:C] * bcol2 * decay2), 0.0)
            qk2 = (m1[C:] * decay2).astype(BF16)
            gl = [g[C - 1:C, :] for g in gcol]
            glast2 = jnp.where(left_row, gl[0], gl[1])
            kdec_t2 = (kk.T * jnp.exp(glast2 - grow2)).astype(BF16)
            lhs2[c, pj] = jnp.concatenate([qk2, kdec_t2], axis=0)
            for i, j in enumerate((ja, jb)):
                eg = jnp.exp(gcol[i])
                v = v_ref[0, rows, j * HD:(j + 1) * HD]
                kb = k * bcol[i]
                rhs[c, j] = jnp.concatenate([v * bcol[i], kb * eg], axis=1).astype(BF16)
                qg[c, j] = (q * eg).astype(BF16)
                glast[c, j] = gl[i]

    groups = [(c, g) for c in range(nc) for g in range(ngrp)]
    for c, g in groups:
        bcat[c, g] = jnp.concatenate([b2[c, 2 * g], b2[c, 2 * g + 1]], axis=1)
        qcat[c, g] = eye4 + bcat[c, g]
    for it in groups:
        bcat[it] = jnp.dot(bcat[it].astype(BF16), block_diag4(bcat[it]), preferred_element_type=F32)
    for _ in range(4):
        for it in groups:
            r = jnp.dot(jnp.concatenate([qcat[it], bcat[it]], axis=0).astype(BF16), block_diag4(bcat[it]),
                        preferred_element_type=F32)
            qcat[it] = qcat[it] + r[:C]
            bcat[it] = r[C:]
    uw = {}
    for c, g in groups:
        it = (c, g)
        t_inv = qcat[it] + jnp.dot(qcat[it].astype(BF16), block_diag4(bcat[it]), preferred_element_type=F32)
        rstack = jnp.concatenate([rhs[c, 4 * g + i] for i in range(4)], axis=0)
        r = jnp.dot(block_diag4(t_inv), rstack, preferred_element_type=F32)
        for i in range(4):
            uw[c, 4 * g + i] = r[i * C:(i + 1) * C]

    states = [s_ref[j] for j in range(nh)]
    ng = ng_ref[...]
    for c in range(nc):
        rows = slice(c * C, (c + 1) * C)
        r1 = {}
        for pj in range(npair):
            ja, jb = 2 * pj, 2 * pj + 1
            lhs = jnp.concatenate(
                [jnp.concatenate([uw[c, j][:, HD:].astype(BF16), qg[c, j]], axis=0) for j in (ja, jb)], axis=1)
            sa, sb = states[ja].astype(BF16), states[jb].astype(BF16)
            sbd = jnp.concatenate([jnp.concatenate([sa, zero_s], axis=1),
                                   jnp.concatenate([zero_s, sb], axis=1)], axis=0)
            r1[pj] = jnp.dot(lhs, sbd, preferred_element_type=F32)
        r2 = {}
        for pj in range(npair):
            ja, jb = 2 * pj, 2 * pj + 1
            va = (uw[c, ja][:, :HD] - r1[pj][:C, :HD]).astype(BF16)
            vb = (uw[c, jb][:, :HD] - r1[pj][:C, HD:]).astype(BF16)
            vbd = jnp.concatenate([jnp.concatenate([va, zero_hd], axis=1),
                                   jnp.concatenate([zero_hd, vb], axis=1)], axis=0)
            r2[pj] = jnp.dot(lhs2[c, pj], vbd, preferred_element_type=F32)
        for pj in range(npair):
            for i, j in enumerate((2 * pj, 2 * pj + 1)):
                hs = slice(i * HD, (i + 1) * HD)
                o = r1[pj][C:, hs] + r2[pj][:C, hs]
                states[j] = states[j] * jnp.exp(glast[c, j]) + r2[pj][C:, hs]
                o = o * lax.rsqrt(jnp.mean(o * o, axis=-1, keepdims=True) + DN_EPS) * ng
                zz = z_ref[0, rows, j * HD:(j + 1) * HD]
                o_ref[0, rows, j * HD:(j + 1) * HD] = (o * (zz * jax.nn.sigmoid(zz))).astype(o_ref.dtype)

    for j in range(nh):
        s_ref[j] = states[j]


def _delta_rule(qk, v, z, bg, norm_g, B, S):
    HD = HEAD_DIM
    C = DN_CHUNK
    hps = DN_HEADS_PER_STEP
    T = DN_CHUNKS_PER_STEP * C
    q_w = DN_QK_HEADS * HD
    v_w = DN_V_HEADS * HD
    qkw = hps // 2 * HD
    vw = hps * HD
    n_hg = DN_V_HEADS // hps
    qk3 = qk.reshape(B, S, 2 * q_w)
    v3 = v.reshape(B, S, v_w)
    z3 = z.reshape(B, S, v_w)
    beta = jnp.transpose(bg[:, :DN_V_HEADS].reshape(B, S, n_hg, hps), (0, 2, 1, 3))
    gcum = jnp.transpose(bg[:, DN_V_HEADS:2 * DN_V_HEADS].reshape(B, S, n_hg, hps), (0, 2, 1, 3))
    bg4 = jnp.concatenate([beta, gcum], axis=-1)
    gt4 = jnp.transpose(gcum.reshape(B, n_hg, S // C, C, hps // 2, 2), (0, 1, 4, 2, 5, 3)).reshape(
        B, n_hg, hps // 2, 2 * S)
    return pl.pallas_call(
        _delta_kernel,
        grid=(B, n_hg, S // T),
        in_specs=[pl.BlockSpec((1, T, qkw), lambda b, h, t: (b, t, h)),
                  pl.BlockSpec((1, T, qkw), lambda b, h, t: (b, t, q_w // qkw + h)),
                  pl.BlockSpec((1, T, vw), lambda b, h, t: (b, t, h)),
                  pl.BlockSpec((1, T, vw), lambda b, h, t: (b, t, h)),
                  pl.BlockSpec((1, 1, T, 2 * hps), lambda b, h, t: (b, h, t, 0)),
                  pl.BlockSpec((1, 1, hps // 2, 2 * T), lambda b, h, t: (b, h, 0, t)),
                  pl.BlockSpec((1, HD), lambda b, h, t: (0, 0))],
        out_specs=pl.BlockSpec((1, T, vw), lambda b, h, t: (b, t, h)),
        out_shape=jax.ShapeDtypeStruct((B, S, v_w), BF16),
        scratch_shapes=[pltpu.VMEM((hps, HD, HD), F32)],
        compiler_params=_params("arbitrary", "arbitrary", "arbitrary"),
        name="delta_rule",
    )(qk3, qk3, v3, z3, bg4, gt4, norm_g.reshape(1, HD)).reshape(B * S, v_w)


def kernel(x, rel_bias, attn_w_in, attn_w_out, dn_w_in, dn_conv_w, dn_a_log, dn_dt_bias, dn_norm_g, dn_w_out,
           ln1_g, ln1_b, router_w, router_b, w_gate_up, b_gate_up, w_down, b_down, ln2_g, ln2_b):
    B, S, D = x.shape
    N = B * S
    E = router_w.shape[2]
    x2d = x.reshape(N, D)

    def vec(p):
        return p.reshape(1, -1)

    def router(i):
        return jnp.transpose(router_w[i]).astype(BF16), router_b[i].reshape(E, 1)

    def moe(i, post):
        return _moe_block(post, i, w_gate_up, b_gate_up, w_down, b_down, vec(ln2_g[i]), vec(ln2_b[i]))

    w_in = attn_w_in[0].astype(BF16)
    os_, lses = [], []
    for g, (window, dilation) in enumerate(DILATED_GROUPS):
        assert window // dilation == ATTN_BLOCK
        qkv = _attn_inproj(x2d, w_in, g, dilation, B, S)
        bias = _attn_bias(rel_bias[:, g * ATTN_HEADS:(g + 1) * ATTN_HEADS], dilation)
        o, lse = _dilated_group(qkv, bias, g)
        os_.append(o)
        lses.append(lse)
    rwt, rb = router(0)
    post = _attn_out(os_, lses, attn_w_out[0].astype(BF16), x2d, vec(ln1_g[0]), vec(ln1_b[0]), rwt, rb, S)
    x2, x2b = moe(0, post)

    conv_cols = dn_conv_w.shape[2]
    main_cols = conv_cols + DN_V_HEADS * HEAD_DIM
    w_in = dn_w_in[0].astype(BF16)
    q_w = DN_QK_HEADS * HEAD_DIM
    v_w = DN_V_HEADS * HEAD_DIM
    qk = _dn_inproj(x2b, w_in, dn_conv_w[0], S, "qk", 0, 2 * q_w)
    v = _dn_inproj(x2b, w_in, dn_conv_w[0], S, "v", 2 * q_w, v_w)
    z = _dn_inproj(x2b, w_in, dn_conv_w[0], S, "z", conv_cols, v_w)
    bg = _dn_gates(x2b, w_in[:, main_cols:], dn_a_log[0], dn_dt_bias[0])
    o = _delta_rule(qk, v, z, bg, dn_norm_g[0], B, S)
    rwt, rb = router(1)
    post = _dn_out(o, dn_w_out[0].astype(BF16), x2, vec(ln1_g[1]), vec(ln1_b[1]), rwt, rb)
    x3, _ = moe(1, post)
    return x3.reshape(B, S, D)
```

```python
import functools
import math

import jax
import jax.numpy as jnp
from jax import lax
from jax.experimental import pallas as pl
from jax.experimental.pallas import tpu as pltpu

F32 = jnp.float32
BF16 = jnp.bfloat16
I32 = jnp.int32
U32 = jnp.uint32

DEPTH = 2
DEEPNORM_ALPHA = (2.0 * DEPTH) ** 0.25
LN_EPS = 1e-5
HEAD_DIM = 128
ATTN_HEADS = 8
ATTN_WIDTH = ATTN_HEADS * HEAD_DIM
DILATED_GROUPS = ((128, 1), (512, 4), (2048, 16))
ATTN_BLOCK = 128
MAX_DISTANCE = 2048
DN_QK_HEADS = 16
DN_V_HEADS = 32
DN_CONV = 4
DN_CHUNK = 64
DN_EPS = 1e-6
TOP_K = 4
SWIGLU_LIMIT = 7.0
SWIGLU_ALPHA = 1.702

MOE_ROWS = 1024
MOE_HALF = 512
MOE_FF_CHUNK = 512
ROW_TILE = 256
DISPATCH_TILE = 1024
VMEM_LIMIT = 58 * 1024 * 1024


def _params(*semantics):
    return pltpu.CompilerParams(dimension_semantics=semantics, vmem_limit_bytes=VMEM_LIMIT)


def _pack_bf16_pair(lo, hi):
    lo_bits = lax.bitcast_convert_type(lo.astype(BF16).astype(F32), U32)
    hi_bits = lax.bitcast_convert_type(hi.astype(BF16).astype(F32), U32)
    return (hi_bits & jnp.uint32(0xFFFF0000)) | (lo_bits >> 16)


def _unpack_bf16_pair(u):
    lo = lax.bitcast_convert_type(u << 16, F32)
    hi = lax.bitcast_convert_type(u & jnp.uint32(0xFFFF0000), F32)
    return lo, hi


def _attn_inproj_kernel(a_ref, w_ref, o_ref, acc_ref, *, d, tm):
    res = jnp.dot(a_ref[...].astype(BF16), w_ref[...], preferred_element_type=F32)
    if d == 1:
        o_ref[0, 0] = res.astype(o_ref.dtype)
        return
    lanes = acc_ref.shape[2]
    for cb in range(acc_ref.shape[0]):
        acc_ref[cb] = res[:, cb * lanes:(cb + 1) * lanes]
    for r in range(d):
        for cb in range(acc_ref.shape[0]):
            o_ref[0, r, :, cb * lanes:(cb + 1) * lanes] = (
                acc_ref[cb, pl.ds(r, tm // d, stride=d), :].astype(o_ref.dtype))


def _attn_inproj(x2d, w, g, d, B, S):
    K = x2d.shape[1]
    tm, tn = 512, 1536
    gw = 3 * ATTN_WIDTH
    nt = gw // tn
    mt = S // tm
    return pl.pallas_call(
        functools.partial(_attn_inproj_kernel, d=d, tm=tm),
        grid=(nt, B, mt),
        in_specs=[pl.BlockSpec((tm, K), lambda n, b, m: (b * mt + m, 0)),
                  pl.BlockSpec((K, tn), lambda n, b, m: (0, g * nt + n))],
        out_specs=pl.BlockSpec((1, d, tm // d, tn), lambda n, b, m: (b, 0, m, n)),
        out_shape=jax.ShapeDtypeStruct((B, d, S // d, gw), BF16),
        scratch_shapes=[pltpu.VMEM((tn // HEAD_DIM, tm, HEAD_DIM), F32)],
        compiler_params=_params("arbitrary", "arbitrary", "arbitrary"),
        name=f"attn_inproj_g{g}",
    )(x2d, w)


def _t5_bucket(dist, num_buckets):
    max_exact = num_buckets // 2
    d_f = jnp.maximum(dist, 1).astype(F32)
    large = max_exact + (jnp.log(d_f / max_exact) / math.log(MAX_DISTANCE / max_exact)
                         * (num_buckets - max_exact)).astype(I32)
    large = jnp.minimum(large, num_buckets - 1)
    return jnp.where(dist < max_exact, dist, large)


def _attn_bias(rel_bias_g, dilation):
    bq = ATTN_BLOCK
    a = jnp.arange(bq, dtype=I32)[:, None]
    c = jnp.arange(2 * bq, dtype=I32)[None, :]
    steps = a + bq - c
    valid = (steps >= 0) & (steps <= bq)
    n_buckets = rel_bias_g.shape[0]
    bucket = _t5_bucket(jnp.maximum(steps, 0) * dilation, n_buckets)
    table = jnp.transpose(rel_bias_g).astype(F32)
    bias = jnp.zeros((table.shape[0],) + bucket.shape, F32)
    for b in range(n_buckets):
        bias = jnp.where(bucket[None] == b, table[:, b][:, None, None], bias)
    return jnp.where(valid[None], bias, -jnp.inf)


ATTN_BLOCKS_PER_STEP = 2


def _attn_kernel(q_ref, kp_ref, kc_ref, vp_ref, vc_ref, bias_ref, o_ref, lse_ref):
    bq = ATTN_BLOCK
    nbs = q_ref.shape[2] // bq
    first = pl.program_id(2) == 0
    scale = HEAD_DIM ** -0.5
    nt = (((1,), (1,)), ((), ()))
    lse_ref[0, 0] = jnp.zeros(lse_ref.shape[2:], F32)
    items = [(j, h) for j in range(nbs) for h in range(ATTN_HEADS)]

    def rows(j):
        return slice(j * bq, (j + 1) * bq)

    def cols(h):
        return slice(h * HEAD_DIM, (h + 1) * HEAD_DIM)

    def prev(ref_p, ref_c, j, h):
        return ref_p[0, 0, :, cols(h)] if j == 0 else ref_c[0, 0, rows(j - 1), cols(h)]

    s_p = {(j, h): lax.dot_general(q_ref[0, 0, rows(j), cols(h)], prev(kp_ref, kc_ref, j, h), nt,
                                   preferred_element_type=F32) for j, h in items}
    s_c = {(j, h): lax.dot_general(q_ref[0, 0, rows(j), cols(h)], kc_ref[0, 0, rows(j), cols(h)], nt,
                                   preferred_element_type=F32) for j, h in items}
    p_p, p_c = {}, {}
    for j, h in items:
        sp = s_p[j, h] * scale + bias_ref[h, :, :bq]
        if j == 0:
            sp = jnp.where(first, -jnp.inf, sp)
        sc = s_c[j, h] * scale + bias_ref[h, :, bq:]
        m = jnp.maximum(jnp.max(sp, axis=-1, keepdims=True), jnp.max(sc, axis=-1, keepdims=True))
        e_p = jnp.exp(sp - m)
        e_c = jnp.exp(sc - m)
        l = jnp.sum(e_p, axis=-1, keepdims=True) + jnp.sum(e_c, axis=-1, keepdims=True)
        inv = 1.0 / l
        p_p[j, h] = (e_p * inv).astype(BF16)
        p_c[j, h] = (e_c * inv).astype(BF16)
        lse_ref[0, 0, rows(j), h:h + 1] = m + jnp.log(l)
    for j, h in items:
        o_ref[0, 0, rows(j), cols(h)] = (
            jnp.dot(p_p[j, h], prev(vp_ref, vc_ref, j, h), preferred_element_type=F32)
            + jnp.dot(p_c[j, h], vc_ref[0, 0, rows(j), cols(h)], preferred_element_type=F32))


def _dilated_group(qkv, bias, g):
    B, d, L, _ = qkv.shape
    nbs = min(ATTN_BLOCKS_PER_STEP, L // ATTN_BLOCK)
    assert L % (nbs * ATTN_BLOCK) == 0
    nb = L // (nbs * ATTN_BLOCK)
    blk = (1, 1, nbs * ATTN_BLOCK, ATTN_WIDTH)
    lse_blk = (1, 1, nbs * ATTN_BLOCK, HEAD_DIM)

    def spec(kind, prev):
        if prev:
            return pl.BlockSpec((1, 1, ATTN_BLOCK, ATTN_WIDTH),
                                lambda b, r, i: (b, r, jnp.maximum(nbs * i - 1, 0), kind))
        return pl.BlockSpec(blk, lambda b, r, i: (b, r, i, kind))

    return pl.pallas_call(
        _attn_kernel,
        grid=(B, d, nb),
        in_specs=[spec(0, False), spec(1, True), spec(1, False), spec(2, True), spec(2, False),
                  pl.BlockSpec(bias.shape, lambda b, r, i: (0, 0, 0))],
        out_specs=[pl.BlockSpec(blk, lambda b, r, i: (b, r, i, 0)),
                   pl.BlockSpec(lse_blk, lambda b, r, i: (b, r, i, 0))],
        out_shape=[jax.ShapeDtypeStruct((B, d, L, ATTN_WIDTH), F32),
                   jax.ShapeDtypeStruct((B, d, L, HEAD_DIM), F32)],
        compiler_params=_params("arbitrary", "arbitrary", "arbitrary"),
        name=f"dilated_attn_g{g}",
    )(qkv, qkv, qkv, qkv, qkv, bias)


def _layer_norm(y, g_ref, b_ref):
    mu = jnp.mean(y, axis=-1, keepdims=True)
    yc = y - mu
    var = jnp.mean(yc * yc, axis=-1, keepdims=True)
    return yc * lax.rsqrt(var + LN_EPS) * g_ref[...] + b_ref[...]


def _residual_ln_router(h, x_ref, g_ref, b_ref, rwt_ref, rb_ref,
                        x1_ref, x1p_ref, topi_ref, gate_ref, rank_ref, cnt_ref, carry_ref, first_step):
    xn = _layer_norm(DEEPNORM_ALPHA * x_ref[...] + h, g_ref, b_ref)
    x1_ref[...] = xn
    half = xn.shape[1] // 2
    x1p_ref[...] = _pack_bf16_pair(xn[:, :half], xn[:, half:])
    logits = lax.dot_general(rwt_ref[...], xn.astype(BF16), (((1,), (1,)), ((), ())),
                             preferred_element_type=F32) + rb_ref[...]
    n_exp, tm = logits.shape
    eidx = lax.broadcasted_iota(I32, logits.shape, 0)
    vals, hots = [], []
    for k in range(TOP_K):
        mx = jnp.max(logits, axis=0, keepdims=True)
        idx = jnp.min(jnp.where(logits == mx, eidx, n_exp), axis=0, keepdims=True)
        topi_ref[k:k + 1, :] = idx
        vals.append(mx)
        hit = eidx == idx
        hots.append(hit.astype(F32))
        logits = jnp.where(hit, -jnp.inf, logits)
    exps = [jnp.exp(v - vals[0]) for v in vals]
    denom = exps[0] + exps[1] + exps[2] + exps[3]
    for k in range(TOP_K):
        gate_ref[k:k + 1, :] = exps[k] / denom

    @pl.when(first_step)
    def _():
        carry_ref[...] = jnp.zeros_like(carry_ref)

    tri = (lax.broadcasted_iota(I32, (tm, tm), 0) <= lax.broadcasted_iota(I32, (tm, tm), 1)).astype(BF16)
    cum = jnp.dot(jnp.concatenate(hots, axis=0).astype(BF16), tri, preferred_element_type=F32)
    base = carry_ref[...]
    for k in range(TOP_K):
        cum_k = cum[k * n_exp:(k + 1) * n_exp]
        before = cum_k - hots[k] + base
        rank_ref[k:k + 1, :] = jnp.sum(hots[k] * before, axis=0, keepdims=True).astype(I32)
        base = base + cum_k[:, tm - 1:tm]
    carry_ref[...] = base
    cnt_ref[...] = jnp.broadcast_to(base, cnt_ref.shape)


def _attn_out_kernel(o0_ref, o1_ref, o2_ref, l0_ref, l1_ref, l2_ref, w_ref, x_ref, g_ref, b_ref, rwt_ref, rb_ref,
                     x1_ref, x1p_ref, topi_ref, gate_ref, rank_ref, cnt_ref,
                     o_scr, l_scr, merged_ref, carry_ref, *, tm):
    o_refs = (o0_ref, o1_ref, o2_ref)
    l_refs = (l0_ref, l1_ref, l2_ref)
    H = ATTN_HEADS
    for g, (_, d) in enumerate(DILATED_GROUPS):
        for r in range(d):
            for h in range(H):
                o_scr[g * H + h, pl.ds(r, tm // d, stride=d), :] = o_refs[g][0, r, :, h * HEAD_DIM:(h + 1) * HEAD_DIM]
            l_scr[g, pl.ds(r, tm // d, stride=d), :] = l_refs[g][0, r]
    l0, l1, l2 = l_scr[0], l_scr[1], l_scr[2]
    m = jnp.maximum(jnp.maximum(l0, l1), l2)
    e = [jnp.exp(l0 - m), jnp.exp(l1 - m), jnp.exp(l2 - m)]
    den = e[0] + e[1] + e[2]
    wts = [ei / den for ei in e]
    for h in range(H):
        o = wts[0][:, h:h + 1] * o_scr[h]
        o = o + wts[1][:, h:h + 1] * o_scr[H + h]
        o = o + wts[2][:, h:h + 1] * o_scr[2 * H + h]
        merged_ref[:, h * HEAD_DIM:(h + 1) * HEAD_DIM] = o.astype(BF16)
    h_out = jnp.dot(merged_ref[...], w_ref[...], preferred_element_type=F32)
    _residual_ln_router(h_out, x_ref, g_ref, b_ref, rwt_ref, rb_ref, x1_ref, x1p_ref, topi_ref, gate_ref,
                        rank_ref, cnt_ref, carry_ref, pl.program_id(0) == 0)


def _dn_out_kernel(o_ref, w_ref, x_ref, g_ref, b_ref, rwt_ref, rb_ref,
                   x1_ref, x1p_ref, topi_ref, gate_ref, rank_ref, cnt_ref, carry_ref):
    h = jnp.dot(o_ref[...], w_ref[...], preferred_element_type=F32)
    _residual_ln_router(h, x_ref, g_ref, b_ref, rwt_ref, rb_ref, x1_ref, x1p_ref, topi_ref,
                        gate_ref, rank_ref, cnt_ref, carry_ref, pl.program_id(0) == 0)


def _post_specs(tm, D, E, idx):
    row = lambda *a: (idx(*a), 0)
    const = lambda *a: (0, 0)
    col = lambda *a: (0, idx(*a))
    ins = [pl.BlockSpec((tm, D), row),
           pl.BlockSpec((1, D), const), pl.BlockSpec((1, D), const),
           pl.BlockSpec((E, D), const), pl.BlockSpec((E, 1), const)]
    outs = [pl.BlockSpec((tm, D), row), pl.BlockSpec((tm, D // 2), row),
            pl.BlockSpec((TOP_K, tm), col), pl.BlockSpec((TOP_K, tm), col), pl.BlockSpec((TOP_K, tm), col),
            pl.BlockSpec((E, 128), const)]
    return ins, outs


def _post_shapes(N, D, E):
    return [jax.ShapeDtypeStruct((N, D), F32), jax.ShapeDtypeStruct((N, D // 2), U32),
            jax.ShapeDtypeStruct((TOP_K, N), I32), jax.ShapeDtypeStruct((TOP_K, N), F32),
            jax.ShapeDtypeStruct((TOP_K, N), I32), jax.ShapeDtypeStruct((E, 128), F32)]


def _attn_out(os_, lses, w_out, x2d, ln_g, ln_b, rwt, rb, S):
    N, D = x2d.shape
    E = rwt.shape[0]
    tm = 256
    mt = S // tm
    ins, outs = _post_specs(tm, D, E, lambda m: m)
    grp_specs = []
    for width in (ATTN_WIDTH, HEAD_DIM):
        for _, d in DILATED_GROUPS:
            grp_specs.append(pl.BlockSpec((1, d, tm // d, width), lambda m: (m // mt, 0, m % mt, 0)))
    return pl.pallas_call(
        functools.partial(_attn_out_kernel, tm=tm),
        grid=(N // tm,),
        in_specs=grp_specs + [pl.BlockSpec(w_out.shape, lambda m: (0, 0))] + ins,
        out_specs=outs,
        out_shape=_post_shapes(N, D, E),
        scratch_shapes=[pltpu.VMEM((len(DILATED_GROUPS) * ATTN_HEADS, tm, HEAD_DIM), F32),
                        pltpu.VMEM((len(DILATED_GROUPS), tm, HEAD_DIM), F32),
                        pltpu.VMEM((tm, ATTN_WIDTH), BF16),
                        pltpu.VMEM((E, 1), F32)],
        compiler_params=_params("arbitrary"),
        name="attn_out_ln_router",
    )(*os_, *lses, w_out, x2d, ln_g, ln_b, rwt, rb)


def _dn_out(o, w_out, x2d, ln_g, ln_b, rwt, rb):
    N, D = x2d.shape
    E = rwt.shape[0]
    Kin = o.shape[1]
    tm = 256
    ins, outs = _post_specs(tm, D, E, lambda m: m)
    return pl.pallas_call(
        _dn_out_kernel,
        grid=(N // tm,),
        in_specs=[pl.BlockSpec((tm, Kin), lambda m: (m, 0)),
                  pl.BlockSpec((Kin, D), lambda m: (0, 0), pipeline_mode=pl.Buffered(1))] + ins,
        out_specs=outs,
        out_shape=_post_shapes(N, D, E),
        scratch_shapes=[pltpu.VMEM((E, 1), F32)],
        compiler_params=_params("arbitrary"),
        name="dn_out_ln_router",
    )(o, w_out, x2d, ln_g, ln_b, rwt, rb)


def _slot_layout(topi, rank, counts, rows):
    K, N = topi.shape
    n_exp = counts.shape[0]
    padded = (counts + rows - 1) // rows * rows
    padded_end = jnp.cumsum(padded)
    padded_start = padded_end - padded
    sel = topi[None] == jnp.arange(n_exp, dtype=I32)[:, None, None]
    pos = jnp.sum(jnp.where(sel, padded_start[:, None, None], 0), axis=0) + rank
    n_blocks = -(-(K * N + n_exp * (rows - 1)) // rows)
    n_used = padded_end[-1] // rows
    blk = jnp.arange(n_blocks, dtype=I32)
    be = jnp.sum((blk[:, None] * rows >= padded_end[None, :]).astype(I32), axis=1)
    be = jnp.minimum(be, n_exp - 1)
    be_last = jnp.sum(jnp.where(blk == jnp.maximum(n_used - 1, 0), be, 0))
    used = blk < n_used
    be = jnp.where(used, be, be_last)
    start_b = jnp.sum(jnp.where(be[:, None] == jnp.arange(n_exp)[None, :], padded_start[None, :], 0), axis=1)
    cnt_e = jnp.sum(jnp.where(be[:, None] == jnp.arange(n_exp)[None, :], counts[None, :], 0), axis=1)
    valid = jnp.where(used, jnp.clip(cnt_e - (blk * rows - start_b), 0, rows), 0).astype(I32)
    n_slots = jnp.full((1,), n_blocks * rows, I32)
    pad_lo = jnp.concatenate([padded_start + counts, padded_end[-1:]]).astype(I32)
    pad_hi = jnp.concatenate([padded_end, n_slots]).astype(I32)
    return pos.astype(I32), be.astype(I32), valid, pad_lo, pad_hi, n_blocks


def _tile_major(pos, tm):
    K, N = pos.shape
    return jnp.transpose(pos.reshape(K, N // tm, tm), (1, 0, 2)).reshape(N // tm, K * tm)


def _dispatch_kernel(lo_ref, hi_ref, pos_hbm, x_ref, xs_hbm, idx_smem, zbuf, sem_idx, sem_rows, sem_zero, *, tm):
    i = pl.program_id(0)

    @pl.when(i == 0)
    def _():
        zbuf[...] = jnp.zeros_like(zbuf)
        one = zbuf.at[pl.ds(0, 1)]

        def per_expert(e, carry):
            lo = lo_ref[e]
            hi = hi_ref[e]
            lo_al = jnp.minimum((lo + 7) // 8 * 8, hi)

            def row_start(r, c):
                pltpu.make_async_copy(one, xs_hbm.at[pl.ds(r, 1)], sem_zero).start()
                return c

            def grp_start(q, c):
                pltpu.make_async_copy(zbuf, xs_hbm.at[pl.ds(pl.multiple_of(q * 8, 8), 8)], sem_zero).start()
                return c

            def row_wait(r, c):
                pltpu.make_async_copy(one, xs_hbm.at[pl.ds(0, 1)], sem_zero).wait()
                return c

            def grp_wait(q, c):
                pltpu.make_async_copy(zbuf, xs_hbm.at[pl.ds(0, 8)], sem_zero).wait()
                return c

            lax.fori_loop(lo, lo_al, row_start, 0)
            lax.fori_loop(lo_al // 8, hi // 8, grp_start, 0)
            lax.fori_loop(lo, lo_al, row_wait, 0)
            lax.fori_loop(lo_al // 8, hi // 8, grp_wait, 0)
            return carry

        lax.fori_loop(0, lo_ref.shape[0], per_expert, 0)

    cp = pltpu.make_async_copy(pos_hbm.at[i], idx_smem, sem_idx)
    cp.start()
    cp.wait()

    def issue(jj, carry):
        for u in range(8):
            j = jj * 8 + u
            for k in range(TOP_K):
                dst = idx_smem[k * tm + j]
                pltpu.make_async_copy(x_ref.at[pl.ds(j, 1)], xs_hbm.at[pl.ds(dst, 1)], sem_rows).start()
        return carry

    lax.fori_loop(0, tm // 8, issue, 0)
    for k in range(TOP_K):
        pltpu.make_async_copy(x_ref, xs_hbm.at[pl.ds(0, tm)], sem_rows).wait()


def _dispatch(x1p, pos_tiles, pad_lo, pad_hi, n_slots):
    N, W = x1p.shape
    tm = DISPATCH_TILE
    grid_spec = pltpu.PrefetchScalarGridSpec(
        num_scalar_prefetch=2,
        grid=(N // tm,),
        in_specs=[pl.BlockSpec(memory_space=pl.ANY),
                  pl.BlockSpec((tm, W), lambda i, lo, hi: (i, 0))],
        out_specs=pl.BlockSpec(memory_space=pl.ANY),
        scratch_shapes=[pltpu.SMEM((TOP_K * tm,), I32), pltpu.VMEM((8, W), U32),
                        pltpu.SemaphoreType.DMA, pltpu.SemaphoreType.DMA, pltpu.SemaphoreType.DMA],
    )
    return pl.pallas_call(
        functools.partial(_dispatch_kernel, tm=tm),
        grid_spec=grid_spec,
        out_shape=jax.ShapeDtypeStruct((n_slots, W), U32),
        compiler_params=_params("arbitrary"),
        name="moe_dispatch",
    )(pad_lo, pad_hi, pos_tiles, x1p)


def _ffn_kernel(be_ref, cnt_ref, xs_ref, wg_ref, wl_ref, bgu_ref, wd_ref, bd_ref, o_ref, xb_ref, act_ref,
                *, nfc, ndc, layer, n_exp):
    b = pl.program_id(0)
    p = pl.program_id(1)
    cnt = cnt_ref[b]
    rows = xb_ref.shape[0]
    half = xs_ref.shape[1]
    fc = o_ref.shape[1]
    n_half = rows // MOE_HALF
    up = p < nfc
    expert = layer * n_exp + be_ref[b]
    bg_row = expert * (2 * nfc) + jnp.minimum(p, nfc - 1)
    bd_row = expert * ndc + jnp.maximum(p - nfc, 0)

    @pl.when(jnp.logical_and(cnt > 0, p == 0))
    def _():
        lo, hi = _unpack_bf16_pair(xs_ref[...])
        xb_ref[:, :half] = lo.astype(BF16)
        xb_ref[:, half:] = hi.astype(BF16)

    def up_phase(live):
        wg = wg_ref[0, 0].astype(BF16)
        wl = wl_ref[0, 0].astype(BF16)
        bg = bgu_ref[pl.ds(bg_row, 1), :]
        bl = bgu_ref[pl.ds(bg_row + nfc, 1), :]
        acts = []
        for h in range(live):
            x = xb_ref[h * MOE_HALF:(h + 1) * MOE_HALF, :]
            hg = jnp.dot(x, wg, preferred_element_type=F32) + bg
            hl = jnp.dot(x, wl, preferred_element_type=F32) + bl
            glu = jnp.minimum(hg, SWIGLU_LIMIT)
            lin = jnp.clip(hl, -SWIGLU_LIMIT, SWIGLU_LIMIT)
            acts.append((glu * jax.nn.sigmoid(SWIGLU_ALPHA * glu) * (lin + 1.0)).astype(BF16))
        for c in range(nfc):
            @pl.when(p == c)
            def _():
                for h in range(live):
                    act_ref[h * MOE_HALF:(h + 1) * MOE_HALF, c * fc:(c + 1) * fc] = acts[h]

    def down_phase(live):
        wd = wd_ref[0, 0].astype(BF16) if live else None
        bd = bd_ref[pl.ds(bd_row, 1), :]
        for h in range(live):
            rs = slice(h * MOE_HALF, (h + 1) * MOE_HALF)
            o_ref[rs, :] = jnp.dot(act_ref[rs, :], wd, preferred_element_type=F32) + bd
        for h in range(live, n_half):
            rs = slice(h * MOE_HALF, (h + 1) * MOE_HALF)
            o_ref[rs, :] = jnp.zeros((MOE_HALF, fc), F32)

    for live in range(1, n_half + 1):
        in_range = cnt > (live - 1) * MOE_HALF
        if live < n_half:
            in_range = jnp.logical_and(in_range, cnt <= live * MOE_HALF)

        @pl.when(jnp.logical_and(in_range, up))
        def _():
            up_phase(live)

        @pl.when(jnp.logical_and(in_range, jnp.logical_not(up)))
        def _():
            down_phase(live)

    @pl.when(jnp.logical_and(cnt == 0, jnp.logical_not(up)))
    def _():
        down_phase(0)


def _expert_ffn(xs, be, valid, layer, w_gate_up, b_gate_up, w_down, b_down):
    n_slots, half = xs.shape
    D = 2 * half
    L, E, _, F2 = w_gate_up.shape
    F = F2 // 2
    rows, fc = MOE_ROWS, MOE_FF_CHUNK
    assert F % fc == 0 and D % fc == 0
    nfc = F // fc
    ndc = D // fc
    n_blocks = n_slots // rows

    def up_chunk(b, p, cnt):
        return jnp.where(cnt[b] > 0, jnp.minimum(p, nfc - 1), nfc - 1)

    def down_chunk(b, p, cnt):
        return jnp.where(cnt[b] > 0, jnp.maximum(p - nfc, 0), ndc - 1)

    def weight(height, col):
        return pl.BlockSpec((1, 1, height, fc), lambda b, p, be, cnt: (layer, be[b], 0, col(b, p, cnt)))

    def table(n_rows):
        return pl.BlockSpec((n_rows, fc), lambda b, p, be, cnt: (0, 0))

    def lin_chunk(b, p, cnt):
        return nfc + up_chunk(b, p, cnt)

    grid_spec = pltpu.PrefetchScalarGridSpec(
        num_scalar_prefetch=2,
        grid=(n_blocks, nfc + ndc),
        in_specs=[pl.BlockSpec((rows, half), lambda b, p, be, cnt: (b, 0)),
                  weight(D, up_chunk), weight(D, lin_chunk), table(L * E * 2 * nfc),
                  weight(F, down_chunk), table(L * E * ndc)],
        out_specs=pl.BlockSpec((rows, fc), lambda b, p, be, cnt: (b, jnp.maximum(p - nfc, 0))),
        scratch_shapes=[pltpu.VMEM((rows, D), BF16), pltpu.VMEM((rows, F), BF16)],
    )
    return pl.pallas_call(
        functools.partial(_ffn_kernel, nfc=nfc, ndc=ndc, layer=layer, n_exp=E),
        grid_spec=grid_spec,
        out_shape=jax.ShapeDtypeStruct((n_slots, D), F32),
        compiler_params=_params("arbitrary", "arbitrary"),
        name="expert_ffn",
    )(be, valid, xs, w_gate_up, w_gate_up, b_gate_up.reshape(L * E * 2 * nfc, fc),
      w_down, b_down.reshape(L * E * ndc, fc))


def _combine_kernel(pos_hbm, ys_hbm, gate_ref, x_ref, g_ref, b_ref, x2_ref, x2b_ref,
                    idx0_smem, idx1_smem, buf, sem_idx, sem_rows, *, tm):
    i = pl.program_id(0)
    n_tiles = 2 * pl.num_programs(0)
    idx_smem = (idx0_smem, idx1_smem)
    D = x_ref.shape[1]

    def idx_copy(tile, slot):
        return pltpu.make_async_copy(pos_hbm.at[tile], idx_smem[slot], sem_idx.at[slot])

    def gather(tile, slot):
        idx_copy(tile, slot).wait()

        def issue(jj, carry):
            for u in range(8):
                for k in range(TOP_K):
                    src = idx_smem[slot][k * tm + jj * 8 + u]
                    pltpu.make_async_copy(ys_hbm.at[src >> 3, pl.ds(src & 7, 1)], buf.at[slot, k, jj, pl.ds(u, 1)],
                                          sem_rows.at[slot]).start()
            return carry

        lax.fori_loop(0, tm // 8, issue, 0)

        @pl.when(tile + 2 < n_tiles)
        def _():
            idx_copy(tile + 2, slot).start()

    def reduce(slot):
        rs = slice(slot * tm, (slot + 1) * tm)
        for k in range(TOP_K):
            pltpu.make_async_copy(ys_hbm.at[pl.ds(0, tm // 8)], buf.at[slot, k], sem_rows.at[slot]).wait()
        gate = gate_ref[rs, :]
        f = buf[slot, 0].reshape(tm, D) * gate[:, 0:1]
        for k in range(1, TOP_K):
            f = f + buf[slot, k].reshape(tm, D) * gate[:, k:k + 1]
        xn = _layer_norm(DEEPNORM_ALPHA * x_ref[rs, :] + f, g_ref, b_ref)
        x2_ref[rs, :] = xn
        x2b_ref[rs, :] = xn.astype(BF16)

    @pl.when(i == 0)
    def _():
        idx_copy(0, 0).start()
        idx_copy(1, 1).start()
        gather(0, 0)

    gather(2 * i + 1, 1)
    reduce(0)

    @pl.when(2 * i + 2 < n_tiles)
    def _():
        gather(2 * i + 2, 0)

    reduce(1)


def _combine_ln(ys, pos_tiles, gates_t, x1, ln_g, ln_b):
    N, D = x1.shape
    tm = ROW_TILE
    row = pl.BlockSpec((2 * tm, D), lambda m: (m, 0))
    vec = pl.BlockSpec((1, D), lambda m: (0, 0))
    anyspec = pl.BlockSpec(memory_space=pl.ANY)
    return pl.pallas_call(
        functools.partial(_combine_kernel, tm=tm),
        grid=(N // (2 * tm),),
        in_specs=[anyspec, anyspec, pl.BlockSpec((2 * tm, TOP_K), lambda m: (m, 0)), row, vec, vec],
        out_specs=[row, row],
        out_shape=[jax.ShapeDtypeStruct((N, D), F32), jax.ShapeDtypeStruct((N, D), BF16)],
        scratch_shapes=[pltpu.SMEM((TOP_K * tm,), I32), pltpu.SMEM((TOP_K * tm,), I32),
                        pltpu.VMEM((2, TOP_K, tm // 8, 8, D), F32),
                        pltpu.SemaphoreType.DMA((2,)), pltpu.SemaphoreType.DMA((2,))],
        compiler_params=_params("arbitrary"),
        name="moe_combine_ln",
    )(pos_tiles, ys.reshape(ys.shape[0] // 8, 8, D), gates_t, x1, ln_g, ln_b)


def _moe_block(post, layer, w_gate_up, b_gate_up, w_down, b_down, ln_g, ln_b):
    x1, x1p, topi, gates, rank, cnt = post
    counts = cnt[:, 0].astype(I32)
    pos, be, valid, pad_lo, pad_hi, n_blocks = _slot_layout(topi, rank, counts, MOE_ROWS)
    xs = _dispatch(x1p, _tile_major(pos, DISPATCH_TILE), pad_lo, pad_hi, n_blocks * MOE_ROWS)
    ys = _expert_ffn(xs, be, valid, layer, w_gate_up, b_gate_up, w_down, b_down)
    return _combine_ln(ys, _tile_major(pos, ROW_TILE), jnp.transpose(gates), x1, ln_g, ln_b)


DN_COL_SUB = 256


def _dn_inproj_kernel(x_ref, w_ref, cw_ref, o_ref, pbuf, *, mode, tm, tn, seq_tiles, q_tiles):
    n = pl.program_id(0)
    m = pl.program_id(1)
    if mode != "z":
        @pl.when(m % seq_tiles == 0)
        def _():
            pbuf[0:8, :] = jnp.zeros((8, tn), F32)

    x = x_ref[...]
    sc = jnp.where(n < q_tiles, HEAD_DIM ** -0.5, 1.0).astype(F32)
    for cs in range(tn // DN_COL_SUB):
        cols = slice(cs * DN_COL_SUB, (cs + 1) * DN_COL_SUB)
        p = jnp.dot(x, w_ref[:, cols], preferred_element_type=F32)
        if mode == "z":
            o_ref[:, cols] = p
            continue
        pbuf[8:tm + 8, cols] = p
        cw = cw_ref[:, cols]
        y = cw[0:1] * pbuf[5:tm + 5, cols]
        y = y + cw[1:2] * pbuf[6:tm + 6, cols]
        y = y + cw[2:3] * pbuf[7:tm + 7, cols]
        y = y + cw[3:4] * p
        pbuf[0:8, cols] = pbuf[tm:tm + 8, cols]
        y = y * jax.nn.sigmoid(y)
        if mode == "v":
            o_ref[:, cols] = y
            continue
        for hh in range(DN_COL_SUB // HEAD_DIM):
            sl = slice(hh * HEAD_DIM, (hh + 1) * HEAD_DIM)
            ys = y[:, sl]
            ss = jnp.sum(ys * ys, axis=-1, keepdims=True)
            o_ref[:, cs * DN_COL_SUB + hh * HEAD_DIM:cs * DN_COL_SUB + (hh + 1) * HEAD_DIM] = (
                ys * lax.rsqrt(ss + DN_EPS) * sc)


def _dn_inproj(xb, w, conv_w, seq_len, mode, col0, width):
    N, K = xb.shape
    tm, tn = 512, 2048
    t0 = col0 // tn
    q_w = DN_QK_HEADS * HEAD_DIM
    conv_tiles = conv_w.shape[1] // tn
    kern = functools.partial(_dn_inproj_kernel, mode=mode, tm=tm, tn=tn, seq_tiles=seq_len // tm,
                             q_tiles=q_w // tn if mode == "qk" else 0)
    return pl.pallas_call(
        kern,
        grid=(width // tn, N // tm),
        in_specs=[pl.BlockSpec((tm, K), lambda n, m: (m, 0)),
                  pl.BlockSpec((K, tn), lambda n, m: (0, t0 + n)),
                  pl.BlockSpec((DN_CONV, tn), lambda n, m: (0, jnp.minimum(t0 + n, conv_tiles - 1)))],
        out_specs=pl.BlockSpec((tm, tn), lambda n, m: (m, n)),
        out_shape=jax.ShapeDtypeStruct((N, width), F32),
        scratch_shapes=[pltpu.VMEM((tm + 8, tn), F32)],
        compiler_params=_params("arbitrary", "arbitrary"),
        name=f"dn_inproj_{mode}",
    )(xb, w, conv_w)


def _dn_gates_kernel(x_ref, w_ref, alog_ref, dtb_ref, o_ref):
    r = jnp.dot(x_ref[...], w_ref[...], preferred_element_type=F32)
    lane = lax.broadcasted_iota(I32, r.shape, 1)
    row = lax.broadcasted_iota(I32, r.shape, 0)
    beta = jax.nn.sigmoid(r)
    z = r + dtb_ref[...]
    softplus = jnp.maximum(z, 0.0) + jnp.log1p(jnp.exp(-jnp.abs(z)))
    g = -jnp.exp(alog_ref[...]) * softplus
    in_chunk = row % DN_CHUNK
    s = 1
    while s < DN_CHUNK:
        g = g + jnp.where(in_chunk >= s, pltpu.roll(g, s, axis=0), 0.0)
        s *= 2
    o_ref[...] = jnp.where(lane < DN_V_HEADS, beta, g)


def _dn_gates(xb, w_ba, a_log, dt_bias):
    N, K = xb.shape
    H = DN_V_HEADS
    tm = 512
    lanes = 128
    w_pad = jnp.zeros((K, lanes), BF16).at[:, :2 * H].set(w_ba)
    alog_pad = jnp.zeros((1, lanes), F32).at[0, H:2 * H].set(a_log)
    dtb_pad = jnp.zeros((1, lanes), F32).at[0, H:2 * H].set(dt_bias)
    vec = pl.BlockSpec((1, lanes), lambda m: (0, 0))
    return pl.pallas_call(
        _dn_gates_kernel,
        grid=(N // tm,),
        in_specs=[pl.BlockSpec((tm, K), lambda m: (m, 0)), pl.BlockSpec((K, lanes), lambda m: (0, 0)), vec, vec],
        out_specs=pl.BlockSpec((tm, lanes), lambda m: (m, 0)),
        out_shape=jax.ShapeDtypeStruct((N, lanes), F32),
        compiler_params=_params("arbitrary"),
        name="dn_gates",
    )(xb, w_pad, alog_pad, dtb_pad)


DN_HEADS_PER_STEP = 8
DN_CHUNKS_PER_STEP = 4


def _delta_kernel(q_ref, k_ref, v_ref, z_ref, bg_ref, gtp_ref, ng_ref, o_ref, s_ref):
    C = DN_CHUNK
    HD = HEAD_DIM
    nh, nc = DN_HEADS_PER_STEP, DN_CHUNKS_PER_STEP
    npair = nh // 2
    ngrp = nh // 4
    nt = (((1,), (1,)), ((), ()))

    @pl.when(pl.program_id(2) == 0)
    def _():
        s_ref[...] = jnp.zeros_like(s_ref)

    ri2 = lax.broadcasted_iota(I32, (C, 2 * C), 0)
    li2 = lax.broadcasted_iota(I32, (C, 2 * C), 1)
    left = li2 < C
    ci2 = li2 % C
    incl2 = ri2 >= ci2
    strict2 = ri2 > ci2
    left_row = lax.broadcasted_iota(I32, (1, 2 * C), 1) < C
    ri4 = lax.broadcasted_iota(I32, (C, 4 * C), 0)
    li4 = lax.broadcasted_iota(I32, (C, 4 * C), 1)
    eye4 = (ri4 == li4 % C).astype(F32)
    blk4 = (lax.broadcasted_iota(I32, (4 * C, 4 * C), 0) // C) == (lax.broadcasted_iota(I32, (4 * C, 4 * C), 1) // C)
    zero_hd = jnp.zeros((C, HD), BF16)
    zero_s = jnp.zeros((HD, HD), BF16)

    def block_diag4(x):
        return jnp.where(blk4, jnp.concatenate([x, x, x, x], axis=0), 0.0).astype(BF16)

    b2, lhs2, rhs, qg, glast, bcat, qcat = {}, {}, {}, {}, {}, {}, {}
    for c in range(nc):
        rows = slice(c * C, (c + 1) * C)
        for pj in range(npair):
            csl = slice(pj * HD, (pj + 1) * HD)
            q = q_ref[0, rows, csl]
            k = k_ref[0, rows, csl]
            kk = jnp.concatenate([k, k], axis=0)
            m1 = lax.dot_general(jnp.concatenate([k, q], axis=0).astype(BF16), kk.astype(BF16), nt,
                                 preferred_element_type=F32)
            ja, jb = 2 * pj, 2 * pj + 1
            bcol = [bg_ref[0, 0, rows, j:j + 1] for j in (ja, jb)]
            gcol = [bg_ref[0, 0, rows, nh + j:nh + j + 1] for j in (ja, jb)]
            grow2 = gtp_ref[0, 0, pj:pj + 1, c * 2 * C:(c + 1) * 2 * C]
            gcol2 = jnp.where(left, gcol[0], gcol[1])
            bcol2 = jnp.where(left, bcol[0], bcol[1])
            decay2 = jnp.exp(jnp.where(incl2, gcol2 - grow2, -jnp.inf))
            b2[c, pj] = jnp.where(strict2, -(m1[:C] * bcol2 * decay2), 0.0)
            qk2 = (m1[C:] * decay2).astype(BF16)
            gl = [g[C - 1:C, :] for g in gcol]
            glast2 = jnp.where(left_row, gl[0], gl[1])
            kdec_t2 = (kk.T * jnp.exp(glast2 - grow2)).astype(BF16)
            lhs2[c, pj] = jnp.concatenate([qk2, kdec_t2], axis=0)
            for i, j in enumerate((ja, jb)):
                eg = jnp.exp(gcol[i])
                v = v_ref[0, rows, j * HD:(j + 1) * HD]
                kb = k * bcol[i]
                rhs[c, j] = jnp.concatenate([v * bcol[i], kb * eg], axis=1).astype(BF16)
                qg[c, j] = (q * eg).astype(BF16)
                glast[c, j] = gl[i]

    groups = [(c, g) for c in range(nc) for g in range(ngrp)]
    for c, g in groups:
        bcat[c, g] = jnp.concatenate([b2[c, 2 * g], b2[c, 2 * g + 1]], axis=1)
        qcat[c, g] = eye4 + bcat[c, g]
    for it in groups:
        bcat[it] = jnp.dot(bcat[it].astype(BF16), block_diag4(bcat[it]), preferred_element_type=F32)
    for _ in range(4):
        for it in groups:
            r = jnp.dot(jnp.concatenate([qcat[it], bcat[it]], axis=0).astype(BF16), block_diag4(bcat[it]),
                        preferred_element_type=F32)
            qcat[it] = qcat[it] + r[:C]
            bcat[it] = r[C:]
    uw = {}
    for c, g in groups:
        it = (c, g)
        t_inv = qcat[it] + jnp.dot(qcat[it].astype(BF16), block_diag4(bcat[it]), preferred_element_type=F32)
        rstack = jnp.concatenate([rhs[c, 4 * g + i] for i in range(4)], axis=0)
        r = jnp.dot(block_diag4(t_inv), rstack, preferred_element_type=F32)
        for i in range(4):
            uw[c, 4 * g + i] = r[i * C:(i + 1) * C]

    states = [s_ref[j] for j in range(nh)]
    ng = ng_ref[...]
    for c in range(nc):
        rows = slice(c * C, (c + 1) * C)
        r1 = {}
        for pj in range(npair):
            ja, jb = 2 * pj, 2 * pj + 1
            lhs = jnp.concatenate(
                [jnp.concatenate([uw[c, j][:, HD:].astype(BF16), qg[c, j]], axis=0) for j in (ja, jb)], axis=1)
            sa, sb = states[ja].astype(BF16), states[jb].astype(BF16)
            sbd = jnp.concatenate([jnp.concatenate([sa, zero_s], axis=1),
                                   jnp.concatenate([zero_s, sb], axis=1)], axis=0)
            r1[pj] = jnp.dot(lhs, sbd, preferred_element_type=F32)
        r2 = {}
        for pj in range(npair):
            ja, jb = 2 * pj, 2 * pj + 1
            va = (uw[c, ja][:, :HD] - r1[pj][:C, :HD]).astype(BF16)
            vb = (uw[c, jb][:, :HD] - r1[pj][:C, HD:]).astype(BF16)
            vbd = jnp.concatenate([jnp.concatenate([va, zero_hd], axis=1),
                                   jnp.concatenate([zero_hd, vb], axis=1)], axis=0)
            r2[pj] = jnp.dot(lhs2[c, pj], vbd, preferred_element_type=F32)
        for pj in range(npair):
            for i, j in enumerate((2 * pj, 2 * pj + 1)):
                hs = slice(i * HD, (i + 1) * HD)
                o = r1[pj][C:, hs] + r2[pj][:C, hs]
                states[j] = states[j] * jnp.exp(glast[c, j]) + r2[pj][C:, hs]
                o = o * lax.rsqrt(jnp.mean(o * o, axis=-1, keepdims=True) + DN_EPS) * ng
                zz = z_ref[0, rows, j * HD:(j + 1) * HD]
                o_ref[0, rows, j * HD:(j + 1) * HD] = (o * (zz * jax.nn.sigmoid(zz))).astype(o_ref.dtype)

    for j in range(nh):
        s_ref[j] = states[j]


def _delta_rule(qk, v, z, bg, norm_g, B, S):
    HD = HEAD_DIM
    C = DN_CHUNK
    hps = DN_HEADS_PER_STEP
    T = DN_CHUNKS_PER_STEP * C
    q_w = DN_QK_HEADS * HD
    v_w = DN_V_HEADS * HD
    qkw = hps // 2 * HD
    vw = hps * HD
    n_hg = DN_V_HEADS // hps
    qk3 = qk.reshape(B, S, 2 * q_w)
    v3 = v.reshape(B, S, v_w)
    z3 = z.reshape(B, S, v_w)
    beta = jnp.transpose(bg[:, :DN_V_HEADS].reshape(B, S, n_hg, hps), (0, 2, 1, 3))
    gcum = jnp.transpose(bg[:, DN_V_HEADS:2 * DN_V_HEADS].reshape(B, S, n_hg, hps), (0, 2, 1, 3))
    bg4 = jnp.concatenate([beta, gcum], axis=-1)
    gt4 = jnp.transpose(gcum.reshape(B, n_hg, S // C, C, hps // 2, 2), (0, 1, 4, 2, 5, 3)).reshape(
        B, n_hg, hps // 2, 2 * S)
    return pl.pallas_call(
        _delta_kernel,
        grid=(B, n_hg, S // T),
        in_specs=[pl.BlockSpec((1, T, qkw), lambda b, h, t: (b, t, h)),
                  pl.BlockSpec((1, T, qkw), lambda b, h, t: (b, t, q_w // qkw + h)),
                  pl.BlockSpec((1, T, vw), lambda b, h, t: (b, t, h)),
                  pl.BlockSpec((1, T, vw), lambda b, h, t: (b, t, h)),
                  pl.BlockSpec((1, 1, T, 2 * hps), lambda b, h, t: (b, h, t, 0)),
                  pl.BlockSpec((1, 1, hps // 2, 2 * T), lambda b, h, t: (b, h, 0, t)),
                  pl.BlockSpec((1, HD), lambda b, h, t: (0, 0))],
        out_specs=pl.BlockSpec((1, T, vw), lambda b, h, t: (b, t, h)),
        out_shape=jax.ShapeDtypeStruct((B, S, v_w), BF16),
        scratch_shapes=[pltpu.VMEM((hps, HD, HD), F32)],
        compiler_params=_params("arbitrary", "arbitrary", "arbitrary"),
        name="delta_rule",
    )(qk3, qk3, v3, z3, bg4, gt4, norm_g.reshape(1, HD)).reshape(B * S, v_w)


def kernel(x, rel_bias, attn_w_in, attn_w_out, dn_w_in, dn_conv_w, dn_a_log, dn_dt_bias, dn_norm_g, dn_w_out,
           ln1_g, ln1_b, router_w, router_b, w_gate_up, b_gate_up, w_down, b_down, ln2_g, ln2_b):
    B, S, D = x.shape
    N = B * S
    E = router_w.shape[2]
    x2d = x.reshape(N, D)

    def vec(p):
        return p.reshape(1, -1)

    def router(i):
        return jnp.transpose(router_w[i]).astype(BF16), router_b[i].reshape(E, 1)

    def moe(i, post):
        return _moe_block(post, i, w_gate_up, b_gate_up, w_down, b_down, vec(ln2_g[i]), vec(ln2_b[i]))

    w_in = attn_w_in[0].astype(BF16)
    os_, lses = [], []
    for g, (window, dilation) in enumerate(DILATED_GROUPS):
        assert window // dilation == ATTN_BLOCK
        qkv = _attn_inproj(x2d, w_in, g, dilation, B, S)
        bias = _attn_bias(rel_bias[:, g * ATTN_HEADS:(g + 1) * ATTN_HEADS], dilation)
        o, lse = _dilated_group(qkv, bias, g)
        os_.append(o)
        lses.append(lse)
    rwt, rb = router(0)
    post = _attn_out(os_, lses, attn_w_out[0].astype(BF16), x2d, vec(ln1_g[0]), vec(ln1_b[0]), rwt, rb, S)
    x2, x2b = moe(0, post)

    conv_cols = dn_conv_w.shape[2]
    main_cols = conv_cols + DN_V_HEADS * HEAD_DIM
    w_in = dn_w_in[0].astype(BF16)
    q_w = DN_QK_HEADS * HEAD_DIM
    v_w = DN_V_HEADS * HEAD_DIM
    qk = _dn_inproj(x2b, w_in, dn_conv_w[0], S, "qk", 0, 2 * q_w)
    v = _dn_inproj(x2b, w_in, dn_conv_w[0], S, "v", 2 * q_w, v_w)
    z = _dn_inproj(x2b, w_in, dn_conv_w[0], S, "z", conv_cols, v_w)
    bg = _dn_gates(x2b, w_in[:, main_cols:], dn_a_log[0], dn_dt_bias[0])
    o = _delta_rule(qk, v, z, bg, dn_norm_g[0], B, S)
    rwt, rb = router(1)
    post = _dn_out(o, dn_w_out[0].astype(BF16), x2, vec(ln1_g[1]), vec(ln1_b[1]), rwt, rb)
    x3, _ = moe(1, post)
    return x3.reshape(B, S, D)
```
